```python
import math
import jax
import jax.numpy as jnp
from jax import lax
import numpy as np

D_MODEL = 1024
BATCH = 16
SEQ = 2048
DEPTH = 2

PLE_DIM = 256
D_FF = 2816
N_BRANCH = 4
BRANCH_WIDTH = 256
CONV_WIDTH = 4
NORM_EPS = 1e-6

GDN_HEADS = 4
GDN_DK = 64
GDN_DV = 64
GDN_CHUNK = 64

MLSTM_HEADS = 4
MLSTM_DK = 64
MLSTM_DV = 64
MLSTM_CHUNK = 64

S5_GROUPS = 16
S5_GROUP_CH = 16
S5_STATE = 64

MOBA_HEADS = 4
MOBA_HEAD_DIM = 64
MOBA_BLOCK = 256
MOBA_TOPK = 3
MOBA_Q_SUB = 32
ROPE_THETA = 10000.0

IN_SPLITS = (
    2 * GDN_HEADS * GDN_DK + GDN_HEADS * GDN_DV,
    GDN_HEADS * GDN_DV,
    GDN_HEADS,
    GDN_HEADS,
    2 * MLSTM_HEADS * MLSTM_DK,
    MLSTM_HEADS * MLSTM_DV,
    MLSTM_HEADS * MLSTM_DV,
    MLSTM_HEADS,
    MLSTM_HEADS,
    S5_GROUPS * S5_GROUP_CH,
    3 * MOBA_HEADS * MOBA_HEAD_DIM,
)
N_IN = sum(IN_SPLITS)

kernel_name = 'hybrid_gated_parallel_mixer_block'


def split_cols(t, sizes):
    cuts = [int(c) for c in np.cumsum(sizes)[:-1]]
    return jnp.split(t, cuts, axis=-1)


def rms_norm(x, w):
    xf = x.astype(jnp.float32)
    y = xf * lax.rsqrt(jnp.mean(xf * xf, axis=-1, keepdims=True) + NORM_EPS)
    return (y * w.astype(jnp.float32)).astype(x.dtype)


def l2_normalize(x):
    return x * lax.rsqrt(jnp.sum(x * x, axis=-1, keepdims=True) + NORM_EPS)


def causal_conv(x, w):
    k_w, c = w.shape
    return lax.conv_general_dilated(
        x, w[:, None, :].astype(x.dtype), window_strides=(1,), padding=[(k_w - 1, 0)],
        dimension_numbers=('NWC', 'WIO', 'NWC'), feature_group_count=c)


def rope(x, positions):
    dh = x.shape[-1]
    inv_freq = ROPE_THETA ** (-jnp.arange(0, dh, 2, dtype=jnp.float32) / dh)
    ang = positions.astype(jnp.float32)[..., None] * inv_freq
    cos = jnp.cos(ang)[:, :, None, :]
    sin = jnp.sin(ang)[:, :, None, :]
    xf = x.astype(jnp.float32)
    x1, x2 = jnp.split(xf, 2, axis=-1)
    return jnp.concatenate([x1 * cos - x2 * sin, x2 * cos + x1 * sin], axis=-1).astype(x.dtype)


def swiglu(u, w_gu, w_down):
    gate, up = jnp.split(u @ w_gu, 2, axis=-1)
    return (jax.nn.silu(gate) * up) @ w_down


def to_chunks(t, chunk):
    b, s = t.shape[:2]
    t = t.reshape((b, s // chunk, chunk) + t.shape[2:])
    return jnp.moveaxis(t, 3, 1)


def gated_deltanet(qkv, z, b_pre, a_pre, conv_w, a_log, dt_bias, norm_w):
    bsz, s, _ = qkv.shape
    h, dk, dv, L = GDN_HEADS, GDN_DK, GDN_DV, GDN_CHUNK
    f32 = jnp.float32
    qkv = jax.nn.silu(causal_conv(qkv, conv_w)).astype(f32)
    q, k, v = split_cols(qkv, (h * dk, h * dk, h * dv))
    q = l2_normalize(q.reshape(bsz, s, h, dk)) * (dk ** -0.5)
    k = l2_normalize(k.reshape(bsz, s, h, dk))
    v = v.reshape(bsz, s, h, dv)
    beta = jax.nn.sigmoid(b_pre.astype(f32))
    g = -jnp.exp(a_log.astype(f32)) * jax.nn.softplus(a_pre.astype(f32) + dt_bias.astype(f32))
    q, k, v, beta, g = (to_chunks(t, L) for t in (q, k, v, beta, g))
    gc = jnp.cumsum(g, axis=-1)
    causal = jnp.tril(jnp.ones((L, L), dtype=bool))
    strict = jnp.tril(jnp.ones((L, L), dtype=bool), -1)
    decay = jnp.exp(jnp.where(causal, gc[..., :, None] - gc[..., None, :], -jnp.inf))
    k_beta = k * beta[..., None]
    m_low = jnp.where(strict, jnp.einsum('bhnid,bhnjd->bhnij', k_beta, k) * decay, 0.0)
    t_mat = jnp.eye(L, dtype=f32) + m_low
    u_vals = lax.linalg.triangular_solve(t_mat, v * beta[..., None], left_side=True, lower=True)
    w_keys = lax.linalg.triangular_solve(t_mat, k_beta * jnp.exp(gc)[..., None], left_side=True, lower=True)
    qk = jnp.einsum('bhnid,bhnjd->bhnij', q, k) * decay
    q_dec = q * jnp.exp(gc)[..., None]
    k_dec = k * jnp.exp(gc[..., -1:] - gc)[..., None]
    chunk_decay = jnp.exp(gc[..., -1])

    def step(state, inp):
        qk_c, q_c, k_c, u_c, w_c, d_c = inp
        v_new = u_c - jnp.einsum('bhld,bhde->bhle', w_c, state)
        out = jnp.einsum('bhld,bhde->bhle', q_c, state) + jnp.einsum('bhlm,bhme->bhle', qk_c, v_new)
        state = state * d_c[..., None, None] + jnp.einsum('bhld,bhle->bhde', k_c, v_new)
        return state, out

    xs = tuple(jnp.moveaxis(t, 2, 0) for t in (qk, q_dec, k_dec, u_vals, w_keys, chunk_decay))
    _, o = lax.scan(step, jnp.zeros((bsz, h, dk, dv), f32), xs)
    o = o.transpose(1, 0, 3, 2, 4).reshape(bsz, s, h, dv)
    o = rms_norm(o, norm_w) * jax.nn.silu(z.astype(f32).reshape(bsz, s, h, dv))
    return o.reshape(bsz, s, h * dv).astype(z.dtype)


def mlstm(qk, v, o_pre, i_pre, f_pre, conv_w, i_bias, f_bias, norm_w):
    bsz, s, _ = v.shape
    h, dk, dv, L = MLSTM_HEADS, MLSTM_DK, MLSTM_DV, MLSTM_CHUNK
    f32 = jnp.float32
    qk = jax.nn.silu(causal_conv(qk, conv_w)).astype(f32)
    q, k = jnp.split(qk, 2, axis=-1)
    q = q.reshape(bsz, s, h, dk)
    k = k.reshape(bsz, s, h, dk) * (dk ** -0.5)
    vv = v.astype(f32).reshape(bsz, s, h, dv)
    log_i = i_pre.astype(f32) + i_bias.astype(f32)
    log_f = jax.nn.log_sigmoid(f_pre.astype(f32) + f_bias.astype(f32))
    q, k, vv, log_i, log_f = (to_chunks(t, L) for t in (q, k, vv, log_i, log_f))
    b = jnp.cumsum(log_f, axis=-1)
    causal = jnp.tril(jnp.ones((L, L), dtype=bool))
    d_intra = jnp.where(causal, b[..., :, None] - b[..., None, :] + log_i[..., None, :], -jnp.inf)
    a_key = b[..., -1:] - b + log_i
    m_key = jnp.max(a_key, axis=-1)
    g_chunk = b[..., -1]

    def step(carry, inp):
        c_mem, n_mem, m_mem = carry
        k_c, v_c, a_c, mk_c, g_c = inp
        m_new = jnp.maximum(g_c + m_mem, mk_c)
        carry_scale = jnp.exp(g_c + m_mem - m_new)
        k_w = k_c * jnp.exp(a_c - m_new[..., None])[..., None]
        c_new = c_mem * carry_scale[..., None, None] + jnp.einsum('bhld,bhle->bhde', k_w, v_c)
        n_new = n_mem * carry_scale[..., None] + jnp.sum(k_w, axis=-2)
        return (c_new, n_new, m_new), (c_mem, n_mem, m_mem)

    xs = tuple(jnp.moveaxis(t, 2, 0) for t in (k, vv, a_key, m_key, g_chunk))
    init = (jnp.zeros((bsz, h, dk, dv), f32), jnp.zeros((bsz, h, dk), f32), jnp.zeros((bsz, h), f32))
    _, (c_prev, n_prev, m_prev) = lax.scan(step, init, xs)
    c_prev = jnp.moveaxis(c_prev, 0, 2)
    n_prev = jnp.moveaxis(n_prev, 0, 2)
    m_prev = jnp.moveaxis(m_prev, 0, 2)
    m_inter = b + m_prev[..., None]
    m_t = jnp.maximum(m_inter, jnp.max(d_intra, axis=-1))
    w_inter = jnp.exp(m_inter - m_t)
    s_qk = jnp.einsum('bhnld,bhnmd->bhnlm', q, k) * jnp.exp(d_intra - m_t[..., None])
    num = (w_inter[..., None] * jnp.einsum('bhnld,bhnde->bhnle', q, c_prev)
           + jnp.einsum('bhnlm,bhnme->bhnle', s_qk, vv))
    qn = w_inter * jnp.einsum('bhnld,bhnd->bhnl', q, n_prev) + jnp.sum(s_qk, axis=-1)
    h_tilde = num / jnp.maximum(jnp.abs(qn), jnp.exp(-m_t))[..., None]
    h_tilde = h_tilde.transpose(0, 2, 3, 1, 4).reshape(bsz, s, h, dv)
    h_norm = rms_norm(h_tilde, norm_w.reshape(h, dv)).reshape(bsz, s, h * dv)
    return (jax.nn.sigmoid(o_pre.astype(f32)) * h_norm).astype(v.dtype)


def s5_ssm(u, lam_re, lam_im, b_re, b_im, c_re, c_im, d_skip, log_dt, w_glu, b_glu):
    bsz, s, _ = u.shape
    G, N, P = S5_GROUPS, S5_GROUP_CH, S5_STATE
    f32 = jnp.float32
    uf = u.astype(f32).reshape(bsz, s, G, N)
    dt = jnp.exp(log_dt.astype(f32))[:, None]
    lr, li = lam_re.astype(f32), lam_im.astype(f32)
    mag = jnp.exp(lr * dt)
    a_re = mag * jnp.cos(li * dt)
    a_im = mag * jnp.sin(li * dt)
    den = lr * lr + li * li
    z_re = ((a_re - 1.0) * lr + a_im * li) / den
    z_im = (a_im * lr - (a_re - 1.0) * li) / den
    br, bi = b_re.astype(f32), b_im.astype(f32)
    bb_re = z_re[..., None] * br - z_im[..., None] * bi
    bb_im = z_re[..., None] * bi + z_im[..., None] * br
    bu_re = jnp.einsum('bsgn,gpn->bsgp', uf, bb_re)
    bu_im = jnp.einsum('bsgn,gpn->bsgp', uf, bb_im)
    a_re_s = jnp.broadcast_to(a_re, (1, s, G, P))
    a_im_s = jnp.broadcast_to(a_im, (1, s, G, P))

    def combine(left, right):
        ar1, ai1, br1, bi1 = left
        ar2, ai2, br2, bi2 = right
        return (ar2 * ar1 - ai2 * ai1, ar2 * ai1 + ai2 * ar1,
                ar2 * br1 - ai2 * bi1 + br2, ar2 * bi1 + ai2 * br1 + bi2)

    _, _, x_re, x_im = lax.associative_scan(combine, (a_re_s, a_im_s, bu_re, bu_im), axis=1)
    y = (jnp.einsum('gnp,bsgp->bsgn', c_re.astype(f32), x_re)
         - jnp.einsum('gnp,bsgp->bsgn', c_im.astype(f32), x_im)
         + d_skip.astype(f32).reshape(G, N) * uf)
    y = jax.nn.gelu(y.reshape(bsz, s, G * N))
    y = y * jax.nn.sigmoid(y @ w_glu.astype(f32) + b_glu.astype(f32))
    return y.astype(u.dtype)


def moba_attention(qkv, positions):
    bsz, s, _ = qkv.shape
    h, dh, blk, qs = MOBA_HEADS, MOBA_HEAD_DIM, MOBA_BLOCK, MOBA_Q_SUB
    f32 = jnp.float32
    q, k, v = jnp.split(qkv, 3, axis=-1)
    q = rope(q.reshape(bsz, s, h, dh), positions).transpose(0, 2, 1, 3)
    k = rope(k.reshape(bsz, s, h, dh), positions).transpose(0, 2, 1, 3)
    v = v.reshape(bsz, s, h, dh).transpose(0, 2, 1, 3)
    nb = -(-s // blk)
    pad = nb * blk - s
    k_blocks = jnp.pad(k, ((0, 0), (0, 0), (0, pad), (0, 0))).reshape(bsz, h, nb, blk, dh)
    v_blocks = jnp.pad(v, ((0, 0), (0, 0), (0, pad), (0, 0))).reshape(bsz, h, nb, blk, dh)
    k_mean = jnp.mean(k_blocks.astype(f32), axis=3)
    q_block = jnp.arange(s) // blk
    gate = jnp.einsum('bhsd,bhnd->bhsn', q.astype(f32), k_mean)
    past = jnp.arange(nb)[None, :] < q_block[:, None]
    gate = jnp.where(past, gate, -jnp.inf)
    n_sel = min(MOBA_TOPK, nb)
    _, sel = lax.top_k(gate, n_sel)
    sel_ok = sel < q_block[:, None]
    scale = dh ** -0.5
    bi = jnp.arange(bsz)[:, None, None, None]
    hi = jnp.arange(h)[None, :, None, None]

    def attend(sub):
        start = sub * qs
        q_s = lax.dynamic_slice_in_dim(q, start, qs, axis=2)
        sel_s = lax.dynamic_slice_in_dim(sel, start, qs, axis=2)
        ok_s = lax.dynamic_slice_in_dim(sel_ok, start, qs, axis=2)
        k_sel = k_blocks[bi, hi, sel_s]
        v_sel = v_blocks[bi, hi, sel_s]
        own = start // blk
        k_own = lax.dynamic_index_in_dim(k_blocks, own, axis=2, keepdims=False)
        v_own = lax.dynamic_index_in_dim(v_blocks, own, axis=2, keepdims=False)
        s_sel = jnp.einsum('bhqd,bhqnkd->bhqnk', q_s, k_sel).astype(f32) * scale
        s_sel = jnp.where(ok_s[..., None], s_sel, -jnp.inf)
        s_own = jnp.einsum('bhqd,bhkd->bhqk', q_s, k_own).astype(f32) * scale
        q_pos = start + jnp.arange(qs)
        k_pos = own * blk + jnp.arange(blk)
        s_own = jnp.where(k_pos[None, :] <= q_pos[:, None], s_own, -jnp.inf)
        scores = jnp.concatenate([s_sel.reshape(bsz, h, qs, n_sel * blk), s_own], axis=-1)
        probs = jax.nn.softmax(scores, axis=-1).astype(v.dtype)
        p_sel = probs[..., :n_sel * blk].reshape(bsz, h, qs, n_sel, blk)
        p_own = probs[..., n_sel * blk:]
        return (jnp.einsum('bhqnk,bhqnkd->bhqd', p_sel, v_sel)
                + jnp.einsum('bhqk,bhkd->bhqd', p_own, v_own))

    out = lax.map(attend, jnp.arange(s // qs))
    return out.transpose(1, 0, 3, 2, 4).reshape(bsz, s, h * dh)


def token_mixing(u, positions, w_in, gdn_conv, gdn_a_log, gdn_dt_bias, gdn_norm,
                 mlstm_conv, mlstm_i_bias, mlstm_f_bias, mlstm_norm,
                 s5_lambda_re, s5_lambda_im, s5_b_re, s5_b_im, s5_c_re, s5_c_im, s5_d, s5_log_dt,
                 s5_w_glu, s5_b_glu, w_gate, w_branch, w_out):
    (gdn_qkv, gdn_z, gdn_b, gdn_a, ml_qk, ml_v, ml_o, ml_i, ml_f, s5_u, moba_qkv) = split_cols(u @ w_in, IN_SPLITS)
    y_gdn = gated_deltanet(gdn_qkv, gdn_z, gdn_b, gdn_a, gdn_conv, gdn_a_log, gdn_dt_bias, gdn_norm)
    y_mlstm = mlstm(ml_qk, ml_v, ml_o, ml_i, ml_f, mlstm_conv, mlstm_i_bias, mlstm_f_bias, mlstm_norm)
    y_s5 = s5_ssm(s5_u, s5_lambda_re, s5_lambda_im, s5_b_re, s5_b_im, s5_c_re, s5_c_im, s5_d, s5_log_dt,
                  s5_w_glu, s5_b_glu)
    y_moba = moba_attention(moba_qkv, positions)
    ys = (y_gdn, y_mlstm, y_s5, y_moba)
    merged = jax.nn.sigmoid(u @ w_gate[0]) * (ys[0] @ w_branch[0])
    for n in range(1, N_BRANCH):
        merged = merged + jax.nn.sigmoid(u @ w_gate[n]) * (ys[n] @ w_branch[n])
    return merged @ w_out


def setup_inputs(seed: int = 0) -> dict:
    key = jax.random.key(seed)
    ks = iter(jax.random.split(key, 64))
    f32 = jnp.float32
    L, D = DEPTH, D_MODEL

    def nrm(shape, scale):
        return jax.random.normal(next(ks), shape, f32) * scale

    def gain(shape):
        return 1.0 + nrm(shape, 0.02)

    def log_uniform(shape, lo, hi):
        return jax.random.uniform(next(ks), shape, f32, minval=math.log(lo), maxval=math.log(hi))

    x = nrm((BATCH, SEQ, D), 1.0)
    p = nrm((DEPTH, BATCH, SEQ, PLE_DIM), 1.0)
    offset = jax.random.randint(next(ks), (BATCH, 1), 0, 4096, dtype=jnp.int32)
    positions = offset + jnp.arange(SEQ, dtype=jnp.int32)[None, :]
    G, N, P = S5_GROUPS, S5_GROUP_CH, S5_STATE
    gdn_dt = jnp.exp(log_uniform((L, GDN_HEADS), 1e-3, 1e-1))
    return {
        'x': x,
        'p': p,
        'positions': positions,
        'ffn1_norm': gain((L, D)),
        'ffn1_w_gu': nrm((L, D, 2 * D_FF), D ** -0.5),
        'ffn1_w_down': nrm((L, D_FF, D), D_FF ** -0.5),
        'mix_norm': gain((L, D)),
        'w_in': nrm((L, D, N_IN), D ** -0.5),
        'gdn_conv': nrm((L, CONV_WIDTH, IN_SPLITS[0]), CONV_WIDTH ** -0.5),
        'gdn_a_log': jnp.log(jax.random.uniform(next(ks), (L, GDN_HEADS), f32, minval=1.0, maxval=16.0)),
        'gdn_dt_bias': gdn_dt + jnp.log(-jnp.expm1(-gdn_dt)),
        'gdn_norm': gain((L, GDN_DV)),
        'mlstm_conv': nrm((L, CONV_WIDTH, IN_SPLITS[4]), CONV_WIDTH ** -0.5),
        'mlstm_i_bias': nrm((L, MLSTM_HEADS), 0.1),
        'mlstm_f_bias': jnp.linspace(3.0, 6.0, MLSTM_HEADS, dtype=f32)[None, :] + nrm((L, MLSTM_HEADS), 0.1),
        'mlstm_norm': gain((L, MLSTM_HEADS * MLSTM_DV)),
        's5_lambda_re': -0.5 + nrm((L, G, P), 0.01),
        's5_lambda_im': jnp.pi * jnp.arange(P, dtype=f32) + nrm((L, G, P), 0.01),
        's5_b_re': nrm((L, G, P, N), (2 * N) ** -0.5),
        's5_b_im': nrm((L, G, P, N), (2 * N) ** -0.5),
        's5_c_re': nrm((L, G, N, P), (2 * P) ** -0.5),
        's5_c_im': nrm((L, G, N, P), (2 * P) ** -0.5),
        's5_d': nrm((L, G * N), 0.5),
        's5_log_dt': log_uniform((L, G), 1e-3, 1e-1),
        's5_w_glu': nrm((L, G * N, G * N), (G * N) ** -0.5),
        's5_b_glu': nrm((L, G * N), 0.01),
        'w_gate': nrm((L, N_BRANCH, D, D), D ** -0.5),
        'w_branch': nrm((L, N_BRANCH, BRANCH_WIDTH, D), BRANCH_WIDTH ** -0.5),
        'w_out': nrm((L, D, D), D ** -0.5),
        'ffn2_norm': gain((L, D)),
        'ffn2_w_gu': nrm((L, D, 2 * D_FF), D ** -0.5),
        'ffn2_w_down': nrm((L, D_FF, D), D_FF ** -0.5),
        'ple_norm': gain((L, D)),
        'ple_w_proj': nrm((L, PLE_DIM, D), PLE_DIM ** -0.5),
        'ple_w_gate': nrm((L, D, D), D ** -0.5),
        'final_norm': gain((D,)),
    }


def reference(x, p, positions, ffn1_norm, ffn1_w_gu, ffn1_w_down, mix_norm, w_in,
              gdn_conv, gdn_a_log, gdn_dt_bias, gdn_norm,
              mlstm_conv, mlstm_i_bias, mlstm_f_bias, mlstm_norm,
              s5_lambda_re, s5_lambda_im, s5_b_re, s5_b_im, s5_c_re, s5_c_im, s5_d, s5_log_dt,
              s5_w_glu, s5_b_glu, w_gate, w_branch, w_out,
              ffn2_norm, ffn2_w_gu, ffn2_w_down, ple_norm, ple_w_proj, ple_w_gate, final_norm):
    h = x
    for i in range(DEPTH):
        h = h + 0.5 * swiglu(rms_norm(h, ffn1_norm[i]), ffn1_w_gu[i], ffn1_w_down[i])
        h = h + token_mixing(
            rms_norm(h, mix_norm[i]), positions, w_in[i],
            gdn_conv[i], gdn_a_log[i], gdn_dt_bias[i], gdn_norm[i],
            mlstm_conv[i], mlstm_i_bias[i], mlstm_f_bias[i], mlstm_norm[i],
            s5_lambda_re[i], s5_lambda_im[i], s5_b_re[i], s5_b_im[i], s5_c_re[i], s5_c_im[i],
            s5_d[i], s5_log_dt[i], s5_w_glu[i], s5_b_glu[i],
            w_gate[i], w_branch[i], w_out[i])
        h = h + 0.5 * swiglu(rms_norm(h, ffn2_norm[i]), ffn2_w_gu[i], ffn2_w_down[i])
        h = h + (p[i] @ ple_w_proj[i]) * jax.nn.sigmoid(rms_norm(h, ple_norm[i]) @ ple_w_gate[i])
    return rms_norm(h, final_norm)
```

```python
import functools
import math

import jax
import jax.numpy as jnp
from jax import lax
from jax.experimental import pallas as pl
from jax.experimental.pallas import tpu as pltpu

F32 = jnp.float32
BF16 = jnp.bfloat16
HIGHEST = lax.Precision.HIGHEST

NORM_EPS = 1e-6
HEADS = 4
HEAD_DIM = 64
BRANCH = HEADS * HEAD_DIM
CHUNK = 64
CONV_WIDTH = 4
CONV_HIST = 8
S5_GROUPS = 16
S5_CH = 16
S5_STATE = 64
S5_FLAT = CHUNK * S5_CH
MOBA_BLOCK = 256
MOBA_TOPK = 3
ROPE_THETA = 10000.0
SMALL_W = 128
V7X_VMEM_LIMIT = 56 * 1024 * 1024

NT_DIMS = (((1,), (1,)), ((), ()))


def _cparams(*sem):
    return pltpu.CompilerParams(dimension_semantics=sem, vmem_limit_bytes=V7X_VMEM_LIMIT)


def _rms(x, w):
    return x * lax.rsqrt(jnp.mean(x * x, axis=-1, keepdims=True) + NORM_EPS) * w


def _silu(x):
    return x * jax.nn.sigmoid(x)


def _softplus(x):
    return jnp.maximum(x, 0.0) + jnp.log1p(jnp.exp(-jnp.abs(x)))


def _dot(a, b):
    return jnp.dot(a, b, preferred_element_type=F32)


def _dot_bf(a, b):
    return jnp.dot(a.astype(BF16), b.astype(BF16), preferred_element_type=F32)


def _dot_nt_bf(a, b):
    return lax.dot_general(a.astype(BF16), b.astype(BF16), NT_DIMS, preferred_element_type=F32)


def _dot_hi(a, b):
    return jnp.dot(a, b, precision=HIGHEST, preferred_element_type=F32)


def _dot_nt_hi(a, b):
    return lax.dot_general(a, b, NT_DIMS, precision=HIGHEST, preferred_element_type=F32)


def _iota(shape, dim):
    return lax.broadcasted_iota(jnp.int32, shape, dim)


def _head_bcast(cols):
    rows = cols.shape[0]
    head = _iota((rows, BRANCH), 1) // HEAD_DIM
    out = jnp.broadcast_to(cols[:, 0:1], (rows, BRANCH))
    for h in range(1, HEADS):
        out = jnp.where(head == h, jnp.broadcast_to(cols[:, h:h + 1], (rows, BRANCH)), out)
    return out


def _causal_conv_silu(x_ref, w_ref, pad_ref, first):
    rows = x_ref.shape[1]

    @pl.when(first)
    def _():
        pad_ref[0:CONV_HIST, :] = jnp.zeros((CONV_HIST, pad_ref.shape[1]), F32)

    pad_ref[CONV_HIST:CONV_HIST + rows, :] = x_ref[0]
    w = w_ref[...]
    acc = w[CONV_WIDTH - 1:CONV_WIDTH, :] * pad_ref[CONV_HIST:CONV_HIST + rows, :]
    for back in range(1, CONV_WIDTH):
        tap = CONV_WIDTH - 1 - back
        acc = acc + w[tap:tap + 1, :] * pad_ref[CONV_HIST - back:CONV_HIST - back + rows, :]
    pad_ref[0:CONV_HIST, :] = pad_ref[rows:rows + CONV_HIST, :]
    return _silu(acc)


def _tri_consts():
    r = _iota((CHUNK, CHUNK), 0)
    c = _iota((CHUNK, CHUNK), 1)
    lower = (c <= r).astype(F32)
    upper_s = (c > r).astype(F32)
    ones = jnp.ones((CHUNK, CHUNK), F32)
    return jnp.concatenate([lower, upper_s, ones], axis=0)


def _ffn_kernel(x_ref, nw_ref, wg_ref, wu_ref, wd_ref, o_ref, xn_ref):
    j = pl.program_id(1)

    @pl.when(j == 0)
    def _():
        xn_ref[...] = _rms(x_ref[...], nw_ref[...]).astype(BF16)

    xn = xn_ref[...]
    gate = _dot(xn, wg_ref[...])
    up = _dot(xn, wu_ref[...])
    part = 0.5 * _dot((_silu(gate) * up).astype(BF16), wd_ref[...])

    @pl.when(j == 0)
    def _():
        o_ref[...] = x_ref[...] + part

    @pl.when(j > 0)
    def _():
        o_ref[...] += part


def _ffn(h, norm_w, w_gu, w_down, tm, tf):
    t, d = h.shape
    f = w_down.shape[0]
    nf = f // tf
    return pl.pallas_call(
        _ffn_kernel,
        out_shape=jax.ShapeDtypeStruct((t, d), F32),
        grid=(t // tm, nf),
        in_specs=[
            pl.BlockSpec((tm, d), lambda i, j: (i, 0)),
            pl.BlockSpec((1, d), lambda i, j: (0, 0)),
            pl.BlockSpec((d, tf), lambda i, j: (0, j)),
            pl.BlockSpec((d, tf), lambda i, j, nf=nf: (0, j + nf)),
            pl.BlockSpec((tf, d), lambda i, j: (j, 0)),
        ],
        out_specs=pl.BlockSpec((tm, d), lambda i, j: (i, 0)),
        scratch_shapes=[pltpu.VMEM((tm, d), BF16)],
        compiler_params=_cparams("parallel", "arbitrary"),
        name="ffn",
    )(h, norm_w, w_gu, w_gu, w_down)


IN_SEGMENTS = (3 * BRANCH, BRANCH, 2 * BRANCH, BRANCH, BRANCH, BRANCH, 3 * BRANCH, SMALL_W)


def _inproj_kernel(x_ref, nw_ref, w_ref, *o_refs):
    xn = _rms(x_ref[...], nw_ref[...]).astype(BF16)
    start = 0
    for o_ref, width in zip(o_refs, IN_SEGMENTS):
        o_ref[...] = _dot(xn, w_ref[:, start:start + width])
        start += width


def _inproj(h, norm_w, w_perm, tm):
    t, d = h.shape
    n = w_perm.shape[1]
    return pl.pallas_call(
        _inproj_kernel,
        out_shape=[jax.ShapeDtypeStruct((t, w), F32) for w in IN_SEGMENTS],
        grid=(t // tm,),
        in_specs=[
            pl.BlockSpec((tm, d), lambda i: (i, 0)),
            pl.BlockSpec((1, d), lambda i: (0, 0)),
            pl.BlockSpec((d, n), lambda i: (0, 0)),
        ],
        out_specs=[pl.BlockSpec((tm, w), lambda i: (i, 0)) for w in IN_SEGMENTS],
        compiler_params=_cparams("parallel"),
        name="inproj",
    )(h, norm_w, w_perm)


def _merge_kernel(h_ref, nw_ref, y0_ref, y1_ref, y2_ref, y3_ref, wg_ref, wb_ref, wo_ref, o_ref):
    h = h_ref[...]
    u = _rms(h, nw_ref[...]).astype(BF16)
    merged = None
    for b, y_ref in enumerate((y0_ref, y1_ref, y2_ref, y3_ref)):
        term = jax.nn.sigmoid(_dot(u, wg_ref[b])) * _dot(y_ref[...].astype(BF16), wb_ref[b])
        merged = term if merged is None else merged + term
    o_ref[...] = h + _dot(merged.astype(BF16), wo_ref[...])


def _merge(h, norm_w, ys, w_gate, w_branch, w_out, tm):
    t, d = h.shape
    nb, bw, _ = w_branch.shape
    return pl.pallas_call(
        _merge_kernel,
        out_shape=jax.ShapeDtypeStruct((t, d), F32),
        grid=(t // tm,),
        in_specs=[
            pl.BlockSpec((tm, d), lambda i: (i, 0)),
            pl.BlockSpec((1, d), lambda i: (0, 0)),
        ] + [pl.BlockSpec((tm, bw), lambda i: (i, 0)) for _ in range(nb)] + [
            pl.BlockSpec((nb, d, d), lambda i: (0, 0, 0)),
            pl.BlockSpec((nb, bw, d), lambda i: (0, 0, 0)),
            pl.BlockSpec((d, d), lambda i: (0, 0)),
        ],
        out_specs=pl.BlockSpec((tm, d), lambda i: (i, 0)),
        compiler_params=_cparams("parallel"),
        name="merge",
    )(h, norm_w, *ys, w_gate, w_branch, w_out)


def _ple_kernel(h_ref, nw_ref, p_ref, wp_ref, wg_ref, fw_ref, o_ref, *, final):
    h = h_ref[...]
    gate = jax.nn.sigmoid(_dot(_rms(h, nw_ref[...]).astype(BF16), wg_ref[...]))
    out = h + _dot(p_ref[...].astype(BF16), wp_ref[...]) * gate
    if final:
        out = _rms(out, fw_ref[...])
    o_ref[...] = out


def _ple(h, norm_w, p, w_proj, w_gate, final_w, final, tm):
    t, d = h.shape
    pd = p.shape[1]
    return pl.pallas_call(
        functools.partial(_ple_kernel, final=final),
        out_shape=jax.ShapeDtypeStruct((t, d), F32),
        grid=(t // tm,),
        in_specs=[
            pl.BlockSpec((tm, d), lambda i: (i, 0)),
            pl.BlockSpec((1, d), lambda i: (0, 0)),
            pl.BlockSpec((tm, pd), lambda i: (i, 0)),
            pl.BlockSpec((pd, d), lambda i: (0, 0)),
            pl.BlockSpec((d, d), lambda i: (0, 0)),
            pl.BlockSpec((1, d), lambda i: (0, 0)),
        ],
        out_specs=pl.BlockSpec((tm, d), lambda i: (i, 0)),
        compiler_params=_cparams("parallel"),
        name="ple",
    )(h, norm_w, p, w_proj, w_gate, final_w)


def _gdn_kernel(qkv_ref, z_ref, sm_ref, cw_ref, alog_ref, dtb_ref, nw_ref, o_ref, pad_ref, state_ref):
    c = pl.program_id(1)

    @pl.when(c == 0)
    def _():
        state_ref[...] = jnp.zeros(state_ref.shape, F32)

    x = _causal_conv_silu(qkv_ref, cw_ref, pad_ref, c == 0)
    sm = sm_ref[0]
    beta = jax.nn.sigmoid(sm[:, 0:HEADS])
    g = -jnp.exp(alog_ref[...]) * _softplus(sm[:, HEADS:2 * HEADS] + dtb_ref[...])
    beta_b = _head_bcast(beta)
    g_b = _head_bcast(g)

    row = _iota((CHUNK, BRANCH), 0)
    col = _iota((CHUNK, BRANCH), 1) % HEAD_DIM
    strict = row > col
    sums = _dot_hi(_tri_consts(), jnp.concatenate([g_b, jnp.where(strict, g_b, 0.0)], axis=1))
    gc = sums[0:CHUNK, 0:BRANCH]
    rest = sums[CHUNK:2 * CHUNK, 0:BRANCH]
    total = sums[2 * CHUNK:3 * CHUNK, 0:BRANCH]
    diff = sums[0:CHUNK, BRANCH:2 * BRANCH]
    decay = jnp.where(row >= col, jnp.exp(diff), 0.0)
    e_gc = jnp.exp(gc)
    e_rest = jnp.exp(rest)
    e_total = jnp.exp(total)

    eye = (_iota((CHUNK, CHUNK), 0) == _iota((CHUNK, CHUNK), 1)).astype(F32)
    z = z_ref[0]
    nw = nw_ref[...]
    outs = []
    for h in range(HEADS):
        lo, hi = h * HEAD_DIM, (h + 1) * HEAD_DIM
        q = x[:, lo:hi]
        k = x[:, BRANCH + lo:BRANCH + hi]
        v = x[:, 2 * BRANCH + lo:2 * BRANCH + hi]
        q = q * lax.rsqrt(jnp.sum(q * q, axis=-1, keepdims=True) + NORM_EPS) * (HEAD_DIM ** -0.5)
        k = k * lax.rsqrt(jnp.sum(k * k, axis=-1, keepdims=True) + NORM_EPS)
        bh = beta_b[:, lo:hi]
        dec = decay[:, lo:hi]
        k_beta = k * bh
        gram = _dot_nt_bf(jnp.concatenate([k_beta, q], axis=0), k)
        m = jnp.where(strict[:, lo:hi], gram[0:CHUNK] * dec, 0.0)
        qk = gram[CHUNK:2 * CHUNK] * dec
        sol = jnp.concatenate([v * bh, k_beta * e_gc[:, lo:hi]], axis=1)
        sol = sol - _dot_hi(m, sol)
        power = m
        for _ in range(int(math.log2(CHUNK)) - 1):
            power = _dot_hi(power, power)
            sol = sol + _dot_hi(power, sol)
        u_vals = sol[:, 0:HEAD_DIM]
        w_keys = sol[:, HEAD_DIM:2 * HEAD_DIM]
        state = state_ref[h]
        v_new = u_vals - _dot_bf(w_keys, state)
        out = _dot_bf(q * e_gc[:, lo:hi], state) + _dot_bf(qk, v_new)
        k_dec_t = _dot_nt_bf(eye, k * e_rest[:, lo:hi])
        state_ref[h] = state * e_total[:, lo:hi] + _dot_bf(k_dec_t, v_new)
        out = _rms(out, nw) * _silu(z[:, lo:hi])
        outs.append(out)
    o_ref[0] = jnp.concatenate(outs, axis=1)


def _gdn(qkv, z, small, conv_w, a_log, dt_bias, norm_w):
    b, s, _ = qkv.shape
    return pl.pallas_call(
        _gdn_kernel,
        out_shape=jax.ShapeDtypeStruct((b, s, BRANCH), F32),
        grid=(b, s // CHUNK),
        in_specs=[
            pl.BlockSpec((1, CHUNK, 3 * BRANCH), lambda i, c: (i, c, 0)),
            pl.BlockSpec((1, CHUNK, BRANCH), lambda i, c: (i, c, 0)),
            pl.BlockSpec((1, CHUNK, SMALL_W), lambda i, c: (i, c, 0)),
            pl.BlockSpec((CONV_WIDTH, 3 * BRANCH), lambda i, c: (0, 0)),
            pl.BlockSpec((1, HEADS), lambda i, c: (0, 0)),
            pl.BlockSpec((1, HEADS), lambda i, c: (0, 0)),
            pl.BlockSpec((1, HEAD_DIM), lambda i, c: (0, 0)),
        ],
        out_specs=pl.BlockSpec((1, CHUNK, BRANCH), lambda i, c: (i, c, 0)),
        scratch_shapes=[
            pltpu.VMEM((CONV_HIST + CHUNK, 3 * BRANCH), F32),
            pltpu.VMEM((HEADS, HEAD_DIM, HEAD_DIM), F32),
        ],
        compiler_params=_cparams("parallel", "arbitrary"),
        name="gdn",
    )(qkv, z, small, conv_w, a_log, dt_bias, norm_w)


def _mlstm_kernel(qk_ref, v_ref, o_ref_in, sm_ref, cw_ref, ib_ref, fb_ref, nw_ref, out_ref,
                  pad_ref, c_ref, n_ref, m_ref):
    c = pl.program_id(1)

    @pl.when(c == 0)
    def _():
        c_ref[...] = jnp.zeros(c_ref.shape, F32)
        n_ref[...] = jnp.zeros(n_ref.shape, F32)
        m_ref[...] = jnp.zeros(m_ref.shape, F32)

    x = _causal_conv_silu(qk_ref, cw_ref, pad_ref, c == 0)
    sm = sm_ref[0]
    log_i = sm[:, 2 * HEADS:3 * HEADS] + ib_ref[...]
    log_f = -_softplus(-(sm[:, 3 * HEADS:4 * HEADS] + fb_ref[...]))
    li_b = _head_bcast(log_i)
    lf_b = _head_bcast(log_f)

    row = _iota((CHUNK, BRANCH), 0)
    col = _iota((CHUNK, BRANCH), 1) % HEAD_DIM
    rhs = jnp.concatenate([lf_b, jnp.where(row > col, lf_b, 0.0), jnp.where(row == col, li_b, 0.0)], axis=1)
    sums = _dot_hi(_tri_consts(), rhs)
    b_col = sums[0:CHUNK, 0:BRANCH]
    rest_col = sums[CHUNK:2 * CHUNK, 0:BRANCH]
    b_last = sums[2 * CHUNK:3 * CHUNK, 0:BRANCH]
    diff = sums[0:CHUNK, BRANCH:2 * BRANCH]
    rest_row = sums[2 * CHUNK:3 * CHUNK, BRANCH:2 * BRANCH]
    li_row = sums[2 * CHUNK:3 * CHUNK, 2 * BRANCH:3 * BRANCH]
    d_intra = jnp.where(row >= col, diff + li_row, -jnp.inf)
    a_key_col = rest_col + li_b
    a_key_row = rest_row + li_row

    eye = (_iota((CHUNK, CHUNK), 0) == _iota((CHUNK, CHUNK), 1)).astype(F32)
    v_all = v_ref[0]
    o_pre = o_ref_in[0]
    nw = nw_ref[...]
    outs = []
    for h in range(HEADS):
        lo, hi = h * HEAD_DIM, (h + 1) * HEAD_DIM
        q = x[:, lo:hi]
        k = x[:, BRANCH + lo:BRANCH + hi] * (HEAD_DIM ** -0.5)
        v = v_all[:, lo:hi]
        c_mem = c_ref[h]
        n_mem = n_ref[h]
        m_mem = m_ref[h]
        m_inter = b_col[:, lo:hi] + m_mem
        m_t = jnp.maximum(m_inter[:, 0:1], jnp.max(d_intra[:, lo:hi], axis=-1, keepdims=True))
        w_inter = jnp.exp(m_inter[:, 0:1] - m_t)
        s_qk = _dot_nt_bf(q, k) * jnp.exp(d_intra[:, lo:hi] - m_t)
        num = w_inter * _dot_bf(q, c_mem) + _dot_bf(s_qk, v)
        qn = w_inter * jnp.sum(q * n_mem, axis=-1, keepdims=True) + jnp.sum(s_qk, axis=-1, keepdims=True)
        h_tilde = num / jnp.maximum(jnp.abs(qn), jnp.exp(-m_t))
        h_norm = _rms(h_tilde, nw[:, lo:hi])
        outs.append(jax.nn.sigmoid(o_pre[:, lo:hi]) * h_norm)
        g_c = b_last[0:1, lo:hi]
        m_key = jnp.max(a_key_row[0:1, lo:hi], axis=-1, keepdims=True)
        m_new = jnp.maximum(g_c + m_mem, m_key)
        scale = jnp.exp(g_c + m_mem - m_new)
        k_w = k * jnp.exp(a_key_col[:, lo:hi] - m_new)
        k_w_t = _dot_nt_bf(eye, k_w)
        c_ref[h] = c_mem * scale[0:1, 0:1] + _dot_bf(k_w_t, v)
        n_ref[h] = n_mem * scale + jnp.sum(k_w, axis=0, keepdims=True)
        m_ref[h] = m_new
    out_ref[0] = jnp.concatenate(outs, axis=1)


def _mlstm(qk, v, o_pre, small, conv_w, i_bias, f_bias, norm_w):
    b, s, _ = qk.shape
    return pl.pallas_call(
        _mlstm_kernel,
        out_shape=jax.ShapeDtypeStruct((b, s, BRANCH), F32),
        grid=(b, s // CHUNK),
        in_specs=[
            pl.BlockSpec((1, CHUNK, 2 * BRANCH), lambda i, c: (i, c, 0)),
            pl.BlockSpec((1, CHUNK, BRANCH), lambda i, c: (i, c, 0)),
            pl.BlockSpec((1, CHUNK, BRANCH), lambda i, c: (i, c, 0)),
            pl.BlockSpec((1, CHUNK, SMALL_W), lambda i, c: (i, c, 0)),
            pl.BlockSpec((CONV_WIDTH, 2 * BRANCH), lambda i, c: (0, 0)),
            pl.BlockSpec((1, HEADS), lambda i, c: (0, 0)),
            pl.BlockSpec((1, HEADS), lambda i, c: (0, 0)),
            pl.BlockSpec((1, BRANCH), lambda i, c: (0, 0)),
        ],
        out_specs=pl.BlockSpec((1, CHUNK, BRANCH), lambda i, c: (i, c, 0)),
        scratch_shapes=[
            pltpu.VMEM((CONV_HIST + CHUNK, 2 * BRANCH), F32),
            pltpu.VMEM((HEADS, HEAD_DIM, HEAD_DIM), F32),
            pltpu.VMEM((HEADS, 1, HEAD_DIM), F32),
            pltpu.VMEM((HEADS, 1, HEAD_DIM), F32),
        ],
        compiler_params=_cparams("parallel", "arbitrary"),
        name="mlstm",
    )(qk, v, o_pre, small, conv_w, i_bias, f_bias, norm_w)


def _s5_prep_kernel(lre_ref, lim_ref, ldt_ref, btre_ref, btim_ref, cre_ref, cim_ref,
                    toep_ref, wst_ref, wout_ref, al_ref):
    lr = lre_ref[0]
    li = lim_ref[0]
    dt = jnp.exp(ldt_ref[0])
    lam = lr * dt
    theta = li * dt
    mag = jnp.exp(lam)
    a_re = mag * jnp.cos(theta)
    a_im = mag * jnp.sin(theta)
    den = lr * lr + li * li
    z_re = ((a_re - 1.0) * lr + a_im * li) / den
    z_im = (a_im * lr - (a_re - 1.0) * li) / den
    bt_re = z_re * btre_ref[0] - z_im * btim_ref[0]
    bt_im = z_re * btim_ref[0] + z_im * btre_ref[0]
    c_re = cre_ref[0]
    c_im = cim_ref[0]

    def tiled(t):
        return jnp.concatenate([t] * CHUNK, axis=0)

    step = (_iota((S5_FLAT, S5_STATE), 0) // S5_CH).astype(F32)

    def power(e):
        mg = jnp.exp(lam * e)
        return mg * jnp.cos(theta * e), mg * jnp.sin(theta * e)

    def cmul(xr, xi, yr, yi):
        return xr * yr - xi * yi, xr * yi + xi * yr

    btr, bti = tiled(bt_re), tiled(bt_im)
    ctr, cti = tiled(c_re), tiled(c_im)
    pr, pi = power(float(CHUNK - 1) - step)
    wr, wi = cmul(pr, pi, btr, bti)
    wst_ref[0] = jnp.concatenate([wr, wi], axis=1).astype(BF16)
    pr, pi = power(step + 1.0)
    wr, wi = cmul(pr, pi, ctr, cti)
    wout_ref[0] = jnp.concatenate([wr, -wi], axis=1).astype(BF16)
    pr, pi = power(step)
    wr, wi = cmul(pr, pi, ctr, cti)
    resp = _dot_nt_hi(jnp.concatenate([bt_re, bt_im], axis=1), jnp.concatenate([wr, -wi], axis=1))
    lane = _iota((S5_CH, S5_FLAT), 1)
    toep = resp
    shift = S5_CH
    while toep.shape[0] < S5_FLAT:
        rows = toep.shape[0]
        lane = _iota((rows, S5_FLAT), 1)
        shifted = jnp.where(lane >= shift, pltpu.roll(toep, shift, axis=1), 0.0)
        toep = jnp.concatenate([toep, shifted], axis=0)
        shift *= 2
    toep_ref[0] = toep.astype(BF16)
    lr_, li_ = power(jnp.full((1, S5_STATE), float(CHUNK), F32))
    al_ref[0] = jnp.concatenate(
        [jnp.concatenate([lr_, lr_], axis=1), jnp.concatenate([-li_, li_], axis=1),
         jnp.zeros((6, 2 * S5_STATE), F32)], axis=0)


def _s5_prep(lam_re, lam_im, log_dt, b_re, b_im, c_re, c_im):
    g = lam_re.shape[0]
    vec = lambda a: a.reshape(g, 1, S5_STATE)
    mat = pl.BlockSpec((1, S5_CH, S5_STATE), lambda i: (i, 0, 0))
    row = pl.BlockSpec((1, 1, S5_STATE), lambda i: (i, 0, 0))
    wide = pl.BlockSpec((1, S5_FLAT, 2 * S5_STATE), lambda i: (i, 0, 0))
    return pl.pallas_call(
        _s5_prep_kernel,
        out_shape=[
            jax.ShapeDtypeStruct((g, S5_FLAT, S5_FLAT), BF16),
            jax.ShapeDtypeStruct((g, S5_FLAT, 2 * S5_STATE), BF16),
            jax.ShapeDtypeStruct((g, S5_FLAT, 2 * S5_STATE), BF16),
            jax.ShapeDtypeStruct((g, 8, 2 * S5_STATE), F32),
        ],
        grid=(g,),
        in_specs=[row, row, row, mat, mat, mat, mat],
        out_specs=[
            pl.BlockSpec((1, S5_FLAT, S5_FLAT), lambda i: (i, 0, 0)),
            wide, wide,
            pl.BlockSpec((1, 8, 2 * S5_STATE), lambda i: (i, 0, 0)),
        ],
        compiler_params=_cparams("parallel"),
        name="s5_prep",
    )(vec(lam_re), vec(lam_im), jnp.broadcast_to(log_dt[:, None, None], (g, 1, S5_STATE)),
      jnp.swapaxes(b_re, 1, 2), jnp.swapaxes(b_im, 1, 2), c_re, c_im)


def _s5_core_kernel(x_ref, toep_ref, wst_ref, wout_ref, al_ref, y_ref, inj_ref, prev_ref, *, batch):
    x = x_ref[0].astype(BF16)
    inj_ref[...] = _dot(x, wst_ref[0])
    a_c = al_ref[0, 0:1, :]
    a_s = al_ref[0, 1:2, :]
    n_chunks = x.shape[0] // batch

    def body(c, state):
        r = pl.multiple_of(c * batch, batch)
        prev_ref[pl.ds(r, batch), :] = state
        return state * a_c + pltpu.roll(state, S5_STATE, axis=1) * a_s + inj_ref[pl.ds(r, batch), :]

    lax.fori_loop(0, n_chunks, body, jnp.zeros((batch, 2 * S5_STATE), F32))
    y_ref[0] = _dot(x, toep_ref[0]) + _dot_nt_bf(prev_ref[...], wout_ref[0])


def _s5_core(x, toep, wst, wout, al, batch):
    g, r, _ = x.shape
    wide = pl.BlockSpec((1, S5_FLAT, 2 * S5_STATE), lambda i: (i, 0, 0))
    return pl.pallas_call(
        functools.partial(_s5_core_kernel, batch=batch),
        out_shape=jax.ShapeDtypeStruct((g, r, S5_FLAT), F32),
        grid=(g,),
        in_specs=[
            pl.BlockSpec((1, r, S5_FLAT), lambda i: (i, 0, 0)),
            pl.BlockSpec((1, S5_FLAT, S5_FLAT), lambda i: (i, 0, 0)),
            wide, wide,
            pl.BlockSpec((1, 8, 2 * S5_STATE), lambda i: (i, 0, 0)),
        ],
        out_specs=pl.BlockSpec((1, r, S5_FLAT), lambda i: (i, 0, 0)),
        scratch_shapes=[pltpu.VMEM((r, 2 * S5_STATE), F32), pltpu.VMEM((r, 2 * S5_STATE), F32)],
        compiler_params=_cparams("parallel"),
        name="s5_core",
    )(x, toep, wst, wout, al)


def _s5_glu_kernel(y_ref, u_ref, d_ref, w_ref, b_ref, o_ref):
    y = y_ref[...] + d_ref[...] * u_ref[...]
    y = jax.nn.gelu(y)
    o_ref[...] = y * jax.nn.sigmoid(_dot(y.astype(BF16), w_ref[...]) + b_ref[...])


def _s5_glu(y, u, d_skip, w_glu, b_glu, tm):
    t, w = y.shape
    tok = pl.BlockSpec((tm, w), lambda i: (i, 0))
    vec = pl.BlockSpec((1, w), lambda i: (0, 0))
    return pl.pallas_call(
        _s5_glu_kernel,
        out_shape=jax.ShapeDtypeStruct((t, w), F32),
        grid=(t // tm,),
        in_specs=[tok, tok, vec, pl.BlockSpec((w, w), lambda i: (0, 0)), vec],
        out_specs=tok,
        compiler_params=_cparams("parallel"),
        name="s5_glu",
    )(y, u, d_skip, w_glu, b_glu)


def _s5(u, batch, seq, prm, tm):
    g, n = S5_GROUPS, S5_CH
    nc = seq // CHUNK
    toep, wst, wout, al = _s5_prep(prm["lam_re"], prm["lam_im"], prm["log_dt"],
                                   prm["b_re"], prm["b_im"], prm["c_re"], prm["c_im"])
    x = u.reshape(batch, nc, CHUNK, g, n).transpose(3, 1, 0, 2, 4).reshape(g, nc * batch, S5_FLAT)
    y = _s5_core(x, toep, wst, wout, al, batch)
    y = y.reshape(g, nc, batch, CHUNK, n).transpose(2, 1, 3, 0, 4).reshape(batch * seq, g * n)
    return _s5_glu(y, u, prm["d"], prm["w_glu"], prm["b_glu"], tm)


def _moba_prep_kernel(q_ref, k_ref, pos_ref, inv_ref, qo_ref, ko_ref, km_ref):
    ang = pos_ref[0] * inv_ref[...]
    cos = jnp.cos(ang)
    half = _iota(ang.shape, 1) < HEAD_DIM // 2
    sin = jnp.where(half, -jnp.sin(ang), jnp.sin(ang))

    def rope(x):
        outs = []
        for h in range(HEADS):
            xh = x[:, h * HEAD_DIM:(h + 1) * HEAD_DIM]
            swapped = jnp.concatenate([xh[:, HEAD_DIM // 2:], xh[:, :HEAD_DIM // 2]], axis=1)
            outs.append(xh * cos + swapped * sin)
        return jnp.concatenate(outs, axis=1)

    k = rope(k_ref[0])
    qo_ref[0] = (rope(q_ref[0]) * (HEAD_DIM ** -0.5)).astype(BF16)
    ko_ref[0] = k.astype(BF16)
    km_ref[0, 0] = jnp.mean(k, axis=0, keepdims=True)


def _moba_prep(qkv, pos, inv2):
    b, s, _ = qkv.shape
    nb = s // MOBA_BLOCK
    blk = lambda j: pl.BlockSpec((1, MOBA_BLOCK, BRANCH), lambda i, n, j=j: (i, n, j))
    return pl.pallas_call(
        _moba_prep_kernel,
        out_shape=[
            jax.ShapeDtypeStruct((b, s, BRANCH), BF16),
            jax.ShapeDtypeStruct((b, s, BRANCH), BF16),
            jax.ShapeDtypeStruct((b, nb, 1, BRANCH), F32),
        ],
        grid=(b, nb),
        in_specs=[blk(0), blk(1),
                  pl.BlockSpec((1, MOBA_BLOCK, 1), lambda i, n: (i, n, 0)),
                  pl.BlockSpec((1, HEAD_DIM), lambda i, n: (0, 0))],
        out_specs=[blk(0), blk(0), pl.BlockSpec((1, 1, 1, BRANCH), lambda i, n: (i, n, 0, 0))],
        compiler_params=_cparams("parallel", "parallel"),
        name="moba_prep",
    )(qkv, qkv, pos, inv2)


def _moba_attn_kernel(q_ref, k_ref, v_ref, km_ref, o_ref, *, nb):
    i = pl.program_id(1)
    blk = MOBA_BLOCK
    lane = _iota((blk, nb), 1)
    rr = _iota((blk, blk), 0)
    cc = _iota((blk, blk), 1)
    outs = []
    for h in range(HEADS):
        lo, hi = h * HEAD_DIM, (h + 1) * HEAD_DIM
        q = q_ref[0, :, lo:hi]
        gate = _dot_nt_hi(q.astype(F32), km_ref[0, :, lo:hi])
        rank = jnp.zeros((blk, nb), jnp.int32)
        for m in range(nb):
            gm = gate[:, m:m + 1]
            beats = (gm > gate) | ((gm == gate) & (m < lane))
            rank = rank + jnp.where(beats & (m < i), 1, 0)
        sel = ((lane < i) & (rank < MOBA_TOPK)).astype(F32)

        def body(n, carry):
            m_run, l_run, acc = carry
            start = pl.multiple_of(n * blk, blk)
            kb = k_ref[0, pl.ds(start, blk), lo:hi]
            vb = v_ref[0, pl.ds(start, blk), lo:hi]
            s = lax.dot_general(q, kb, NT_DIMS, preferred_element_type=F32)
            picked = jnp.max(jnp.where(lane == n, sel, 0.0), axis=-1, keepdims=True) > 0.0
            s = jnp.where(picked, s, -jnp.inf)
            m_new = jnp.maximum(m_run, jnp.max(s, axis=-1, keepdims=True))
            m_safe = jnp.where(m_new == -jnp.inf, 0.0, m_new)
            alpha = jnp.exp(m_run - m_safe)
            p = jnp.exp(s - m_safe)
            l_new = alpha * l_run + jnp.sum(p, axis=-1, keepdims=True)
            acc = alpha * acc + _dot_bf(p, vb)
            return m_new, l_new, acc

        init = (jnp.full((blk, 1), -jnp.inf, F32), jnp.zeros((blk, 1), F32), jnp.zeros((blk, HEAD_DIM), F32))
        m_run, l_run, acc = lax.fori_loop(0, i, body, init)
        start = pl.multiple_of(i * blk, blk)
        kb = k_ref[0, pl.ds(start, blk), lo:hi]
        vb = v_ref[0, pl.ds(start, blk), lo:hi]
        s = lax.dot_general(q, kb, NT_DIMS, preferred_element_type=F32)
        s = jnp.where(cc <= rr, s, -jnp.inf)
        m_new = jnp.maximum(m_run, jnp.max(s, axis=-1, keepdims=True))
        alpha = jnp.exp(m_run - m_new)
        p = jnp.exp(s - m_new)
        l_new = alpha * l_run + jnp.sum(p, axis=-1, keepdims=True)
        acc = alpha * acc + _dot_bf(p, vb)
        outs.append(acc / l_new)
    o_ref[0] = jnp.concatenate(outs, axis=1)


def _moba_attn(q, k, qkv, kmean):
    b, s, _ = q.shape
    nb = s // MOBA_BLOCK
    return pl.pallas_call(
        functools.partial(_moba_attn_kernel, nb=nb),
        out_shape=jax.ShapeDtypeStruct((b, s, BRANCH), F32),
        grid=(b, nb),
        in_specs=[
            pl.BlockSpec((1, MOBA_BLOCK, BRANCH), lambda i, n: (i, n, 0)),
            pl.BlockSpec((1, s, BRANCH), lambda i, n: (i, 0, 0)),
            pl.BlockSpec((1, s, BRANCH), lambda i, n: (i, 0, 2)),
            pl.BlockSpec((1, nb, BRANCH), lambda i, n: (i, 0, 0)),
        ],
        out_specs=pl.BlockSpec((1, MOBA_BLOCK, BRANCH), lambda i, n: (i, n, 0)),
        compiler_params=_cparams("parallel", "arbitrary"),
        name="moba_attn",
    )(q, k, qkv, kmean)


def _moba(qkv, positions):
    b, s, _ = qkv.shape
    inv = ROPE_THETA ** (-jnp.arange(0, HEAD_DIM, 2, dtype=F32) / HEAD_DIM)
    inv2 = jnp.concatenate([inv, inv])[None, :]
    pos = positions.astype(F32)[..., None]
    q, k, kmean = _moba_prep(qkv, pos, inv2)
    return _moba_attn(q, k, qkv, kmean.reshape(b, s // MOBA_BLOCK, BRANCH))


def _permute_w_in(w_in):
    cuts = [0]
    for width in (3 * BRANCH, BRANCH, HEADS, HEADS, 2 * BRANCH, BRANCH, BRANCH, HEADS, HEADS, BRANCH, 3 * BRANCH):
        cuts.append(cuts[-1] + width)
    seg = [w_in[:, cuts[n]:cuts[n + 1]] for n in range(len(cuts) - 1)]
    gdn_qkv, gdn_z, gdn_b, gdn_a, ml_qk, ml_v, ml_o, ml_i, ml_f, s5_u, moba_qkv = seg
    pad = jnp.zeros((w_in.shape[0], SMALL_W - 4 * HEADS), w_in.dtype)
    return jnp.concatenate([gdn_qkv, gdn_z, ml_qk, ml_v, ml_o, s5_u, moba_qkv, gdn_b, gdn_a, ml_i, ml_f, pad], axis=1)


def _token_tile(t):
    for tm in (512, 256, 128, 64, 32, 16, 8):
        if t % tm == 0:
            return tm
    raise ValueError(f"token count {t} is not a multiple of 8")


def _ff_tile(f):
    for parts in (2, 1):
        if f % parts == 0 and (f // parts) % 128 == 0:
            return f // parts
    raise ValueError(f"ffn width {f} is not a multiple of 128")


def kernel(x, p, positions, ffn1_norm, ffn1_w_gu, ffn1_w_down, mix_norm, w_in, gdn_conv, gdn_a_log, gdn_dt_bias, gdn_norm, mlstm_conv, mlstm_i_bias, mlstm_f_bias, mlstm_norm, s5_lambda_re, s5_lambda_im, s5_b_re, s5_b_im, s5_c_re, s5_c_im, s5_d, s5_log_dt, s5_w_glu, s5_b_glu, w_gate, w_branch, w_out, ffn2_norm, ffn2_w_gu, ffn2_w_down, ple_norm, ple_w_proj, ple_w_gate, final_norm):
    batch, seq, d = x.shape
    depth = p.shape[0]
    t = batch * seq
    tm = _token_tile(t)
    tf = _ff_tile(ffn1_w_down.shape[1])
    bf = lambda a: a.astype(BF16)
    row = lambda a: a.reshape(1, -1)
    seq3 = lambda a: a.reshape(batch, seq, a.shape[-1])

    h = x.reshape(t, d)
    for i in range(depth):
        h = _ffn(h, row(ffn1_norm[i]), bf(ffn1_w_gu[i]), bf(ffn1_w_down[i]), tm, tf)

        gdn_qkv, gdn_z, ml_qk, ml_v, ml_o, s5_u, moba_qkv, small = _inproj(
            h, row(mix_norm[i]), bf(_permute_w_in(w_in[i])), tm)
        y_gdn = _gdn(seq3(gdn_qkv), seq3(gdn_z), seq3(small), gdn_conv[i], row(gdn_a_log[i]),
                     row(gdn_dt_bias[i]), row(gdn_norm[i]))
        y_mlstm = _mlstm(seq3(ml_qk), seq3(ml_v), seq3(ml_o), seq3(small), mlstm_conv[i],
                         row(mlstm_i_bias[i]), row(mlstm_f_bias[i]), row(mlstm_norm[i]))
        s5_prm = dict(lam_re=s5_lambda_re[i], lam_im=s5_lambda_im[i], log_dt=s5_log_dt[i],
                      b_re=s5_b_re[i], b_im=s5_b_im[i], c_re=s5_c_re[i], c_im=s5_c_im[i],
                      d=row(s5_d[i]), w_glu=bf(s5_w_glu[i]), b_glu=row(s5_b_glu[i]))
        y_s5 = _s5(s5_u, batch, seq, s5_prm, tm)
        y_moba = _moba(seq3(moba_qkv), positions)
        ys = (y_gdn.reshape(t, BRANCH), y_mlstm.reshape(t, BRANCH), y_s5, y_moba.reshape(t, BRANCH))
        h = _merge(h, row(mix_norm[i]), ys, bf(w_gate[i]), bf(w_branch[i]), bf(w_out[i]), tm)

        h = _ffn(h, row(ffn2_norm[i]), bf(ffn2_w_gu[i]), bf(ffn2_w_down[i]), tm, tf)
        h = _ple(h, row(ple_norm[i]), p[i].reshape(t, -1), bf(ple_w_proj[i]), bf(ple_w_gate[i]),
                 row(final_norm), i == depth - 1, tm)
    return h.reshape(batch, seq, d)
```

```python
import functools
import math

import jax
import jax.numpy as jnp
from jax import lax
from jax.experimental import pallas as pl
from jax.experimental.pallas import tpu as pltpu

F32 = jnp.float32
BF16 = jnp.bfloat16
HIGHEST = lax.Precision.HIGHEST

NORM_EPS = 1e-6
HEADS = 4
HEAD_DIM = 64
BRANCH = HEADS * HEAD_DIM
CHUNK = 64
CONV_WIDTH = 4
CONV_HIST = 8
S5_GROUPS = 16
S5_CH = 16
S5_STATE = 64
S5_FLAT = CHUNK * S5_CH
MOBA_BLOCK = 256
MOBA_TOPK = 3
ROPE_THETA = 10000.0
SMALL_W = 128
V7X_VMEM_LIMIT = 56 * 1024 * 1024

NT_DIMS = (((1,), (1,)), ((), ()))


def _cparams(*sem):
    return pltpu.CompilerParams(dimension_semantics=sem, vmem_limit_bytes=V7X_VMEM_LIMIT)


def _rms(x, w):
    return x * lax.rsqrt(jnp.mean(x * x, axis=-1, keepdims=True) + NORM_EPS) * w


def _silu(x):
    return x * jax.nn.sigmoid(x)


def _softplus(x):
    return jnp.maximum(x, 0.0) + jnp.log1p(jnp.exp(-jnp.abs(x)))


def _dot(a, b):
    return jnp.dot(a, b, preferred_element_type=F32)


def _dot_bf(a, b):
    return jnp.dot(a.astype(BF16), b.astype(BF16), preferred_element_type=F32)


def _dot_nt_bf(a, b):
    return lax.dot_general(a.astype(BF16), b.astype(BF16), NT_DIMS, preferred_element_type=F32)


def _dot_hi(a, b):
    return jnp.dot(a, b, precision=HIGHEST, preferred_element_type=F32)


def _dot_nt_hi(a, b):
    return lax.dot_general(a, b, NT_DIMS, precision=HIGHEST, preferred_element_type=F32)


def _split_bf(a, parts):
    out = []
    for _ in range(parts):
        piece = a.astype(BF16)
        out.append(piece)
        a = a - piece.astype(F32)
    return out


def _dot_x3(a, b):
    a_hi, a_lo = _split_bf(a, 2)
    b_hi, b_lo = _split_bf(b, 2)
    return _dot(a_hi, b_hi) + _dot(a_hi, b_lo) + _dot(a_lo, b_hi)


def _dot_exact01(a01, b):
    return sum(_dot(a01, piece) for piece in _split_bf(b, 3))


def _iota(shape, dim):
    return lax.broadcasted_iota(jnp.int32, shape, dim)


def _head_bcast(cols):
    rows = cols.shape[0]
    head = _iota((rows, BRANCH), 1) // HEAD_DIM
    out = jnp.broadcast_to(cols[:, 0:1], (rows, BRANCH))
    for h in range(1, HEADS):
        out = jnp.where(head == h, jnp.broadcast_to(cols[:, h:h + 1], (rows, BRANCH)), out)
    return out


def _causal_conv_silu(x_ref, w_ref, pad_ref, first):
    rows = x_ref.shape[1]

    @pl.when(first)
    def _():
        pad_ref[0:CONV_HIST, :] = jnp.zeros((CONV_HIST, pad_ref.shape[1]), F32)

    pad_ref[CONV_HIST:CONV_HIST + rows, :] = x_ref[0]
    w = w_ref[...]
    acc = w[CONV_WIDTH - 1:CONV_WIDTH, :] * pad_ref[CONV_HIST:CONV_HIST + rows, :]
    for back in range(1, CONV_WIDTH):
        tap = CONV_WIDTH - 1 - back
        acc = acc + w[tap:tap + 1, :] * pad_ref[CONV_HIST - back:CONV_HIST - back + rows, :]
    pad_ref[0:CONV_HIST, :] = pad_ref[rows:rows + CONV_HIST, :]
    return _silu(acc)


def _tri_consts():
    r = _iota((CHUNK, CHUNK), 0)
    c = _iota((CHUNK, CHUNK), 1)
    lower = (c <= r).astype(F32)
    upper_s = (c > r).astype(F32)
    ones = jnp.ones((CHUNK, CHUNK), F32)
    return jnp.concatenate([lower, upper_s, ones], axis=0)


def _ffn_kernel(x_ref, nw_ref, wg_ref, wu_ref, wd_ref, o_ref, xn_ref):
    j = pl.program_id(1)

    @pl.when(j == 0)
    def _():
        xn_ref[...] = _rms(x_ref[...], nw_ref[...]).astype(BF16)

    xn = xn_ref[...]
    gate = _dot(xn, wg_ref[...])
    up = _dot(xn, wu_ref[...])
    part = 0.5 * _dot((_silu(gate) * up).astype(BF16), wd_ref[...])

    @pl.when(j == 0)
    def _():
        o_ref[...] = x_ref[...] + part

    @pl.when(j > 0)
    def _():
        o_ref[...] += part


def _ffn(h, norm_w, w_gu, w_down, tm, tf):
    t, d = h.shape
    f = w_down.shape[0]
    nf = f // tf
    return pl.pallas_call(
        _ffn_kernel,
        out_shape=jax.ShapeDtypeStruct((t, d), F32),
        grid=(t // tm, nf),
        in_specs=[
            pl.BlockSpec((tm, d), lambda i, j: (i, 0)),
            pl.BlockSpec((1, d), lambda i, j: (0, 0)),
            pl.BlockSpec((d, tf), lambda i, j: (0, j)),
            pl.BlockSpec((d, tf), lambda i, j, nf=nf: (0, j + nf)),
            pl.BlockSpec((tf, d), lambda i, j: (j, 0)),
        ],
        out_specs=pl.BlockSpec((tm, d), lambda i, j: (i, 0)),
        scratch_shapes=[pltpu.VMEM((tm, d), BF16)],
        compiler_params=_cparams("parallel", "arbitrary"),
        name="ffn",
    )(h, norm_w, w_gu, w_gu, w_down)


IN_SEGMENTS = (3 * BRANCH, BRANCH, 2 * BRANCH, BRANCH, BRANCH, BRANCH, 3 * BRANCH, SMALL_W)


def _inproj_kernel(x_ref, nw_ref, w_ref, *o_refs):
    xn = _rms(x_ref[...], nw_ref[...]).astype(BF16)
    start = 0
    for o_ref, width in zip(o_refs, IN_SEGMENTS):
        o_ref[...] = _dot(xn, w_ref[:, start:start + width])
        start += width


def _inproj(h, norm_w, w_perm, tm):
    t, d = h.shape
    n = w_perm.shape[1]
    return pl.pallas_call(
        _inproj_kernel,
        out_shape=[jax.ShapeDtypeStruct((t, w), F32) for w in IN_SEGMENTS],
        grid=(t // tm,),
        in_specs=[
            pl.BlockSpec((tm, d), lambda i: (i, 0)),
            pl.BlockSpec((1, d), lambda i: (0, 0)),
            pl.BlockSpec((d, n), lambda i: (0, 0)),
        ],
        out_specs=[pl.BlockSpec((tm, w), lambda i: (i, 0)) for w in IN_SEGMENTS],
        compiler_params=_cparams("parallel"),
        name="inproj",
    )(h, norm_w, w_perm)


def _merge_kernel(h_ref, nw_ref, y0_ref, y1_ref, y2_ref, y3_ref, wg_ref, wb_ref, wo_ref, o_ref):
    h = h_ref[...]
    u = _rms(h, nw_ref[...]).astype(BF16)
    merged = None
    for b, y_ref in enumerate((y0_ref, y1_ref, y2_ref, y3_ref)):
        term = jax.nn.sigmoid(_dot(u, wg_ref[b])) * _dot(y_ref[...].astype(BF16), wb_ref[b])
        merged = term if merged is None else merged + term
    o_ref[...] = h + _dot(merged.astype(BF16), wo_ref[...])


def _merge(h, norm_w, ys, w_gate, w_branch, w_out, tm):
    t, d = h.shape
    nb, bw, _ = w_branch.shape
    return pl.pallas_call(
        _merge_kernel,
        out_shape=jax.ShapeDtypeStruct((t, d), F32),
        grid=(t // tm,),
        in_specs=[
            pl.BlockSpec((tm, d), lambda i: (i, 0)),
            pl.BlockSpec((1, d), lambda i: (0, 0)),
        ] + [pl.BlockSpec((tm, bw), lambda i: (i, 0)) for _ in range(nb)] + [
            pl.BlockSpec((nb, d, d), lambda i: (0, 0, 0)),
            pl.BlockSpec((nb, bw, d), lambda i: (0, 0, 0)),
            pl.BlockSpec((d, d), lambda i: (0, 0)),
        ],
        out_specs=pl.BlockSpec((tm, d), lambda i: (i, 0)),
        compiler_params=_cparams("parallel"),
        name="merge",
    )(h, norm_w, *ys, w_gate, w_branch, w_out)


def _ple_kernel(h_ref, nw_ref, p_ref, wp_ref, wg_ref, fw_ref, o_ref, *, final):
    h = h_ref[...]
    gate = jax.nn.sigmoid(_dot(_rms(h, nw_ref[...]).astype(BF16), wg_ref[...]))
    out = h + _dot(p_ref[...].astype(BF16), wp_ref[...]) * gate
    if final:
        out = _rms(out, fw_ref[...])
    o_ref[...] = out


def _ple(h, norm_w, p, w_proj, w_gate, final_w, final, tm):
    t, d = h.shape
    pd = p.shape[1]
    return pl.pallas_call(
        functools.partial(_ple_kernel, final=final),
        out_shape=jax.ShapeDtypeStruct((t, d), F32),
        grid=(t // tm,),
        in_specs=[
            pl.BlockSpec((tm, d), lambda i: (i, 0)),
            pl.BlockSpec((1, d), lambda i: (0, 0)),
            pl.BlockSpec((tm, pd), lambda i: (i, 0)),
            pl.BlockSpec((pd, d), lambda i: (0, 0)),
            pl.BlockSpec((d, d), lambda i: (0, 0)),
            pl.BlockSpec((1, d), lambda i: (0, 0)),
        ],
        out_specs=pl.BlockSpec((tm, d), lambda i: (i, 0)),
        compiler_params=_cparams("parallel"),
        name="ple",
    )(h, norm_w, p, w_proj, w_gate, final_w)


def _gdn_kernel(qkv_ref, z_ref, sm_ref, cw_ref, alog_ref, dtb_ref, nw_ref, o_ref, pad_ref, state_ref, *, cps):
    step = pl.program_id(1)

    @pl.when(step == 0)
    def _():
        state_ref[...] = jnp.zeros(state_ref.shape, F32)

    x = _causal_conv_silu(qkv_ref, cw_ref, pad_ref, step == 0)
    sm = sm_ref[0]
    beta_b = _head_bcast(jax.nn.sigmoid(sm[:, 0:HEADS]))
    g_b = _head_bcast(-jnp.exp(alog_ref[...]) * _softplus(sm[:, HEADS:2 * HEADS] + dtb_ref[...]))

    row = _iota((CHUNK, BRANCH), 0)
    col = _iota((CHUNK, BRANCH), 1) % HEAD_DIM
    strict = row > col
    causal = row >= col
    tri = _tri_consts().astype(BF16)
    eye = (_iota((CHUNK, CHUNK), 0) == _iota((CHUNK, CHUNK), 1)).astype(BF16)
    z = z_ref[0]
    nw = nw_ref[...]

    def head(t, h, part=0):
        return t[:, part * BRANCH + h * HEAD_DIM:part * BRANCH + (h + 1) * HEAD_DIM]

    sums = []
    for ci in range(cps):
        gb = g_b[ci * CHUNK:(ci + 1) * CHUNK]
        sums.append(_dot_exact01(tri, jnp.concatenate([gb, jnp.where(strict, gb, 0.0)], axis=1)))
    probs = [(ci, h) for ci in range(cps) for h in range(HEADS)]
    loc = {}
    for ci, h in probs:
        xc = x[ci * CHUNK:(ci + 1) * CHUNK]
        s = sums[ci]
        q, k, v = head(xc, h, 0), head(xc, h, 1), head(xc, h, 2)
        q = q * lax.rsqrt(jnp.sum(q * q, axis=-1, keepdims=True) + NORM_EPS) * (HEAD_DIM ** -0.5)
        k = k * lax.rsqrt(jnp.sum(k * k, axis=-1, keepdims=True) + NORM_EPS)
        bh = head(beta_b[ci * CHUNK:(ci + 1) * CHUNK], h)
        e_gc = jnp.exp(head(s[0:CHUNK], h))
        e_rest = jnp.exp(head(s[CHUNK:2 * CHUNK], h))
        e_total = jnp.exp(head(s[2 * CHUNK:3 * CHUNK], h))
        decay = jnp.where(head(causal, h), jnp.exp(head(s[0:CHUNK], h, 1)), 0.0)
        k_beta = k * bh
        loc[ci, h] = dict(q=q, k=k, k_beta=k_beta, decay=decay, e_total=e_total, q_dec=q * e_gc,
                          k_dec=k * e_rest, sol=jnp.concatenate([v * bh, k_beta * e_gc], axis=1))
    gram = {p: _dot_nt_bf(jnp.concatenate([loc[p]["k_beta"], loc[p]["q"]], axis=0), loc[p]["k"]) for p in probs}
    k_dec_t = {p: lax.dot_general(eye, loc[p]["k_dec"].astype(BF16), NT_DIMS, preferred_element_type=F32)
               for p in probs}
    power, sol, qk = {}, {}, {}
    for p in probs:
        power[p] = jnp.where(head(strict, p[1]), gram[p][0:CHUNK] * loc[p]["decay"], 0.0)
        qk[p] = gram[p][CHUNK:2 * CHUNK] * loc[p]["decay"]
        sol[p] = loc[p]["sol"]
    n_factors = int(math.log2(CHUNK))
    for j in range(n_factors):
        last = j == n_factors - 1
        prods = {p: _dot_x3(power[p], sol[p] if last else jnp.concatenate([sol[p], power[p]], axis=1))
                 for p in probs}
        for p in probs:
            term = prods[p][:, 0:2 * HEAD_DIM]
            sol[p] = sol[p] - term if j == 0 else sol[p] + term
            if not last:
                power[p] = prods[p][:, 2 * HEAD_DIM:3 * HEAD_DIM]
    state = [state_ref[h] for h in range(HEADS)]
    outs = []
    for ci in range(cps):
        from_state = [_dot_bf(jnp.concatenate([sol[ci, h][:, HEAD_DIM:2 * HEAD_DIM], loc[ci, h]["q_dec"]], axis=0),
                              state[h]) for h in range(HEADS)]
        v_new = [sol[ci, h][:, 0:HEAD_DIM] - from_state[h][0:CHUNK] for h in range(HEADS)]
        from_v = [_dot_bf(jnp.concatenate([qk[ci, h], k_dec_t[ci, h]], axis=0), v_new[h]) for h in range(HEADS)]
        heads_out = []
        for h in range(HEADS):
            out = from_state[h][CHUNK:2 * CHUNK] + from_v[h][0:CHUNK]
            state[h] = state[h] * loc[ci, h]["e_total"] + from_v[h][CHUNK:2 * CHUNK]
            heads_out.append(_rms(out, nw) * _silu(head(z[ci * CHUNK:(ci + 1) * CHUNK], h)))
        outs.append(jnp.concatenate(heads_out, axis=1))
    for h in range(HEADS):
        state_ref[h] = state[h]
    o_ref[0] = jnp.concatenate(outs, axis=0)


def _gdn(qkv, z, small, conv_w, a_log, dt_bias, norm_w, cps):
    b, s, _ = qkv.shape
    rows = cps * CHUNK
    return pl.pallas_call(
        functools.partial(_gdn_kernel, cps=cps),
        out_shape=jax.ShapeDtypeStruct((b, s, BRANCH), F32),
        grid=(b, s // rows),
        in_specs=[
            pl.BlockSpec((1, rows, 3 * BRANCH), lambda i, c: (i, c, 0)),
            pl.BlockSpec((1, rows, BRANCH), lambda i, c: (i, c, 0)),
            pl.BlockSpec((1, rows, SMALL_W), lambda i, c: (i, c, 0)),
            pl.BlockSpec((CONV_WIDTH, 3 * BRANCH), lambda i, c: (0, 0)),
            pl.BlockSpec((1, HEADS), lambda i, c: (0, 0)),
            pl.BlockSpec((1, HEADS), lambda i, c: (0, 0)),
            pl.BlockSpec((1, HEAD_DIM), lambda i, c: (0, 0)),
        ],
        out_specs=pl.BlockSpec((1, rows, BRANCH), lambda i, c: (i, c, 0)),
        scratch_shapes=[
            pltpu.VMEM((CONV_HIST + rows, 3 * BRANCH), F32),
            pltpu.VMEM((HEADS, HEAD_DIM, HEAD_DIM), F32),
        ],
        compiler_params=_cparams("parallel", "arbitrary"),
        name="gdn",
    )(qkv, z, small, conv_w, a_log, dt_bias, norm_w)


def _mlstm_kernel(qk_ref, v_ref, o_ref_in, sm_ref, cw_ref, ib_ref, fb_ref, nw_ref, out_ref,
                  pad_ref, c_ref, n_ref, m_ref, *, cps):
    step = pl.program_id(1)

    @pl.when(step == 0)
    def _():
        c_ref[...] = jnp.zeros(c_ref.shape, F32)
        n_ref[...] = jnp.zeros(n_ref.shape, F32)
        m_ref[...] = jnp.zeros(m_ref.shape, F32)

    x = _causal_conv_silu(qk_ref, cw_ref, pad_ref, step == 0)
    sm = sm_ref[0]
    li_b = _head_bcast(sm[:, 2 * HEADS:3 * HEADS] + ib_ref[...])
    lf_b = _head_bcast(-_softplus(-(sm[:, 3 * HEADS:4 * HEADS] + fb_ref[...])))

    row = _iota((CHUNK, BRANCH), 0)
    col = _iota((CHUNK, BRANCH), 1) % HEAD_DIM
    tri = _tri_consts().astype(BF16)
    eye = (_iota((CHUNK, CHUNK), 0) == _iota((CHUNK, CHUNK), 1)).astype(BF16)
    v_all = v_ref[0]
    o_pre = o_ref_in[0]
    nw = nw_ref[...]

    def head(t, h):
        return t[:, h * HEAD_DIM:(h + 1) * HEAD_DIM]

    probs = [(ci, h) for ci in range(cps) for h in range(HEADS)]
    gates = []
    for ci in range(cps):
        lf = lf_b[ci * CHUNK:(ci + 1) * CHUNK]
        li = li_b[ci * CHUNK:(ci + 1) * CHUNK]
        rhs = jnp.concatenate([lf, jnp.where(row > col, lf, 0.0), jnp.where(row == col, li, 0.0)], axis=1)
        sums = _dot_exact01(tri, rhs)
        li_row = sums[2 * CHUNK:3 * CHUNK, 2 * BRANCH:3 * BRANCH]
        gates.append(dict(
            b_col=sums[0:CHUNK, 0:BRANCH],
            b_last=sums[2 * CHUNK:3 * CHUNK, 0:BRANCH],
            d_intra=jnp.where(row >= col, sums[0:CHUNK, BRANCH:2 * BRANCH] + li_row, -jnp.inf),
            a_key_col=sums[CHUNK:2 * CHUNK, 0:BRANCH] + li,
            a_key_row=sums[2 * CHUNK:3 * CHUNK, BRANCH:2 * BRANCH] + li_row))
    m_mem = [m_ref[h] for h in range(HEADS)]
    m_prev, scale, m_next = {}, {}, {}
    for ci, h in probs:
        g_c = head(gates[ci]["b_last"], h)[0:1]
        m_key = jnp.max(head(gates[ci]["a_key_row"], h)[0:1], axis=-1, keepdims=True)
        m_prev[ci, h] = m_mem[h]
        m_next[ci, h] = jnp.maximum(g_c + m_mem[h], m_key)
        scale[ci, h] = jnp.exp(g_c + m_mem[h] - m_next[ci, h])
        m_mem[h] = m_next[ci, h]
    loc = {}
    for ci, h in probs:
        xc = x[ci * CHUNK:(ci + 1) * CHUNK]
        q = head(xc, h)
        k = head(xc[:, BRANCH:], h) * (HEAD_DIM ** -0.5)
        m_inter = head(gates[ci]["b_col"], h)[:, 0:1] + m_prev[ci, h][:, 0:1]
        d_intra = head(gates[ci]["d_intra"], h)
        m_t = jnp.maximum(m_inter, jnp.max(d_intra, axis=-1, keepdims=True))
        loc[ci, h] = dict(q=q, k=k, v=head(v_all[ci * CHUNK:(ci + 1) * CHUNK], h), m_t=m_t,
                          w_inter=jnp.exp(m_inter - m_t), e_intra=jnp.exp(d_intra - m_t),
                          k_w=k * jnp.exp(head(gates[ci]["a_key_col"], h) - m_next[ci, h]))
    s_qk = {p: _dot_nt_bf(loc[p]["q"], loc[p]["k"]) * loc[p]["e_intra"] for p in probs}
    k_w_t = {p: lax.dot_general(eye, loc[p]["k_w"].astype(BF16), NT_DIMS, preferred_element_type=F32)
             for p in probs}
    intra = {p: _dot_bf(s_qk[p], loc[p]["v"]) for p in probs}
    inject = {p: _dot_bf(k_w_t[p], loc[p]["v"]) for p in probs}
    c_mem = [c_ref[h] for h in range(HEADS)]
    n_mem = [n_ref[h] for h in range(HEADS)]
    c_prev, n_prev = {}, {}
    for ci, h in probs:
        c_prev[ci, h], n_prev[ci, h] = c_mem[h], n_mem[h]
        c_mem[h] = c_mem[h] * scale[ci, h][0:1, 0:1] + inject[ci, h]
        n_mem[h] = n_mem[h] * scale[ci, h] + jnp.sum(loc[ci, h]["k_w"], axis=0, keepdims=True)
    inter = {p: _dot_bf(loc[p]["q"], c_prev[p]) for p in probs}
    outs = []
    for ci in range(cps):
        heads_out = []
        for h in range(HEADS):
            t = loc[ci, h]
            num = t["w_inter"] * inter[ci, h] + intra[ci, h]
            qn = (t["w_inter"] * jnp.sum(t["q"] * n_prev[ci, h], axis=-1, keepdims=True)
                  + jnp.sum(s_qk[ci, h], axis=-1, keepdims=True))
            h_tilde = num / jnp.maximum(jnp.abs(qn), jnp.exp(-t["m_t"]))
            gate = jax.nn.sigmoid(head(o_pre[ci * CHUNK:(ci + 1) * CHUNK], h))
            heads_out.append(gate * _rms(h_tilde, head(nw, h)))
        outs.append(jnp.concatenate(heads_out, axis=1))
    for h in range(HEADS):
        c_ref[h] = c_mem[h]
        n_ref[h] = n_mem[h]
        m_ref[h] = m_mem[h]
    out_ref[0] = jnp.concatenate(outs, axis=0)


def _mlstm(qk, v, o_pre, small, conv_w, i_bias, f_bias, norm_w, cps):
    b, s, _ = qk.shape
    rows = cps * CHUNK
    return pl.pallas_call(
        functools.partial(_mlstm_kernel, cps=cps),
        out_shape=jax.ShapeDtypeStruct((b, s, BRANCH), F32),
        grid=(b, s // rows),
        in_specs=[
            pl.BlockSpec((1, rows, 2 * BRANCH), lambda i, c: (i, c, 0)),
            pl.BlockSpec((1, rows, BRANCH), lambda i, c: (i, c, 0)),
            pl.BlockSpec((1, rows, BRANCH), lambda i, c: (i, c, 0)),
            pl.BlockSpec((1, rows, SMALL_W), lambda i, c: (i, c, 0)),
            pl.BlockSpec((CONV_WIDTH, 2 * BRANCH), lambda i, c: (0, 0)),
            pl.BlockSpec((1, HEADS), lambda i, c: (0, 0)),
            pl.BlockSpec((1, HEADS), lambda i, c: (0, 0)),
            pl.BlockSpec((1, BRANCH), lambda i, c: (0, 0)),
        ],
        out_specs=pl.BlockSpec((1, rows, BRANCH), lambda i, c: (i, c, 0)),
        scratch_shapes=[
            pltpu.VMEM((CONV_HIST + rows, 2 * BRANCH), F32),
            pltpu.VMEM((HEADS, HEAD_DIM, HEAD_DIM), F32),
            pltpu.VMEM((HEADS, 1, HEAD_DIM), F32),
            pltpu.VMEM((HEADS, 1, HEAD_DIM), F32),
        ],
        compiler_params=_cparams("parallel", "arbitrary"),
        name="mlstm",
    )(qk, v, o_pre, small, conv_w, i_bias, f_bias, norm_w)


def _s5_prep_kernel(lre_ref, lim_ref, ldt_ref, btre_ref, btim_ref, cre_ref, cim_ref,
                    toep_ref, wst_ref, wout_ref, al_ref):
    lr = lre_ref[0]
    li = lim_ref[0]
    dt = jnp.exp(ldt_ref[0])
    lam = lr * dt
    theta = li * dt
    mag = jnp.exp(lam)
    a_re = mag * jnp.cos(theta)
    a_im = mag * jnp.sin(theta)
    den = lr * lr + li * li
    z_re = ((a_re - 1.0) * lr + a_im * li) / den
    z_im = (a_im * lr - (a_re - 1.0) * li) / den
    bt_re = z_re * btre_ref[0] - z_im * btim_ref[0]
    bt_im = z_re * btim_ref[0] + z_im * btre_ref[0]
    c_re = cre_ref[0]
    c_im = cim_ref[0]

    def tiled(t):
        return jnp.concatenate([t] * CHUNK, axis=0)

    step = (_iota((S5_FLAT, S5_STATE), 0) // S5_CH).astype(F32)

    def power(e):
        mg = jnp.exp(lam * e)
        return mg * jnp.cos(theta * e), mg * jnp.sin(theta * e)

    def cmul(xr, xi, yr, yi):
        return xr * yr - xi * yi, xr * yi + xi * yr

    btr, bti = tiled(bt_re), tiled(bt_im)
    ctr, cti = tiled(c_re), tiled(c_im)
    pr, pi = power(float(CHUNK - 1) - step)
    wr, wi = cmul(pr, pi, btr, bti)
    wst_ref[0] = jnp.concatenate([wr, wi], axis=1).astype(BF16)
    pr, pi = power(step + 1.0)
    wr, wi = cmul(pr, pi, ctr, cti)
    wout_ref[0] = jnp.concatenate([wr, -wi], axis=1).astype(BF16)
    pr, pi = power(step)
    wr, wi = cmul(pr, pi, ctr, cti)
    resp = _dot_nt_hi(jnp.concatenate([bt_re, bt_im], axis=1), jnp.concatenate([wr, -wi], axis=1))
    lane = _iota((S5_CH, S5_FLAT), 1)
    toep = resp
    shift = S5_CH
    while toep.shape[0] < S5_FLAT:
        rows = toep.shape[0]
        lane = _iota((rows, S5_FLAT), 1)
        shifted = jnp.where(lane >= shift, pltpu.roll(toep, shift, axis=1), 0.0)
        toep = jnp.concatenate([toep, shifted], axis=0)
        shift *= 2
    toep_ref[0] = toep.astype(BF16)
    lr_, li_ = power(jnp.full((1, S5_STATE), float(CHUNK), F32))
    al_ref[0] = jnp.concatenate(
        [jnp.concatenate([lr_, lr_], axis=1), jnp.concatenate([-li_, li_], axis=1),
         jnp.zeros((6, 2 * S5_STATE), F32)], axis=0)


def _s5_prep(lam_re, lam_im, log_dt, b_re, b_im, c_re, c_im):
    g = lam_re.shape[0]
    vec = lambda a: a.reshape(g, 1, S5_STATE)
    mat = pl.BlockSpec((1, S5_CH, S5_STATE), lambda i: (i, 0, 0))
    row = pl.BlockSpec((1, 1, S5_STATE), lambda i: (i, 0, 0))
    wide = pl.BlockSpec((1, S5_FLAT, 2 * S5_STATE), lambda i: (i, 0, 0))
    return pl.pallas_call(
        _s5_prep_kernel,
        out_shape=[
            jax.ShapeDtypeStruct((g, S5_FLAT, S5_FLAT), BF16),
            jax.ShapeDtypeStruct((g, S5_FLAT, 2 * S5_STATE), BF16),
            jax.ShapeDtypeStruct((g, S5_FLAT, 2 * S5_STATE), BF16),
            jax.ShapeDtypeStruct((g, 8, 2 * S5_STATE), F32),
        ],
        grid=(g,),
        in_specs=[row, row, row, mat, mat, mat, mat],
        out_specs=[
            pl.BlockSpec((1, S5_FLAT, S5_FLAT), lambda i: (i, 0, 0)),
            wide, wide,
            pl.BlockSpec((1, 8, 2 * S5_STATE), lambda i: (i, 0, 0)),
        ],
        compiler_params=_cparams("parallel"),
        name="s5_prep",
    )(vec(lam_re), vec(lam_im), jnp.broadcast_to(log_dt[:, None, None], (g, 1, S5_STATE)),
      jnp.swapaxes(b_re, 1, 2), jnp.swapaxes(b_im, 1, 2), c_re, c_im)


def _s5_core_kernel(x_ref, toep_ref, wst_ref, wout_ref, al_ref, y_ref, inj_ref, prev_ref, *, batch):
    x = x_ref[0].astype(BF16)
    inj_ref[...] = _dot(x, wst_ref[0])
    a_c = al_ref[0, 0:1, :]
    a_s = al_ref[0, 1:2, :]
    n_chunks = x.shape[0] // batch

    def body(c, state):
        r = pl.multiple_of(c * batch, batch)
        prev_ref[pl.ds(r, batch), :] = state
        return state * a_c + pltpu.roll(state, S5_STATE, axis=1) * a_s + inj_ref[pl.ds(r, batch), :]

    lax.fori_loop(0, n_chunks, body, jnp.zeros((batch, 2 * S5_STATE), F32))
    y_ref[0] = _dot(x, toep_ref[0]) + _dot_nt_bf(prev_ref[...], wout_ref[0])


def _s5_core(x, toep, wst, wout, al, batch):
    g, r, _ = x.shape
    wide = pl.BlockSpec((1, S5_FLAT, 2 * S5_STATE), lambda i: (i, 0, 0))
    return pl.pallas_call(
        functools.partial(_s5_core_kernel, batch=batch),
        out_shape=jax.ShapeDtypeStruct((g, r, S5_FLAT), F32),
        grid=(g,),
        in_specs=[
            pl.BlockSpec((1, r, S5_FLAT), lambda i: (i, 0, 0)),
            pl.BlockSpec((1, S5_FLAT, S5_FLAT), lambda i: (i, 0, 0)),
            wide, wide,
            pl.BlockSpec((1, 8, 2 * S5_STATE), lambda i: (i, 0, 0)),
        ],
        out_specs=pl.BlockSpec((1, r, S5_FLAT), lambda i: (i, 0, 0)),
        scratch_shapes=[pltpu.VMEM((r, 2 * S5_STATE), F32), pltpu.VMEM((r, 2 * S5_STATE), F32)],
        compiler_params=_cparams("parallel"),
        name="s5_core",
    )(x, toep, wst, wout, al)


def _s5_glu_kernel(y_ref, u_ref, d_ref, w_ref, b_ref, o_ref):
    y = y_ref[...] + d_ref[...] * u_ref[...]
    y = jax.nn.gelu(y)
    o_ref[...] = y * jax.nn.sigmoid(_dot(y.astype(BF16), w_ref[...]) + b_ref[...])


def _s5_glu(y, u, d_skip, w_glu, b_glu, tm):
    t, w = y.shape
    tok = pl.BlockSpec((tm, w), lambda i: (i, 0))
    vec = pl.BlockSpec((1, w), lambda i: (0, 0))
    return pl.pallas_call(
        _s5_glu_kernel,
        out_shape=jax.ShapeDtypeStruct((t, w), F32),
        grid=(t // tm,),
        in_specs=[tok, tok, vec, pl.BlockSpec((w, w), lambda i: (0, 0)), vec],
        out_specs=tok,
        compiler_params=_cparams("parallel"),
        name="s5_glu",
    )(y, u, d_skip, w_glu, b_glu)


def _s5(u, batch, seq, prm, tm):
    g, n = S5_GROUPS, S5_CH
    nc = seq // CHUNK
    toep, wst, wout, al = _s5_prep(prm["lam_re"], prm["lam_im"], prm["log_dt"],
                                   prm["b_re"], prm["b_im"], prm["c_re"], prm["c_im"])
    x = u.reshape(batch, nc, CHUNK, g, n).transpose(3, 1, 0, 2, 4).reshape(g, nc * batch, S5_FLAT)
    y = _s5_core(x, toep, wst, wout, al, batch)
    y = y.reshape(g, nc, batch, CHUNK, n).transpose(2, 1, 3, 0, 4).reshape(batch * seq, g * n)
    return _s5_glu(y, u, prm["d"], prm["w_glu"], prm["b_glu"], tm)


def _moba_prep_kernel(q_ref, k_ref, v_ref, pos_ref, inv_ref, qo_ref, ko_ref, vt_ref, sel_ref, km_ref):
    n = pl.program_id(1)
    nbp = km_ref.shape[1]

    @pl.when(n == 0)
    def _():
        km_ref[...] = jnp.zeros(km_ref.shape, F32)

    ang = pos_ref[0] * inv_ref[...]
    cos = jnp.cos(ang)
    half = _iota(ang.shape, 1) < HEAD_DIM // 2
    sin = jnp.where(half, -jnp.sin(ang), jnp.sin(ang))

    def rope(x, h):
        xh = x[:, h * HEAD_DIM:(h + 1) * HEAD_DIM]
        swapped = jnp.concatenate([xh[:, HEAD_DIM // 2:], xh[:, :HEAD_DIM // 2]], axis=1)
        return xh * cos + swapped * sin

    q_all = q_ref[0]
    k_all = k_ref[0]
    blk_row = _iota((nbp, MOBA_BLOCK), 0)
    for h in range(HEADS):
        qh = rope(q_all, h)
        kh = rope(k_all, h)
        gate = _dot_nt_hi(km_ref[h], qh)
        rank = jnp.zeros((nbp, MOBA_BLOCK), jnp.int32)
        for m in range(nbp):
            gm = gate[m:m + 1, :]
            beats = (gm > gate) | ((gm == gate) & (m < blk_row))
            rank = rank + jnp.where(beats & (m < n), 1, 0)
        sel_ref[0, 0, h] = ((blk_row < n) & (rank < MOBA_TOPK)).astype(F32)
        qo_ref[0, 0, h] = (qh * (HEAD_DIM ** -0.5)).astype(BF16)
        ko_ref[0, 0, h] = kh.astype(BF16)
        km_ref[h, pl.ds(n, 1), :] = jnp.mean(kh, axis=0, keepdims=True)
    vt_ref[0, 0] = v_ref[0].T.astype(BF16)


def _moba_prep(qkv, pos, inv2):
    b, s, _ = qkv.shape
    nb = s // MOBA_BLOCK
    nbp = -(-nb // 8) * 8
    blk = lambda j: pl.BlockSpec((1, MOBA_BLOCK, BRANCH), lambda i, n, j=j: (i, n, j))
    heads = pl.BlockSpec((1, 1, HEADS, MOBA_BLOCK, HEAD_DIM), lambda i, n: (i, n, 0, 0, 0))
    return pl.pallas_call(
        _moba_prep_kernel,
        out_shape=[
            jax.ShapeDtypeStruct((b, nb, HEADS, MOBA_BLOCK, HEAD_DIM), BF16),
            jax.ShapeDtypeStruct((b, nb, HEADS, MOBA_BLOCK, HEAD_DIM), BF16),
            jax.ShapeDtypeStruct((b, nb, BRANCH, MOBA_BLOCK), BF16),
            jax.ShapeDtypeStruct((b, nb, HEADS, nbp, MOBA_BLOCK), F32),
        ],
        grid=(b, nb),
        in_specs=[blk(0), blk(1), blk(2),
                  pl.BlockSpec((1, MOBA_BLOCK, 1), lambda i, n: (i, n, 0)),
                  pl.BlockSpec((1, HEAD_DIM), lambda i, n: (0, 0))],
        out_specs=[heads, heads,
                   pl.BlockSpec((1, 1, BRANCH, MOBA_BLOCK), lambda i, n: (i, n, 0, 0)),
                   pl.BlockSpec((1, 1, HEADS, nbp, MOBA_BLOCK), lambda i, n: (i, n, 0, 0, 0))],
        scratch_shapes=[pltpu.VMEM((HEADS, nbp, HEAD_DIM), F32)],
        compiler_params=_cparams("parallel", "arbitrary"),
        name="moba_prep",
    )(qkv, qkv, qkv, pos, inv2)


def _moba_attn_kernel(q_ref, k_ref, vt_ref, sel_ref, o_ref):
    i = pl.program_id(1)
    blk = MOBA_BLOCK
    nbp = sel_ref.shape[3]
    key_i = _iota((blk, blk), 0)
    qry_i = _iota((blk, blk), 1)
    sel_row = _iota((nbp, blk), 0)

    def update(n, carries, masks):
        scores = [lax.dot_general(k_ref[0, n, h], q_ref[0, 0, h], NT_DIMS, preferred_element_type=F32)
                  for h in range(HEADS)]
        stats = []
        for h in range(HEADS):
            m_run, l_run, _ = carries[h]
            s = jnp.where(masks[h], scores[h], -jnp.inf)
            m_new = jnp.maximum(m_run, jnp.max(s, axis=0, keepdims=True))
            m_safe = jnp.where(m_new == -jnp.inf, 0.0, m_new)
            alpha = jnp.exp(m_run - m_safe)
            p = jnp.exp(s - m_safe)
            stats.append((m_new, alpha * l_run + jnp.sum(p, axis=0, keepdims=True), alpha, p.astype(BF16)))
        out = []
        for h in range(HEADS):
            m_new, l_new, alpha, p = stats[h]
            vt = vt_ref[0, n, h * HEAD_DIM:(h + 1) * HEAD_DIM, :]
            out.append((m_new, l_new, alpha * carries[h][2] + _dot(vt, p)))
        return tuple(out)

    def body(n, carries):
        masks = [jnp.max(jnp.where(sel_row == n, sel_ref[0, 0, h], 0.0), axis=0, keepdims=True) > 0.0
                 for h in range(HEADS)]
        return update(n, carries, masks)

    init = (jnp.full((1, blk), -jnp.inf, F32), jnp.zeros((1, blk), F32), jnp.zeros((HEAD_DIM, blk), F32))
    carries = lax.fori_loop(0, i, body, (init,) * HEADS)
    carries = update(i, carries, [key_i <= qry_i] * HEADS)
    o_ref[0] = jnp.concatenate([acc / l_run for _, l_run, acc in carries], axis=0).T


def _moba_attn(q, k, vt, sel):
    b, nb = q.shape[:2]
    nbp = sel.shape[3]
    return pl.pallas_call(
        _moba_attn_kernel,
        out_shape=jax.ShapeDtypeStruct((b, nb * MOBA_BLOCK, BRANCH), F32),
        grid=(b, nb),
        in_specs=[
            pl.BlockSpec((1, 1, HEADS, MOBA_BLOCK, HEAD_DIM), lambda i, n: (i, n, 0, 0, 0)),
            pl.BlockSpec((1, nb, HEADS, MOBA_BLOCK, HEAD_DIM), lambda i, n: (i, 0, 0, 0, 0)),
            pl.BlockSpec((1, nb, BRANCH, MOBA_BLOCK), lambda i, n: (i, 0, 0, 0)),
            pl.BlockSpec((1, 1, HEADS, nbp, MOBA_BLOCK), lambda i, n: (i, n, 0, 0, 0)),
        ],
        out_specs=pl.BlockSpec((1, MOBA_BLOCK, BRANCH), lambda i, n: (i, n, 0)),
        compiler_params=_cparams("parallel", "arbitrary"),
        name="moba_attn",
    )(q, k, vt, sel)


def _moba(qkv, positions):
    inv = ROPE_THETA ** (-jnp.arange(0, HEAD_DIM, 2, dtype=F32) / HEAD_DIM)
    inv2 = jnp.concatenate([inv, inv])[None, :]
    pos = positions.astype(F32)[..., None]
    return _moba_attn(*_moba_prep(qkv, pos, inv2))


def _permute_w_in(w_in):
    cuts = [0]
    for width in (3 * BRANCH, BRANCH, HEADS, HEADS, 2 * BRANCH, BRANCH, BRANCH, HEADS, HEADS, BRANCH, 3 * BRANCH):
        cuts.append(cuts[-1] + width)
    seg = [w_in[:, cuts[n]:cuts[n + 1]] for n in range(len(cuts) - 1)]
    gdn_qkv, gdn_z, gdn_b, gdn_a, ml_qk, ml_v, ml_o, ml_i, ml_f, s5_u, moba_qkv = seg
    pad = jnp.zeros((w_in.shape[0], SMALL_W - 4 * HEADS), w_in.dtype)
    return jnp.concatenate([gdn_qkv, gdn_z, ml_qk, ml_v, ml_o, s5_u, moba_qkv, gdn_b, gdn_a, ml_i, ml_f, pad], axis=1)


def _token_tile(t):
    for tm in (512, 256, 128, 64, 32, 16, 8):
        if t % tm == 0:
            return tm
    raise ValueError(f"token count {t} is not a multiple of 8")


def _chunks_per_step(seq):
    n_chunks = seq // CHUNK
    for cps in (4, 2, 1):
        if n_chunks % cps == 0:
            return cps


def _ff_tile(f):
    for parts in (2, 1):
        if f % parts == 0 and (f // parts) % 128 == 0:
            return f // parts
    raise ValueError(f"ffn width {f} is not a multiple of 128")


def kernel(x, p, positions, ffn1_norm, ffn1_w_gu, ffn1_w_down, mix_norm, w_in, gdn_conv, gdn_a_log, gdn_dt_bias, gdn_norm, mlstm_conv, mlstm_i_bias, mlstm_f_bias, mlstm_norm, s5_lambda_re, s5_lambda_im, s5_b_re, s5_b_im, s5_c_re, s5_c_im, s5_d, s5_log_dt, s5_w_glu, s5_b_glu, w_gate, w_branch, w_out, ffn2_norm, ffn2_w_gu, ffn2_w_down, ple_norm, ple_w_proj, ple_w_gate, final_norm):
    batch, seq, d = x.shape
    depth = p.shape[0]
    t = batch * seq
    tm = _token_tile(t)
    tf = _ff_tile(ffn1_w_down.shape[1])
    cps = _chunks_per_step(seq)
    bf = lambda a: a.astype(BF16)
    row = lambda a: a.reshape(1, -1)
    seq3 = lambda a: a.reshape(batch, seq, a.shape[-1])

    h = x.reshape(t, d)
    for i in range(depth):
        h = _ffn(h, row(ffn1_norm[i]), bf(ffn1_w_gu[i]), bf(ffn1_w_down[i]), tm, tf)

        gdn_qkv, gdn_z, ml_qk, ml_v, ml_o, s5_u, moba_qkv, small = _inproj(
            h, row(mix_norm[i]), bf(_permute_w_in(w_in[i])), tm)
        y_gdn = _gdn(seq3(gdn_qkv), seq3(gdn_z), seq3(small), gdn_conv[i], row(gdn_a_log[i]),
                     row(gdn_dt_bias[i]), row(gdn_norm[i]), cps)
        y_mlstm = _mlstm(seq3(ml_qk), seq3(ml_v), seq3(ml_o), seq3(small), mlstm_conv[i],
                         row(mlstm_i_bias[i]), row(mlstm_f_bias[i]), row(mlstm_norm[i]), cps)
        s5_prm = dict(lam_re=s5_lambda_re[i], lam_im=s5_lambda_im[i], log_dt=s5_log_dt[i],
                      b_re=s5_b_re[i], b_im=s5_b_im[i], c_re=s5_c_re[i], c_im=s5_c_im[i],
                      d=row(s5_d[i]), w_glu=bf(s5_w_glu[i]), b_glu=row(s5_b_glu[i]))
        y_s5 = _s5(s5_u, batch, seq, s5_prm, tm)
        y_moba = _moba(seq3(moba_qkv), positions)
        ys = (y_gdn.reshape(t, BRANCH), y_mlstm.reshape(t, BRANCH), y_s5, y_moba.reshape(t, BRANCH))
        h = _merge(h, row(mix_norm[i]), ys, bf(w_gate[i]), bf(w_branch[i]), bf(w_out[i]), tm)

        h = _ffn(h, row(ffn2_norm[i]), bf(ffn2_w_gu[i]), bf(ffn2_w_down[i]), tm, tf)
        h = _ple(h, row(ple_norm[i]), p[i].reshape(t, -1), bf(ple_w_proj[i]), bf(ple_w_gate[i]),
                 row(final_norm), i == depth - 1, tm)
    return h.reshape(batch, seq, d)
```

```python
import functools
import math

import jax
import jax.numpy as jnp
from jax import lax
from jax.experimental import pallas as pl
from jax.experimental.pallas import tpu as pltpu

F32 = jnp.float32
BF16 = jnp.bfloat16
HIGHEST = lax.Precision.HIGHEST

NORM_EPS = 1e-6
HEADS = 4
HEAD_DIM = 64
BRANCH = HEADS * HEAD_DIM
CHUNK = 64
CONV_WIDTH = 4
CONV_HIST = 8
S5_GROUPS = 16
S5_CH = 16
S5_STATE = 64
S5_SUB = 4
MOBA_BLOCK = 256
MOBA_TOPK = 3
ROPE_THETA = 10000.0
LANE = 128
SMALL_W = LANE
V7X_VMEM_LIMIT = 56 * 1024 * 1024

NT_DIMS = (((1,), (1,)), ((), ()))


def _cparams(*sem):
    return pltpu.CompilerParams(dimension_semantics=sem, vmem_limit_bytes=V7X_VMEM_LIMIT)


def _rms(x, w):
    return x * lax.rsqrt(jnp.mean(x * x, axis=-1, keepdims=True) + NORM_EPS) * w


def _silu(x):
    return x * jax.nn.sigmoid(x)


def _softplus(x):
    return jnp.maximum(x, 0.0) + jnp.log1p(jnp.exp(-jnp.abs(x)))


def _dot(a, b):
    return jnp.dot(a, b, preferred_element_type=F32)


def _dot_bf(a, b):
    return jnp.dot(a.astype(BF16), b.astype(BF16), preferred_element_type=F32)


def _dot_nt_bf(a, b):
    return lax.dot_general(a.astype(BF16), b.astype(BF16), NT_DIMS, preferred_element_type=F32)


def _dot_nt_hi(a, b):
    return lax.dot_general(a, b, NT_DIMS, precision=HIGHEST, preferred_element_type=F32)


def _split_bf(a, parts):
    out = []
    for _ in range(parts):
        piece = a.astype(BF16)
        out.append(piece)
        a = a - piece.astype(F32)
    return out


def _dot_x3(a, b):
    a_hi, a_lo = _split_bf(a, 2)
    b_hi, b_lo = _split_bf(b, 2)
    return _dot(a_hi, b_hi) + _dot(a_hi, b_lo) + _dot(a_lo, b_hi)


def _dot_exact01(a01, b):
    hi, mid, lo = _split_bf(b, 3)
    return _dot(a01, hi) + _dot(a01, mid) + _dot(a01, lo)


def _iota(shape, dim):
    return lax.broadcasted_iota(jnp.int32, shape, dim)


def _head_block_ones():
    r = _iota((BRANCH, BRANCH), 0) // HEAD_DIM
    c = _iota((BRANCH, BRANCH), 1) // HEAD_DIM
    return (r == c).astype(BF16)


def _head_sums(x, ones_bd, parts):
    pieces = _split_bf(x, parts)
    out = _dot(pieces[0], ones_bd)
    for piece in pieces[1:]:
        out = out + _dot(piece, ones_bd)
    return out


def _head_bcast(cols):
    rows = cols.shape[0]
    head = _iota((rows, BRANCH), 1) // HEAD_DIM
    out = jnp.broadcast_to(cols[:, 0:1], (rows, BRANCH))
    for h in range(1, HEADS):
        out = jnp.where(head == h, jnp.broadcast_to(cols[:, h:h + 1], (rows, BRANCH)), out)
    return out


def _causal_conv_silu(x_ref, w_ref, pad_ref, first):
    rows = x_ref.shape[1]

    @pl.when(first)
    def _():
        pad_ref[0:CONV_HIST, :] = jnp.zeros((CONV_HIST, pad_ref.shape[1]), F32)

    pad_ref[CONV_HIST:CONV_HIST + rows, :] = x_ref[0]
    w = w_ref[...]
    acc = w[CONV_WIDTH - 1:CONV_WIDTH, :] * pad_ref[CONV_HIST:CONV_HIST + rows, :]
    for back in range(1, CONV_WIDTH):
        tap = CONV_WIDTH - 1 - back
        acc = acc + w[tap:tap + 1, :] * pad_ref[CONV_HIST - back:CONV_HIST - back + rows, :]
    pad_ref[0:CONV_HIST, :] = pad_ref[rows:rows + CONV_HIST, :]
    return _silu(acc)


def _tri_consts():
    r = _iota((CHUNK, CHUNK), 0)
    c = _iota((CHUNK, CHUNK), 1)
    lower = (c <= r).astype(F32)
    upper_s = (c > r).astype(F32)
    ones = jnp.ones((CHUNK, CHUNK), F32)
    return jnp.concatenate([lower, upper_s, ones], axis=0)


def _ffn_kernel(x_ref, nw_ref, wg_ref, wu_ref, wd_ref, o_ref, xn_ref):
    j = pl.program_id(1)

    @pl.when(j == 0)
    def _():
        xn_ref[...] = _rms(x_ref[...], nw_ref[...]).astype(BF16)

    xn = xn_ref[...]
    gate = _dot(xn, wg_ref[...])
    up = _dot(xn, wu_ref[...])
    part = 0.5 * _dot((_silu(gate) * up).astype(BF16), wd_ref[...])

    @pl.when(j == 0)
    def _():
        o_ref[...] = x_ref[...] + part

    @pl.when(j > 0)
    def _():
        o_ref[...] += part


def _ffn(h, norm_w, w_gu, w_down, tm, tf):
    t, d = h.shape
    f = w_down.shape[0]
    nf = f // tf
    return pl.pallas_call(
        _ffn_kernel,
        out_shape=jax.ShapeDtypeStruct((t, d), F32),
        grid=(t // tm, nf),
        in_specs=[
            pl.BlockSpec((tm, d), lambda i, j: (i, 0)),
            pl.BlockSpec((1, d), lambda i, j: (0, 0)),
            pl.BlockSpec((d, tf), lambda i, j: (0, j)),
            pl.BlockSpec((d, tf), lambda i, j, nf=nf: (0, j + nf)),
            pl.BlockSpec((tf, d), lambda i, j: (j, 0)),
        ],
        out_specs=pl.BlockSpec((tm, d), lambda i, j: (i, 0)),
        scratch_shapes=[pltpu.VMEM((tm, d), BF16)],
        compiler_params=_cparams("parallel", "arbitrary"),
        name="ffn",
    )(h, norm_w, w_gu, w_gu, w_down)


IN_SEGMENTS = (3 * BRANCH, BRANCH, 2 * BRANCH, BRANCH, BRANCH, BRANCH, 3 * BRANCH, SMALL_W)


def _inproj_kernel(x_ref, nw_ref, w_ref, *o_refs):
    xn = _rms(x_ref[...], nw_ref[...]).astype(BF16)
    start = 0
    for o_ref, width in zip(o_refs, IN_SEGMENTS):
        o_ref[...] = _dot(xn, w_ref[:, start:start + width])
        start += width


def _inproj(h, norm_w, w_perm, tm):
    t, d = h.shape
    n = w_perm.shape[1]
    return pl.pallas_call(
        _inproj_kernel,
        out_shape=[jax.ShapeDtypeStruct((t, w), F32) for w in IN_SEGMENTS],
        grid=(t // tm,),
        in_specs=[
            pl.BlockSpec((tm, d), lambda i: (i, 0)),
            pl.BlockSpec((1, d), lambda i: (0, 0)),
            pl.BlockSpec((d, n), lambda i: (0, 0)),
        ],
        out_specs=[pl.BlockSpec((tm, w), lambda i: (i, 0)) for w in IN_SEGMENTS],
        compiler_params=_cparams("parallel"),
        name="inproj",
    )(h, norm_w, w_perm)


def _merge_kernel(h_ref, nw_ref, y0_ref, y1_ref, y2_ref, y3_ref, wg_ref, wb_ref, wo_ref, o_ref):
    h = h_ref[...]
    u = _rms(h, nw_ref[...]).astype(BF16)
    merged = None
    for b, y_ref in enumerate((y0_ref, y1_ref, y2_ref, y3_ref)):
        term = jax.nn.sigmoid(_dot(u, wg_ref[b])) * _dot(y_ref[...].astype(BF16), wb_ref[b])
        merged = term if merged is None else merged + term
    o_ref[...] = h + _dot(merged.astype(BF16), wo_ref[...])


def _merge(h, norm_w, ys, w_gate, w_branch, w_out, tm):
    t, d = h.shape
    nb, bw, _ = w_branch.shape
    return pl.pallas_call(
        _merge_kernel,
        out_shape=jax.ShapeDtypeStruct((t, d), F32),
        grid=(t // tm,),
        in_specs=[
            pl.BlockSpec((tm, d), lambda i: (i, 0)),
            pl.BlockSpec((1, d), lambda i: (0, 0)),
        ] + [pl.BlockSpec((tm, bw), lambda i: (i, 0)) for _ in range(nb)] + [
            pl.BlockSpec((nb, d, d), lambda i: (0, 0, 0)),
            pl.BlockSpec((nb, bw, d), lambda i: (0, 0, 0)),
            pl.BlockSpec((d, d), lambda i: (0, 0)),
        ],
        out_specs=pl.BlockSpec((tm, d), lambda i: (i, 0)),
        compiler_params=_cparams("parallel"),
        name="merge",
    )(h, norm_w, *ys, w_gate, w_branch, w_out)


def _ple_kernel(h_ref, nw_ref, p_ref, wp_ref, wg_ref, fw_ref, o_ref, *, final):
    h = h_ref[...]
    gate = jax.nn.sigmoid(_dot(_rms(h, nw_ref[...]).astype(BF16), wg_ref[...]))
    out = h + _dot(p_ref[...].astype(BF16), wp_ref[...]) * gate
    if final:
        out = _rms(out, fw_ref[...])
    o_ref[...] = out


def _ple(h, norm_w, p, w_proj, w_gate, final_w, final, tm):
    t, d = h.shape
    pd = p.shape[1]
    return pl.pallas_call(
        functools.partial(_ple_kernel, final=final),
        out_shape=jax.ShapeDtypeStruct((t, d), F32),
        grid=(t // tm,),
        in_specs=[
            pl.BlockSpec((tm, d), lambda i: (i, 0)),
            pl.BlockSpec((1, d), lambda i: (0, 0)),
            pl.BlockSpec((tm, pd), lambda i: (i, 0)),
            pl.BlockSpec((pd, d), lambda i: (0, 0)),
            pl.BlockSpec((d, d), lambda i: (0, 0)),
            pl.BlockSpec((1, d), lambda i: (0, 0)),
        ],
        out_specs=pl.BlockSpec((tm, d), lambda i: (i, 0)),
        compiler_params=_cparams("parallel"),
        name="ple",
    )(h, norm_w, p, w_proj, w_gate, final_w)


def _gdn_kernel(qkv_ref, z_ref, sm_ref, cw_ref, alog_ref, dtb_ref, nw_ref, o_ref, pad_ref, state_ref, *, cps):
    step = pl.program_id(1)

    @pl.when(step == 0)
    def _():
        state_ref[...] = jnp.zeros(state_ref.shape, F32)

    rows = cps * CHUNK
    x = _causal_conv_silu(qkv_ref, cw_ref, pad_ref, step == 0)
    sm = sm_ref[0]
    beta = _head_bcast(jax.nn.sigmoid(sm[:, 0:HEADS]))
    g = _head_bcast(-jnp.exp(alog_ref[...]) * _softplus(sm[:, HEADS:2 * HEADS] + dtb_ref[...]))
    ones_bd = _head_block_ones()
    q, k, v = x[:, 0:BRANCH], x[:, BRANCH:2 * BRANCH], x[:, 2 * BRANCH:3 * BRANCH]
    q = q * lax.rsqrt(_head_sums(q * q, ones_bd, 2) + NORM_EPS) * (HEAD_DIM ** -0.5)
    k = k * lax.rsqrt(_head_sums(k * k, ones_bd, 2) + NORM_EPS)

    row_c = _iota((CHUNK, BRANCH), 0)
    col_c = _iota((CHUNK, BRANCH), 1) % HEAD_DIM
    tri = _tri_consts().astype(BF16)
    sums = []
    for ci in range(cps):
        gb = g[ci * CHUNK:(ci + 1) * CHUNK]
        sums.append(_dot_exact01(tri, jnp.concatenate([gb, jnp.where(row_c > col_c, gb, 0.0)], axis=1)))
    stack = lambda r0, c0: jnp.concatenate([s[r0:r0 + CHUNK, c0:c0 + BRANCH] for s in sums], axis=0)
    row = _iota((rows, BRANCH), 0) % CHUNK
    col = _iota((rows, BRANCH), 1) % HEAD_DIM
    e_gc = jnp.exp(stack(0, 0))
    e_rest = jnp.exp(stack(CHUNK, 0))
    e_total = jnp.exp(stack(2 * CHUNK, 0))
    decay = jnp.where(row >= col, jnp.exp(stack(0, BRANCH)), 0.0)
    k_beta = k * beta
    q_dec = q * e_gc
    k_dec = (k * e_rest).astype(BF16)
    rhs_u = v * beta
    rhs_w = k_beta * e_gc

    probs = [(ci, h) for ci in range(cps) for h in range(HEADS)]

    def cut(t, p):
        return t[p[0] * CHUNK:(p[0] + 1) * CHUNK, p[1] * HEAD_DIM:(p[1] + 1) * HEAD_DIM]

    eye = (_iota((CHUNK, CHUNK), 0) == _iota((CHUNK, CHUNK), 1)).astype(BF16)
    gram = {p: _dot_nt_bf(jnp.concatenate([cut(k_beta, p), cut(q, p)], axis=0), cut(k, p)) for p in probs}
    k_dec_t = {p: lax.dot_general(eye, cut(k_dec, p), NT_DIMS, preferred_element_type=F32) for p in probs}
    power, sol, qk = {}, {}, {}
    for p in probs:
        dec = cut(decay, p)
        power[p] = jnp.where(cut(row > col, p), gram[p][0:CHUNK] * dec, 0.0)
        qk[p] = gram[p][CHUNK:2 * CHUNK] * dec
        sol[p] = jnp.concatenate([cut(rhs_u, p), cut(rhs_w, p)], axis=1)
    n_factors = int(math.log2(CHUNK))
    for j in range(n_factors):
        last = j == n_factors - 1
        prods = {p: _dot_x3(power[p], sol[p] if last else jnp.concatenate([sol[p], power[p]], axis=1))
                 for p in probs}
        for p in probs:
            term = prods[p][:, 0:2 * HEAD_DIM]
            sol[p] = sol[p] - term if j == 0 else sol[p] + term
            if not last:
                power[p] = prods[p][:, 2 * HEAD_DIM:3 * HEAD_DIM]
    state = [state_ref[h] for h in range(HEADS)]
    outs = []
    for ci in range(cps):
        from_state = [_dot_bf(jnp.concatenate([sol[ci, h][:, HEAD_DIM:2 * HEAD_DIM], cut(q_dec, (ci, h))], axis=0),
                              state[h]) for h in range(HEADS)]
        v_new = [sol[ci, h][:, 0:HEAD_DIM] - from_state[h][0:CHUNK] for h in range(HEADS)]
        from_v = [_dot_bf(jnp.concatenate([qk[ci, h], k_dec_t[ci, h]], axis=0), v_new[h]) for h in range(HEADS)]
        for h in range(HEADS):
            state[h] = state[h] * cut(e_total, (ci, h)) + from_v[h][CHUNK:2 * CHUNK]
        outs.append(jnp.concatenate([from_state[h][CHUNK:2 * CHUNK] + from_v[h][0:CHUNK] for h in range(HEADS)],
                                    axis=1))
    for h in range(HEADS):
        state_ref[h] = state[h]
    out = jnp.concatenate(outs, axis=0)
    mean_sq = _head_sums(out * out, ones_bd, 2) * (1.0 / HEAD_DIM)
    nw = jnp.concatenate([nw_ref[...]] * HEADS, axis=1)
    o_ref[0] = out * lax.rsqrt(mean_sq + NORM_EPS) * nw * _silu(z_ref[0])


def _gdn(qkv, z, small, conv_w, a_log, dt_bias, norm_w, cps):
    b, s, _ = qkv.shape
    rows = cps * CHUNK
    return pl.pallas_call(
        functools.partial(_gdn_kernel, cps=cps),
        out_shape=jax.ShapeDtypeStruct((b, s, BRANCH), F32),
        grid=(b, s // rows),
        in_specs=[
            pl.BlockSpec((1, rows, 3 * BRANCH), lambda i, c: (i, c, 0)),
            pl.BlockSpec((1, rows, BRANCH), lambda i, c: (i, c, 0)),
            pl.BlockSpec((1, rows, SMALL_W), lambda i, c: (i, c, 0)),
            pl.BlockSpec((CONV_WIDTH, 3 * BRANCH), lambda i, c: (0, 0)),
            pl.BlockSpec((1, HEADS), lambda i, c: (0, 0)),
            pl.BlockSpec((1, HEADS), lambda i, c: (0, 0)),
            pl.BlockSpec((1, HEAD_DIM), lambda i, c: (0, 0)),
        ],
        out_specs=pl.BlockSpec((1, rows, BRANCH), lambda i, c: (i, c, 0)),
        scratch_shapes=[
            pltpu.VMEM((CONV_HIST + rows, 3 * BRANCH), F32),
            pltpu.VMEM((HEADS, HEAD_DIM, HEAD_DIM), F32),
        ],
        compiler_params=_cparams("parallel", "arbitrary"),
        name="gdn",
    )(qkv, z, small, conv_w, a_log, dt_bias, norm_w)


def _mlstm_kernel(qk_ref, v_ref, o_ref_in, sm_ref, cw_ref, ib_ref, fb_ref, nw_ref, out_ref,
                  pad_ref, c_ref, n_ref, m_ref, *, cps):
    step = pl.program_id(1)

    @pl.when(step == 0)
    def _():
        c_ref[...] = jnp.zeros(c_ref.shape, F32)
        n_ref[...] = jnp.zeros(n_ref.shape, F32)
        m_ref[...] = jnp.zeros(m_ref.shape, F32)

    rows = cps * CHUNK
    x = _causal_conv_silu(qk_ref, cw_ref, pad_ref, step == 0)
    q = x[:, 0:BRANCH]
    k = x[:, BRANCH:2 * BRANCH] * (HEAD_DIM ** -0.5)
    v = v_ref[0]
    sm = sm_ref[0]
    log_i = _head_bcast(sm[:, 2 * HEADS:3 * HEADS] + ib_ref[...])
    log_f = _head_bcast(-_softplus(-(sm[:, 3 * HEADS:4 * HEADS] + fb_ref[...])))
    ones_bd = _head_block_ones()

    row_c = _iota((CHUNK, BRANCH), 0)
    col_c = _iota((CHUNK, BRANCH), 1) % HEAD_DIM
    tri = _tri_consts().astype(BF16)
    sums = []
    for ci in range(cps):
        lf = log_f[ci * CHUNK:(ci + 1) * CHUNK]
        li = log_i[ci * CHUNK:(ci + 1) * CHUNK]
        sums.append(_dot_exact01(tri, jnp.concatenate(
            [lf, jnp.where(row_c > col_c, lf, 0.0), jnp.where(row_c == col_c, li, 0.0)], axis=1)))
    stack = lambda r0, c0: jnp.concatenate([s[r0:r0 + CHUNK, c0:c0 + BRANCH] for s in sums], axis=0)
    row = _iota((rows, BRANCH), 0) % CHUNK
    col = _iota((rows, BRANCH), 1) % HEAD_DIM
    b_col = stack(0, 0)
    b_last = stack(2 * CHUNK, 0)
    a_key = stack(CHUNK, 0) + log_i
    d_intra = jnp.where(row >= col, stack(0, BRANCH) + stack(2 * CHUNK, 2 * BRANCH), -jnp.inf)
    run = log_i - b_col
    shift = 1
    while shift < CHUNK:
        run = jnp.where(row >= shift, jnp.maximum(run, pltpu.roll(run, shift, axis=0)), run)
        shift *= 2
    m_mem = m_ref[...]
    m_prev, m_next, scale = [], [], []
    for ci in range(cps):
        last = ci * CHUNK + CHUNK - 1
        g_c = b_last[last:last + 1]
        m_new = jnp.maximum(g_c + m_mem, g_c + run[last:last + 1])
        m_prev.append(m_mem)
        m_next.append(m_new)
        scale.append(jnp.exp(g_c + m_mem - m_new))
        m_mem = m_new
    m_ref[...] = m_mem
    per_chunk = lambda rows_: jnp.concatenate([jnp.broadcast_to(r, (CHUNK, BRANCH)) for r in rows_], axis=0)
    m_inter = b_col + per_chunk(m_prev)
    m_t = jnp.maximum(m_inter, b_col + run)
    w_inter = jnp.exp(m_inter - m_t)
    e_intra = jnp.exp(d_intra - m_t)
    k_w = k * jnp.exp(a_key - per_chunk(m_next))
    k_w_bf = k_w.astype(BF16)

    probs = [(ci, h) for ci in range(cps) for h in range(HEADS)]

    def cut(t, p):
        return t[p[0] * CHUNK:(p[0] + 1) * CHUNK, p[1] * HEAD_DIM:(p[1] + 1) * HEAD_DIM]

    def paste(pieces):
        return jnp.concatenate([jnp.concatenate([pieces[ci, h] for h in range(HEADS)], axis=1)
                                for ci in range(cps)], axis=0)

    eye = (_iota((CHUNK, CHUNK), 0) == _iota((CHUNK, CHUNK), 1)).astype(BF16)
    s_qk = {p: _dot_nt_bf(cut(q, p), cut(k, p)) * cut(e_intra, p) for p in probs}
    k_w_t = {p: lax.dot_general(eye, cut(k_w_bf, p), NT_DIMS, preferred_element_type=F32) for p in probs}
    intra = {p: _dot_bf(s_qk[p], cut(v, p)) for p in probs}
    inject = {p: _dot_bf(k_w_t[p], cut(v, p)) for p in probs}
    c_mem = [c_ref[h] for h in range(HEADS)]
    n_mem = n_ref[...]
    c_prev, n_prev = {}, []
    for ci in range(cps):
        n_prev.append(n_mem)
        n_mem = n_mem * scale[ci] + jnp.sum(k_w[ci * CHUNK:(ci + 1) * CHUNK], axis=0, keepdims=True)
        for h in range(HEADS):
            c_prev[ci, h] = c_mem[h]
            c_mem[h] = c_mem[h] * scale[ci][:, h * HEAD_DIM:(h + 1) * HEAD_DIM] + inject[ci, h]
    for h in range(HEADS):
        c_ref[h] = c_mem[h]
    n_ref[...] = n_mem
    inter = {p: _dot_bf(cut(q, p), c_prev[p]) for p in probs}
    num = w_inter * paste(inter) + paste(intra)
    qn = w_inter * _head_sums(q * per_chunk(n_prev), ones_bd, 3) + _head_sums(paste(s_qk), ones_bd, 3)
    h_tilde = num / jnp.maximum(jnp.abs(qn), jnp.exp(-m_t))
    mean_sq = _head_sums(h_tilde * h_tilde, ones_bd, 2) * (1.0 / HEAD_DIM)
    out_ref[0] = jax.nn.sigmoid(o_ref_in[0]) * (h_tilde * lax.rsqrt(mean_sq + NORM_EPS) * nw_ref[...])


def _mlstm(qk, v, o_pre, small, conv_w, i_bias, f_bias, norm_w, cps):
    b, s, _ = qk.shape
    rows = cps * CHUNK
    return pl.pallas_call(
        functools.partial(_mlstm_kernel, cps=cps),
        out_shape=jax.ShapeDtypeStruct((b, s, BRANCH), F32),
        grid=(b, s // rows),
        in_specs=[
            pl.BlockSpec((1, rows, 2 * BRANCH), lambda i, c: (i, c, 0)),
            pl.BlockSpec((1, rows, BRANCH), lambda i, c: (i, c, 0)),
            pl.BlockSpec((1, rows, BRANCH), lambda i, c: (i, c, 0)),
            pl.BlockSpec((1, rows, SMALL_W), lambda i, c: (i, c, 0)),
            pl.BlockSpec((CONV_WIDTH, 2 * BRANCH), lambda i, c: (0, 0)),
            pl.BlockSpec((1, HEADS), lambda i, c: (0, 0)),
            pl.BlockSpec((1, HEADS), lambda i, c: (0, 0)),
            pl.BlockSpec((1, BRANCH), lambda i, c: (0, 0)),
        ],
        out_specs=pl.BlockSpec((1, rows, BRANCH), lambda i, c: (i, c, 0)),
        scratch_shapes=[
            pltpu.VMEM((CONV_HIST + rows, 2 * BRANCH), F32),
            pltpu.VMEM((HEADS, HEAD_DIM, HEAD_DIM), F32),
            pltpu.VMEM((1, BRANCH), F32),
            pltpu.VMEM((1, BRANCH), F32),
        ],
        compiler_params=_cparams("parallel", "arbitrary"),
        name="mlstm",
    )(qk, v, o_pre, small, conv_w, i_bias, f_bias, norm_w)


def _s5_prep_kernel(lre_ref, lim_ref, ldt_ref, btre_ref, btim_ref, cre_ref, cim_ref, wglu_ref,
                    toep_ref, wst_ref, wout_ref, al_ref, glu_ref):
    gn, gp = S5_GROUPS * S5_CH, S5_GROUPS * S5_STATE
    lr = lre_ref[...]
    li = lim_ref[...]
    dt = jnp.exp(ldt_ref[...])
    lam = lr * dt
    theta = li * dt

    def power(e):
        mg = jnp.exp(lam * e)
        return mg * jnp.cos(theta * e), mg * jnp.sin(theta * e)

    def cmul(xr, xi, yr, yi):
        return xr * yr - xi * yi, xr * yi + xi * yr

    a_re, a_im = power(1.0)
    den = lr * lr + li * li
    z_re = ((a_re - 1.0) * lr + a_im * li) / den
    z_im = (a_im * lr - (a_re - 1.0) * li) / den
    same_group = (_iota((gn, gp), 0) // S5_CH) == (_iota((gn, gp), 1) // S5_STATE)

    def spread(t):
        return jnp.where(same_group, jnp.concatenate([t] * S5_GROUPS, axis=1), 0.0)

    bb_re, bb_im = cmul(z_re, z_im, spread(btre_ref[...]), spread(btim_ref[...]))
    c_re, c_im = spread(cre_ref[...]), spread(cim_ref[...])
    for s in range(S5_SUB):
        wr, wi = cmul(*power(float(S5_SUB - 1 - s)), bb_re, bb_im)
        wst_ref[s * gn:(s + 1) * gn, :] = jnp.concatenate([wr, wi], axis=1).astype(BF16)
        wr, wi = cmul(*power(float(s + 1)), c_re, c_im)
        wout_ref[s * gn:(s + 1) * gn, :] = jnp.concatenate([wr, -wi], axis=1).astype(BF16)
    c_cat = jnp.concatenate([c_re, -c_im], axis=1)
    resp = []
    for tau in range(S5_SUB):
        wr, wi = cmul(*power(float(tau)), bb_re, bb_im)
        resp.append(_dot_nt_hi(jnp.concatenate([wr, wi], axis=1), c_cat))
    zero = jnp.zeros((gn, gn), F32)
    w_glu = wglu_ref[...]
    for s in range(S5_SUB):
        toep_ref[s * gn:(s + 1) * gn, :] = jnp.concatenate(
            [resp[t - s] if t >= s else zero for t in range(S5_SUB)], axis=1).astype(BF16)
        glu_ref[s * gn:(s + 1) * gn, :] = jnp.concatenate(
            [w_glu if t == s else zero for t in range(S5_SUB)], axis=1).astype(BF16)
    l_re, l_im = power(float(S5_SUB))
    al_ref[...] = jnp.concatenate(
        [jnp.concatenate([l_re, l_re], axis=1), jnp.concatenate([-l_im, l_im], axis=1),
         jnp.zeros((6, 2 * gp), F32)], axis=0)


def _s5_prep(lam_re, lam_im, log_dt, b_re, b_im, c_re, c_im, w_glu):
    gn, gp = S5_GROUPS * S5_CH, S5_GROUPS * S5_STATE
    flat = S5_SUB * gn
    row = lambda a: a.reshape(1, gp)
    full = lambda shape: pl.BlockSpec(shape, lambda i: (0,) * len(shape))
    return pl.pallas_call(
        _s5_prep_kernel,
        out_shape=[
            jax.ShapeDtypeStruct((flat, flat), BF16),
            jax.ShapeDtypeStruct((flat, 2 * gp), BF16),
            jax.ShapeDtypeStruct((flat, 2 * gp), BF16),
            jax.ShapeDtypeStruct((8, 2 * gp), F32),
            jax.ShapeDtypeStruct((flat, flat), BF16),
        ],
        grid=(1,),
        in_specs=[full((1, gp))] * 3 + [full((gn, S5_STATE))] * 4 + [full((gn, gn))],
        out_specs=[full((flat, flat)), full((flat, 2 * gp)), full((flat, 2 * gp)), full((8, 2 * gp)),
                   full((flat, flat))],
        compiler_params=_cparams("arbitrary"),
        name="s5_prep",
    )(row(lam_re), row(lam_im), row(jnp.broadcast_to(log_dt[:, None], lam_re.shape)),
      jnp.swapaxes(b_re, 1, 2).reshape(gn, S5_STATE), jnp.swapaxes(b_im, 1, 2).reshape(gn, S5_STATE),
      c_re.reshape(gn, S5_STATE), c_im.reshape(gn, S5_STATE), w_glu)


def _s5_main_kernel(x_ref, toep_ref, wst_ref, wout_ref, al_ref, glu_ref, d_ref, b_ref, o_ref,
                    inj_ref, prev_ref, state_ref):
    @pl.when(pl.program_id(0) == 0)
    def _():
        state_ref[...] = jnp.zeros(state_ref.shape, F32)

    nb, tc, flat = x_ref.shape
    gp = state_ref.shape[1] // 2
    x = x_ref[...].reshape(nb * tc, flat)
    xb = x.astype(BF16)
    n_tiles = inj_ref.shape[0]
    tiles = lambda t: [t[:, j * LANE:(j + 1) * LANE] for j in range(n_tiles)]
    for j, piece in enumerate(tiles(_dot(xb, wst_ref[...]))):
        inj_ref[j] = piece
    a_c = al_ref[0:1, :]
    a_s = al_ref[1:2, :]
    state = state_ref[...]
    for c in range(tc):
        rows = pl.ds(c, nb, stride=tc)
        for j, piece in enumerate(tiles(state)):
            prev_ref[j, rows, :] = piece
        inj = jnp.concatenate([inj_ref[j, rows, :] for j in range(n_tiles)], axis=1)
        state = state * a_c + pltpu.roll(state, gp, axis=1) * a_s + inj
    state_ref[...] = state
    prev = jnp.concatenate([prev_ref[j] for j in range(n_tiles)], axis=1)
    y = _dot(xb, toep_ref[...]) + _dot_nt_bf(prev, wout_ref[...])
    y = jax.nn.gelu(y + d_ref[...] * x)
    y = y * jax.nn.sigmoid(_dot(y.astype(BF16), glu_ref[...]) + b_ref[...])
    o_ref[...] = y.reshape(nb, tc, flat)


def _s5(u, batch, seq, prm):
    gp = S5_GROUPS * S5_STATE
    flat = S5_SUB * u.shape[1]
    rows = seq // S5_SUB
    tc = next(c for c in (32, 16, 8) if rows % c == 0)
    toep, wst, wout, al, glu = _s5_prep(prm["lam_re"], prm["lam_im"], prm["log_dt"], prm["b_re"], prm["b_im"],
                                        prm["c_re"], prm["c_im"], prm["w_glu"])
    full = lambda shape: pl.BlockSpec(shape, lambda i: (0,) * len(shape))
    tile = pl.BlockSpec((batch, tc, flat), lambda i: (0, i, 0))
    y = pl.pallas_call(
        _s5_main_kernel,
        out_shape=jax.ShapeDtypeStruct((batch, rows, flat), F32),
        grid=(rows // tc,),
        in_specs=[tile, full((flat, flat)), full((flat, 2 * gp)), full((flat, 2 * gp)), full((8, 2 * gp)),
                  full((flat, flat)), full((1, flat)), full((1, flat))],
        out_specs=tile,
        scratch_shapes=[pltpu.VMEM((2 * gp // LANE, batch * tc, LANE), F32),
                        pltpu.VMEM((2 * gp // LANE, batch * tc, LANE), F32),
                        pltpu.VMEM((batch, 2 * gp), F32)],
        compiler_params=_cparams("arbitrary"),
        name="s5_main",
    )(u.reshape(batch, rows, flat), toep, wst, wout, al, glu,
      jnp.tile(prm["d"], (1, S5_SUB)), jnp.tile(prm["b_glu"], (1, S5_SUB)))
    return y.reshape(batch * seq, u.shape[1])


def _moba_prep_kernel(q_ref, k_ref, v_ref, pos_ref, inv_ref, qo_ref, ko_ref, vt_ref, sel_ref, km_ref):
    n = pl.program_id(1)
    nbp = km_ref.shape[1]

    @pl.when(n == 0)
    def _():
        km_ref[...] = jnp.zeros(km_ref.shape, F32)

    ang = pos_ref[0] * inv_ref[...]
    cos = jnp.cos(ang)
    half = _iota(ang.shape, 1) < HEAD_DIM // 2
    sin = jnp.where(half, -jnp.sin(ang), jnp.sin(ang))

    def rope(x, h):
        xh = x[:, h * HEAD_DIM:(h + 1) * HEAD_DIM]
        swapped = jnp.concatenate([xh[:, HEAD_DIM // 2:], xh[:, :HEAD_DIM // 2]], axis=1)
        return xh * cos + swapped * sin

    q_all = q_ref[0]
    k_all = k_ref[0]
    blk_row = _iota((nbp, MOBA_BLOCK), 0)
    for h in range(HEADS):
        qh = rope(q_all, h)
        kh = rope(k_all, h)
        gate = _dot_nt_hi(km_ref[h], qh)
        rank = jnp.zeros((nbp, MOBA_BLOCK), jnp.int32)
        for m in range(nbp):
            gm = gate[m:m + 1, :]
            beats = (gm > gate) | ((gm == gate) & (m < blk_row))
            rank = rank + jnp.where(beats & (m < n), 1, 0)
        sel_ref[0, 0, h] = ((blk_row < n) & (rank < MOBA_TOPK)).astype(F32)
        qo_ref[0, 0, h] = (qh * (HEAD_DIM ** -0.5)).astype(BF16)
        ko_ref[0, 0, h] = kh.astype(BF16)
        km_ref[h, pl.ds(n, 1), :] = jnp.mean(kh, axis=0, keepdims=True)
    vt_ref[0, 0] = v_ref[0].T.astype(BF16)


def _moba_prep(qkv, pos, inv2):
    b, s, _ = qkv.shape
    nb = s // MOBA_BLOCK
    nbp = -(-nb // 8) * 8
    blk = lambda j: pl.BlockSpec((1, MOBA_BLOCK, BRANCH), lambda i, n, j=j: (i, n, j))
    heads = pl.BlockSpec((1, 1, HEADS, MOBA_BLOCK, HEAD_DIM), lambda i, n: (i, n, 0, 0, 0))
    return pl.pallas_call(
        _moba_prep_kernel,
        out_shape=[
            jax.ShapeDtypeStruct((b, nb, HEADS, MOBA_BLOCK, HEAD_DIM), BF16),
            jax.ShapeDtypeStruct((b, nb, HEADS, MOBA_BLOCK, HEAD_DIM), BF16),
            jax.ShapeDtypeStruct((b, nb, BRANCH, MOBA_BLOCK), BF16),
            jax.ShapeDtypeStruct((b, nb, HEADS, nbp, MOBA_BLOCK), F32),
        ],
        grid=(b, nb),
        in_specs=[blk(0), blk(1), blk(2),
                  pl.BlockSpec((1, MOBA_BLOCK, 1), lambda i, n: (i, n, 0)),
                  pl.BlockSpec((1, HEAD_DIM), lambda i, n: (0, 0))],
        out_specs=[heads, heads,
                   pl.BlockSpec((1, 1, BRANCH, MOBA_BLOCK), lambda i, n: (i, n, 0, 0)),
                   pl.BlockSpec((1, 1, HEADS, nbp, MOBA_BLOCK), lambda i, n: (i, n, 0, 0, 0))],
        scratch_shapes=[pltpu.VMEM((HEADS, nbp, HEAD_DIM), F32)],
        compiler_params=_cparams("parallel", "arbitrary"),
        name="moba_prep",
    )(qkv, qkv, qkv, pos, inv2)


def _moba_attn_kernel(q_ref, k_ref, vt_ref, sel_ref, o_ref):
    i = pl.program_id(1)
    blk = MOBA_BLOCK
    nbp = sel_ref.shape[3]
    key_i = _iota((blk, blk), 0)
    qry_i = _iota((blk, blk), 1)
    sel_row = _iota((nbp, blk), 0)

    def update(n, carries, masks):
        scores = [lax.dot_general(k_ref[0, n, h], q_ref[0, 0, h], NT_DIMS, preferred_element_type=F32)
                  for h in range(HEADS)]
        stats = []
        for h in range(HEADS):
            m_run, l_run, _ = carries[h]
            s = jnp.where(masks[h], scores[h], -jnp.inf)
            m_new = jnp.maximum(m_run, jnp.max(s, axis=0, keepdims=True))
            m_safe = jnp.where(m_new == -jnp.inf, 0.0, m_new)
            alpha = jnp.exp(m_run - m_safe)
            p = jnp.exp(s - m_safe)
            stats.append((m_new, alpha * l_run + jnp.sum(p, axis=0, keepdims=True), alpha, p.astype(BF16)))
        out = []
        for h in range(HEADS):
            m_new, l_new, alpha, p = stats[h]
            vt = vt_ref[0, n, h * HEAD_DIM:(h + 1) * HEAD_DIM, :]
            out.append((m_new, l_new, alpha * carries[h][2] + _dot(vt, p)))
        return tuple(out)

    def body(n, carries):
        masks = [jnp.max(jnp.where(sel_row == n, sel_ref[0, 0, h], 0.0), axis=0, keepdims=True) > 0.0
                 for h in range(HEADS)]
        return update(n, carries, masks)

    init = (jnp.full((1, blk), -jnp.inf, F32), jnp.zeros((1, blk), F32), jnp.zeros((HEAD_DIM, blk), F32))
    carries = lax.fori_loop(0, i, body, (init,) * HEADS)
    carries = update(i, carries, [key_i <= qry_i] * HEADS)
    o_ref[0] = jnp.concatenate([acc / l_run for _, l_run, acc in carries], axis=0).T


def _moba_attn(q, k, vt, sel):
    b, nb = q.shape[:2]
    nbp = sel.shape[3]
    return pl.pallas_call(
        _moba_attn_kernel,
        out_shape=jax.ShapeDtypeStruct((b, nb * MOBA_BLOCK, BRANCH), F32),
        grid=(b, nb),
        in_specs=[
            pl.BlockSpec((1, 1, HEADS, MOBA_BLOCK, HEAD_DIM), lambda i, n: (i, n, 0, 0, 0)),
            pl.BlockSpec((1, nb, HEADS, MOBA_BLOCK, HEAD_DIM), lambda i, n: (i, 0, 0, 0, 0)),
            pl.BlockSpec((1, nb, BRANCH, MOBA_BLOCK), lambda i, n: (i, 0, 0, 0)),
            pl.BlockSpec((1, 1, HEADS, nbp, MOBA_BLOCK), lambda i, n: (i, n, 0, 0, 0)),
        ],
        out_specs=pl.BlockSpec((1, MOBA_BLOCK, BRANCH), lambda i, n: (i, n, 0)),
        compiler_params=_cparams("parallel", "arbitrary"),
        name="moba_attn",
    )(q, k, vt, sel)


def _moba(qkv, positions):
    inv = ROPE_THETA ** (-jnp.arange(0, HEAD_DIM, 2, dtype=F32) / HEAD_DIM)
    inv2 = jnp.concatenate([inv, inv])[None, :]
    pos = positions.astype(F32)[..., None]
    return _moba_attn(*_moba_prep(qkv, pos, inv2))


def _permute_w_in(w_in):
    cuts = [0]
    for width in (3 * BRANCH, BRANCH, HEADS, HEADS, 2 * BRANCH, BRANCH, BRANCH, HEADS, HEADS, BRANCH, 3 * BRANCH):
        cuts.append(cuts[-1] + width)
    seg = [w_in[:, cuts[n]:cuts[n + 1]] for n in range(len(cuts) - 1)]
    gdn_qkv, gdn_z, gdn_b, gdn_a, ml_qk, ml_v, ml_o, ml_i, ml_f, s5_u, moba_qkv = seg
    pad = jnp.zeros((w_in.shape[0], SMALL_W - 4 * HEADS), w_in.dtype)
    return jnp.concatenate([gdn_qkv, gdn_z, ml_qk, ml_v, ml_o, s5_u, moba_qkv, gdn_b, gdn_a, ml_i, ml_f, pad], axis=1)


def _token_tile(t):
    for tm in (512, 256, 128, 64, 32, 16, 8):
        if t % tm == 0:
            return tm
    raise ValueError(f"token count {t} is not a multiple of 8")


def _chunks_per_step(seq):
    n_chunks = seq // CHUNK
    for cps in (4, 2, 1):
        if n_chunks % cps == 0:
            return cps


def _ff_tile(f):
    for parts in (2, 1):
        if f % parts == 0 and (f // parts) % 128 == 0:
            return f // parts
    raise ValueError(f"ffn width {f} is not a multiple of 128")


def kernel(x, p, positions, ffn1_norm, ffn1_w_gu, ffn1_w_down, mix_norm, w_in, gdn_conv, gdn_a_log, gdn_dt_bias, gdn_norm, mlstm_conv, mlstm_i_bias, mlstm_f_bias, mlstm_norm, s5_lambda_re, s5_lambda_im, s5_b_re, s5_b_im, s5_c_re, s5_c_im, s5_d, s5_log_dt, s5_w_glu, s5_b_glu, w_gate, w_branch, w_out, ffn2_norm, ffn2_w_gu, ffn2_w_down, ple_norm, ple_w_proj, ple_w_gate, final_norm):
    batch, seq, d = x.shape
    depth = p.shape[0]
    t = batch * seq
    tm = _token_tile(t)
    tf = _ff_tile(ffn1_w_down.shape[1])
    cps = _chunks_per_step(seq)
    bf = lambda a: a.astype(BF16)
    row = lambda a: a.reshape(1, -1)
    seq3 = lambda a: a.reshape(batch, seq, a.shape[-1])

    h = x.reshape(t, d)
    for i in range(depth):
        h = _ffn(h, row(ffn1_norm[i]), bf(ffn1_w_gu[i]), bf(ffn1_w_down[i]), tm, tf)

        gdn_qkv, gdn_z, ml_qk, ml_v, ml_o, s5_u, moba_qkv, small = _inproj(
            h, row(mix_norm[i]), bf(_permute_w_in(w_in[i])), tm)
        y_gdn = _gdn(seq3(gdn_qkv), seq3(gdn_z), seq3(small), gdn_conv[i], row(gdn_a_log[i]),
                     row(gdn_dt_bias[i]), row(gdn_norm[i]), cps)
        y_mlstm = _mlstm(seq3(ml_qk), seq3(ml_v), seq3(ml_o), seq3(small), mlstm_conv[i],
                         row(mlstm_i_bias[i]), row(mlstm_f_bias[i]), row(mlstm_norm[i]), cps)
        s5_prm = dict(lam_re=s5_lambda_re[i], lam_im=s5_lambda_im[i], log_dt=s5_log_dt[i],
                      b_re=s5_b_re[i], b_im=s5_b_im[i], c_re=s5_c_re[i], c_im=s5_c_im[i],
                      d=row(s5_d[i]), w_glu=s5_w_glu[i], b_glu=row(s5_b_glu[i]))
        y_s5 = _s5(s5_u, batch, seq, s5_prm)
        y_moba = _moba(seq3(moba_qkv), positions)
        ys = (y_gdn.reshape(t, BRANCH), y_mlstm.reshape(t, BRANCH), y_s5, y_moba.reshape(t, BRANCH))
        h = _merge(h, row(mix_norm[i]), ys, bf(w_gate[i]), bf(w_branch[i]), bf(w_out[i]), tm)

        h = _ffn(h, row(ffn2_norm[i]), bf(ffn2_w_gu[i]), bf(ffn2_w_down[i]), tm, tf)
        h = _ple(h, row(ple_norm[i]), p[i].reshape(t, -1), bf(ple_w_proj[i]), bf(ple_w_gate[i]),
                 row(final_norm), i == depth - 1, tm)
    return h.reshape(batch, seq, d)
```

```python
import functools
import math

import jax
import jax.numpy as jnp
from jax import lax
from jax.experimental import pallas as pl
from jax.experimental.pallas import tpu as pltpu

F32 = jnp.float32
BF16 = jnp.bfloat16
HIGHEST = lax.Precision.HIGHEST

NORM_EPS = 1e-6
HEADS = 4
HEAD_DIM = 64
BRANCH = HEADS * HEAD_DIM
CHUNK = 64
CONV_WIDTH = 4
CONV_HIST = 8
S5_GROUPS = 16
S5_CH = 16
S5_STATE = 64
S5_SUB = 4
MOBA_BLOCK = 256
MOBA_TOPK = 3
ROPE_THETA = 10000.0
LANE = 128
SMALL_W = LANE
V7X_VMEM_LIMIT = 56 * 1024 * 1024

NT_DIMS = (((1,), (1,)), ((), ()))


def _cparams(*sem):
    return pltpu.CompilerParams(dimension_semantics=sem, vmem_limit_bytes=V7X_VMEM_LIMIT)


def _rms(x, w):
    return x * lax.rsqrt(jnp.mean(x * x, axis=-1, keepdims=True) + NORM_EPS) * w


def _silu(x):
    return x * jax.nn.sigmoid(x)


def _softplus(x):
    return jnp.maximum(x, 0.0) + jnp.log1p(jnp.exp(-jnp.abs(x)))


def _dot(a, b):
    return jnp.dot(a, b, preferred_element_type=F32)


def _dot_bf(a, b):
    return jnp.dot(a.astype(BF16), b.astype(BF16), preferred_element_type=F32)


def _dot_nt_bf(a, b):
    return lax.dot_general(a.astype(BF16), b.astype(BF16), NT_DIMS, preferred_element_type=F32)


def _dot_nt_hi(a, b):
    return lax.dot_general(a, b, NT_DIMS, precision=HIGHEST, preferred_element_type=F32)


def _split_bf(a, parts):
    out = []
    for _ in range(parts):
        piece = a.astype(BF16)
        out.append(piece)
        a = a - piece.astype(F32)
    return out


def _dot_x3(a, b):
    a_hi, a_lo = _split_bf(a, 2)
    b_hi, b_lo = _split_bf(b, 2)
    return _dot(a_hi, b_hi) + _dot(a_hi, b_lo) + _dot(a_lo, b_hi)


def _dot_exact01(a01, b):
    hi, mid, lo = _split_bf(b, 3)
    return _dot(a01, hi) + _dot(a01, mid) + _dot(a01, lo)


def _iota(shape, dim):
    return lax.broadcasted_iota(jnp.int32, shape, dim)


def _head_block_ones():
    r = _iota((BRANCH, BRANCH), 0) // HEAD_DIM
    c = _iota((BRANCH, BRANCH), 1) // HEAD_DIM
    return (r == c).astype(BF16)


def _head_sums(x, ones_bd, parts):
    pieces = _split_bf(x, parts)
    out = _dot(pieces[0], ones_bd)
    for piece in pieces[1:]:
        out = out + _dot(piece, ones_bd)
    return out


def _head_bcast(cols):
    rows = cols.shape[0]
    head = _iota((rows, BRANCH), 1) // HEAD_DIM
    out = jnp.broadcast_to(cols[:, 0:1], (rows, BRANCH))
    for h in range(1, HEADS):
        out = jnp.where(head == h, jnp.broadcast_to(cols[:, h:h + 1], (rows, BRANCH)), out)
    return out


def _causal_conv_silu(x_ref, w_ref, pad_ref, first):
    rows = x_ref.shape[1]

    @pl.when(first)
    def _():
        pad_ref[0:CONV_HIST, :] = jnp.zeros((CONV_HIST, pad_ref.shape[1]), F32)

    pad_ref[CONV_HIST:CONV_HIST + rows, :] = x_ref[0]
    w = w_ref[...]
    acc = w[CONV_WIDTH - 1:CONV_WIDTH, :] * pad_ref[CONV_HIST:CONV_HIST + rows, :]
    for back in range(1, CONV_WIDTH):
        tap = CONV_WIDTH - 1 - back
        acc = acc + w[tap:tap + 1, :] * pad_ref[CONV_HIST - back:CONV_HIST - back + rows, :]
    pad_ref[0:CONV_HIST, :] = pad_ref[rows:rows + CONV_HIST, :]
    return _silu(acc)


def _tri_consts():
    r = _iota((CHUNK, CHUNK), 0)
    c = _iota((CHUNK, CHUNK), 1)
    lower = (c <= r).astype(F32)
    upper_s = (c > r).astype(F32)
    ones = jnp.ones((CHUNK, CHUNK), F32)
    return jnp.concatenate([lower, upper_s, ones], axis=0)


def _resident(shape):
    return pl.BlockSpec(shape, lambda *_: (0,) * len(shape), pipeline_mode=pl.Buffered(1))


def _ffn_kernel(x_ref, nw_ref, wgu_ref, wd_ref, *rest, sw, ple, final):
    o_ref = rest[-1]
    x = x_ref[...]
    xn = _rms(x, nw_ref[...]).astype(BF16)
    f = wd_ref.shape[0]

    def gate_up(j):
        return (_dot(xn, wgu_ref[:, j * sw:(j + 1) * sw]), _dot(xn, wgu_ref[:, f + j * sw:f + (j + 1) * sw]))

    ahead = gate_up(0)
    acc = None
    for j in range(f // sw):
        gate, up = ahead
        if (j + 1) * sw < f:
            ahead = gate_up(j + 1)
        part = _dot((_silu(gate) * up).astype(BF16), wd_ref[j * sw:(j + 1) * sw, :])
        acc = part if acc is None else acc + part
    out = x + 0.5 * acc
    if ple:
        pn_ref, p_ref, wp_ref, wg_ref, fw_ref = rest[:-1]
        gate = jax.nn.sigmoid(_dot(_rms(out, pn_ref[...]).astype(BF16), wg_ref[...]))
        out = out + _dot(p_ref[...].astype(BF16), wp_ref[...]) * gate
        if final:
            out = _rms(out, fw_ref[...])
    o_ref[...] = out


def _ffn(h, norm_w, w_gu, w_down, tm, sw, ple=None, final=False):
    t, d = h.shape
    f = w_down.shape[0]
    operands = [h, norm_w, w_gu, w_down]
    in_specs = [pl.BlockSpec((tm, d), lambda i: (i, 0)), _resident((1, d)), _resident((d, 2 * f)), _resident((f, d))]
    if ple is not None:
        pd = ple[1].shape[1]
        operands += list(ple)
        in_specs += [_resident((1, d)), pl.BlockSpec((tm, pd), lambda i: (i, 0)), _resident((pd, d)),
                     _resident((d, d)), _resident((1, d))]
    return pl.pallas_call(
        functools.partial(_ffn_kernel, sw=sw, ple=ple is not None, final=final),
        out_shape=jax.ShapeDtypeStruct((t, d), F32),
        grid=(t // tm,),
        in_specs=in_specs,
        out_specs=pl.BlockSpec((tm, d), lambda i: (i, 0)),
        compiler_params=_cparams("parallel"),
        name="ffn",
    )(*operands)


IN_SEGMENTS = (3 * BRANCH, BRANCH, 2 * BRANCH, BRANCH, BRANCH, BRANCH, 3 * BRANCH, SMALL_W)


def _inproj_kernel(x_ref, nw_ref, w_ref, *o_refs):
    xn = _rms(x_ref[...], nw_ref[...]).astype(BF16)
    start = 0
    for o_ref, width in zip(o_refs, IN_SEGMENTS):
        o_ref[...] = _dot(xn, w_ref[:, start:start + width])
        start += width


def _inproj(h, norm_w, w_perm, tm):
    t, d = h.shape
    n = w_perm.shape[1]
    return pl.pallas_call(
        _inproj_kernel,
        out_shape=[jax.ShapeDtypeStruct((t, w), F32) for w in IN_SEGMENTS],
        grid=(t // tm,),
        in_specs=[
            pl.BlockSpec((tm, d), lambda i: (i, 0)),
            pl.BlockSpec((1, d), lambda i: (0, 0)),
            pl.BlockSpec((d, n), lambda i: (0, 0)),
        ],
        out_specs=[pl.BlockSpec((tm, w), lambda i: (i, 0)) for w in IN_SEGMENTS],
        compiler_params=_cparams("parallel"),
        name="inproj",
    )(h, norm_w, w_perm)


def _merge_kernel(h_ref, nw_ref, y0_ref, y1_ref, y2_ref, y3_ref, wg_ref, wb_ref, wo_ref, o_ref):
    h = h_ref[...]
    u = _rms(h, nw_ref[...]).astype(BF16)
    merged = None
    for b, y_ref in enumerate((y0_ref, y1_ref, y2_ref, y3_ref)):
        term = jax.nn.sigmoid(_dot(u, wg_ref[b])) * _dot(y_ref[...].astype(BF16), wb_ref[b])
        merged = term if merged is None else merged + term
    o_ref[...] = h + _dot(merged.astype(BF16), wo_ref[...])


def _merge(h, norm_w, ys, w_gate, w_branch, w_out, tm):
    t, d = h.shape
    nb, bw, _ = w_branch.shape
    return pl.pallas_call(
        _merge_kernel,
        out_shape=jax.ShapeDtypeStruct((t, d), F32),
        grid=(t // tm,),
        in_specs=[
            pl.BlockSpec((tm, d), lambda i: (i, 0)),
            pl.BlockSpec((1, d), lambda i: (0, 0)),
        ] + [pl.BlockSpec((tm, bw), lambda i: (i, 0)) for _ in range(nb)] + [
            pl.BlockSpec((nb, d, d), lambda i: (0, 0, 0)),
            pl.BlockSpec((nb, bw, d), lambda i: (0, 0, 0)),
            pl.BlockSpec((d, d), lambda i: (0, 0)),
        ],
        out_specs=pl.BlockSpec((tm, d), lambda i: (i, 0)),
        compiler_params=_cparams("parallel"),
        name="merge",
    )(h, norm_w, *ys, w_gate, w_branch, w_out)


def _gdn_kernel(qkv_ref, z_ref, sm_ref, cw_ref, alog_ref, dtb_ref, nw_ref, o_ref, pad_ref, state_ref, *, cps):
    step = pl.program_id(1)

    @pl.when(step == 0)
    def _():
        state_ref[...] = jnp.zeros(state_ref.shape, F32)

    rows = cps * CHUNK
    x = _causal_conv_silu(qkv_ref, cw_ref, pad_ref, step == 0)
    sm = sm_ref[0]
    beta = _head_bcast(jax.nn.sigmoid(sm[:, 0:HEADS]))
    g = _head_bcast(-jnp.exp(alog_ref[...]) * _softplus(sm[:, HEADS:2 * HEADS] + dtb_ref[...]))
    ones_bd = _head_block_ones()
    q, k, v = x[:, 0:BRANCH], x[:, BRANCH:2 * BRANCH], x[:, 2 * BRANCH:3 * BRANCH]
    q = q * lax.rsqrt(_head_sums(q * q, ones_bd, 2) + NORM_EPS) * (HEAD_DIM ** -0.5)
    k = k * lax.rsqrt(_head_sums(k * k, ones_bd, 2) + NORM_EPS)

    row_c = _iota((CHUNK, BRANCH), 0)
    col_c = _iota((CHUNK, BRANCH), 1) % HEAD_DIM
    tri = _tri_consts().astype(BF16)
    sums = []
    for ci in range(cps):
        gb = g[ci * CHUNK:(ci + 1) * CHUNK]
        sums.append(_dot_exact01(tri, jnp.concatenate([gb, jnp.where(row_c > col_c, gb, 0.0)], axis=1)))
    stack = lambda r0, c0: jnp.concatenate([s[r0:r0 + CHUNK, c0:c0 + BRANCH] for s in sums], axis=0)
    row = _iota((rows, BRANCH), 0) % CHUNK
    col = _iota((rows, BRANCH), 1) % HEAD_DIM
    e_gc = jnp.exp(stack(0, 0))
    e_rest = jnp.exp(stack(CHUNK, 0))
    e_total = jnp.exp(stack(2 * CHUNK, 0))
    decay = jnp.where(row >= col, jnp.exp(stack(0, BRANCH)), 0.0)
    k_beta = k * beta
    q_dec = q * e_gc
    k_dec = (k * e_rest).astype(BF16)
    rhs_u = v * beta
    rhs_w = k_beta * e_gc

    probs = [(ci, h) for ci in range(cps) for h in range(HEADS)]

    def cut(t, p):
        return t[p[0] * CHUNK:(p[0] + 1) * CHUNK, p[1] * HEAD_DIM:(p[1] + 1) * HEAD_DIM]

    eye = (_iota((CHUNK, CHUNK), 0) == _iota((CHUNK, CHUNK), 1)).astype(BF16)
    gram = {p: _dot_nt_bf(jnp.concatenate([cut(k_beta, p), cut(q, p)], axis=0), cut(k, p)) for p in probs}
    k_dec_t = {p: lax.dot_general(eye, cut(k_dec, p), NT_DIMS, preferred_element_type=F32) for p in probs}
    power, sol, qk = {}, {}, {}
    for p in probs:
        dec = cut(decay, p)
        power[p] = jnp.where(cut(row > col, p), gram[p][0:CHUNK] * dec, 0.0)
        qk[p] = gram[p][CHUNK:2 * CHUNK] * dec
        sol[p] = jnp.concatenate([cut(rhs_u, p), cut(rhs_w, p)], axis=1)
    n_factors = int(math.log2(CHUNK))
    for j in range(n_factors):
        last = j == n_factors - 1
        prods = {p: _dot_x3(power[p], sol[p] if last else jnp.concatenate([sol[p], power[p]], axis=1))
                 for p in probs}
        for p in probs:
            term = prods[p][:, 0:2 * HEAD_DIM]
            sol[p] = sol[p] - term if j == 0 else sol[p] + term
            if not last:
                power[p] = prods[p][:, 2 * HEAD_DIM:3 * HEAD_DIM]
    state = [state_ref[h] for h in range(HEADS)]
    outs = []
    for ci in range(cps):
        from_state = [_dot_bf(jnp.concatenate([sol[ci, h][:, HEAD_DIM:2 * HEAD_DIM], cut(q_dec, (ci, h))], axis=0),
                              state[h]) for h in range(HEADS)]
        v_new = [sol[ci, h][:, 0:HEAD_DIM] - from_state[h][0:CHUNK] for h in range(HEADS)]
        from_v = [_dot_bf(jnp.concatenate([qk[ci, h], k_dec_t[ci, h]], axis=0), v_new[h]) for h in range(HEADS)]
        for h in range(HEADS):
            state[h] = state[h] * cut(e_total, (ci, h)) + from_v[h][CHUNK:2 * CHUNK]
        outs.append(jnp.concatenate([from_state[h][CHUNK:2 * CHUNK] + from_v[h][0:CHUNK] for h in range(HEADS)],
                                    axis=1))
    for h in range(HEADS):
        state_ref[h] = state[h]
    out = jnp.concatenate(outs, axis=0)
    mean_sq = _head_sums(out * out, ones_bd, 2) * (1.0 / HEAD_DIM)
    nw = jnp.concatenate([nw_ref[...]] * HEADS, axis=1)
    o_ref[0] = out * lax.rsqrt(mean_sq + NORM_EPS) * nw * _silu(z_ref[0])


def _gdn(qkv, z, small, conv_w, a_log, dt_bias, norm_w, cps):
    b, s, _ = qkv.shape
    rows = cps * CHUNK
    return pl.pallas_call(
        functools.partial(_gdn_kernel, cps=cps),
        out_shape=jax.ShapeDtypeStruct((b, s, BRANCH), F32),
        grid=(b, s // rows),
        in_specs=[
            pl.BlockSpec((1, rows, 3 * BRANCH), lambda i, c: (i, c, 0)),
            pl.BlockSpec((1, rows, BRANCH), lambda i, c: (i, c, 0)),
            pl.BlockSpec((1, rows, SMALL_W), lambda i, c: (i, c, 0)),
            pl.BlockSpec((CONV_WIDTH, 3 * BRANCH), lambda i, c: (0, 0)),
            pl.BlockSpec((1, HEADS), lambda i, c: (0, 0)),
            pl.BlockSpec((1, HEADS), lambda i, c: (0, 0)),
            pl.BlockSpec((1, HEAD_DIM), lambda i, c: (0, 0)),
        ],
        out_specs=pl.BlockSpec((1, rows, BRANCH), lambda i, c: (i, c, 0)),
        scratch_shapes=[
            pltpu.VMEM((CONV_HIST + rows, 3 * BRANCH), F32),
            pltpu.VMEM((HEADS, HEAD_DIM, HEAD_DIM), F32),
        ],
        compiler_params=_cparams("parallel", "arbitrary"),
        name="gdn",
    )(qkv, z, small, conv_w, a_log, dt_bias, norm_w)


def _mlstm_kernel(qk_ref, v_ref, o_ref_in, sm_ref, cw_ref, ib_ref, fb_ref, nw_ref, out_ref,
                  pad_ref, c_ref, n_ref, m_ref, *, cps):
    step = pl.program_id(1)

    @pl.when(step == 0)
    def _():
        c_ref[...] = jnp.zeros(c_ref.shape, F32)
        n_ref[...] = jnp.zeros(n_ref.shape, F32)
        m_ref[...] = jnp.zeros(m_ref.shape, F32)

    rows = cps * CHUNK
    x = _causal_conv_silu(qk_ref, cw_ref, pad_ref, step == 0)
    q = x[:, 0:BRANCH]
    k = x[:, BRANCH:2 * BRANCH] * (HEAD_DIM ** -0.5)
    v = v_ref[0]
    sm = sm_ref[0]
    log_i = _head_bcast(sm[:, 2 * HEADS:3 * HEADS] + ib_ref[...])
    log_f = _head_bcast(-_softplus(-(sm[:, 3 * HEADS:4 * HEADS] + fb_ref[...])))
    ones_bd = _head_block_ones()

    row_c = _iota((CHUNK, BRANCH), 0)
    col_c = _iota((CHUNK, BRANCH), 1) % HEAD_DIM
    tri = _tri_consts().astype(BF16)
    sums = []
    for ci in range(cps):
        lf = log_f[ci * CHUNK:(ci + 1) * CHUNK]
        li = log_i[ci * CHUNK:(ci + 1) * CHUNK]
        sums.append(_dot_exact01(tri, jnp.concatenate(
            [lf, jnp.where(row_c > col_c, lf, 0.0), jnp.where(row_c == col_c, li, 0.0)], axis=1)))
    stack = lambda r0, c0: jnp.concatenate([s[r0:r0 + CHUNK, c0:c0 + BRANCH] for s in sums], axis=0)
    row = _iota((rows, BRANCH), 0) % CHUNK
    col = _iota((rows, BRANCH), 1) % HEAD_DIM
    b_col = stack(0, 0)
    b_last = stack(2 * CHUNK, 0)
    a_key = stack(CHUNK, 0) + log_i
    d_intra = jnp.where(row >= col, stack(0, BRANCH) + stack(2 * CHUNK, 2 * BRANCH), -jnp.inf)
    run = log_i - b_col
    shift = 1
    while shift < CHUNK:
        run = jnp.where(row >= shift, jnp.maximum(run, pltpu.roll(run, shift, axis=0)), run)
        shift *= 2
    m_mem = m_ref[...]
    m_prev, m_next, scale = [], [], []
    for ci in range(cps):
        last = ci * CHUNK + CHUNK - 1
        g_c = b_last[last:last + 1]
        m_new = jnp.maximum(g_c + m_mem, g_c + run[last:last + 1])
        m_prev.append(m_mem)
        m_next.append(m_new)
        scale.append(jnp.exp(g_c + m_mem - m_new))
        m_mem = m_new
    m_ref[...] = m_mem
    per_chunk = lambda rows_: jnp.concatenate([jnp.broadcast_to(r, (CHUNK, BRANCH)) for r in rows_], axis=0)
    m_inter = b_col + per_chunk(m_prev)
    m_t = jnp.maximum(m_inter, b_col + run)
    w_inter = jnp.exp(m_inter - m_t)
    e_intra = jnp.exp(d_intra - m_t)
    k_w = k * jnp.exp(a_key - per_chunk(m_next))
    k_w_bf = k_w.astype(BF16)

    probs = [(ci, h) for ci in range(cps) for h in range(HEADS)]

    def cut(t, p):
        return t[p[0] * CHUNK:(p[0] + 1) * CHUNK, p[1] * HEAD_DIM:(p[1] + 1) * HEAD_DIM]

    def paste(pieces):
        return jnp.concatenate([jnp.concatenate([pieces[ci, h] for h in range(HEADS)], axis=1)
                                for ci in range(cps)], axis=0)

    eye = (_iota((CHUNK, CHUNK), 0) == _iota((CHUNK, CHUNK), 1)).astype(BF16)
    s_qk = {p: _dot_nt_bf(cut(q, p), cut(k, p)) * cut(e_intra, p) for p in probs}
    k_w_t = {p: lax.dot_general(eye, cut(k_w_bf, p), NT_DIMS, preferred_element_type=F32) for p in probs}
    intra = {p: _dot_bf(s_qk[p], cut(v, p)) for p in probs}
    inject = {p: _dot_bf(k_w_t[p], cut(v, p)) for p in probs}
    c_mem = [c_ref[h] for h in range(HEADS)]
    n_mem = n_ref[...]
    c_prev, n_prev = {}, []
    for ci in range(cps):
        n_prev.append(n_mem)
        n_mem = n_mem * scale[ci] + jnp.sum(k_w[ci * CHUNK:(ci + 1) * CHUNK], axis=0, keepdims=True)
        for h in range(HEADS):
            c_prev[ci, h] = c_mem[h]
            c_mem[h] = c_mem[h] * scale[ci][:, h * HEAD_DIM:(h + 1) * HEAD_DIM] + inject[ci, h]
    for h in range(HEADS):
        c_ref[h] = c_mem[h]
    n_ref[...] = n_mem
    inter = {p: _dot_bf(cut(q, p), c_prev[p]) for p in probs}
    num = w_inter * paste(inter) + paste(intra)
    qn = w_inter * _head_sums(q * per_chunk(n_prev), ones_bd, 3) + _head_sums(paste(s_qk), ones_bd, 3)
    h_tilde = num / jnp.maximum(jnp.abs(qn), jnp.exp(-m_t))
    mean_sq = _head_sums(h_tilde * h_tilde, ones_bd, 2) * (1.0 / HEAD_DIM)
    out_ref[0] = jax.nn.sigmoid(o_ref_in[0]) * (h_tilde * lax.rsqrt(mean_sq + NORM_EPS) * nw_ref[...])


def _mlstm(qk, v, o_pre, small, conv_w, i_bias, f_bias, norm_w, cps):
    b, s, _ = qk.shape
    rows = cps * CHUNK
    return pl.pallas_call(
        functools.partial(_mlstm_kernel, cps=cps),
        out_shape=jax.ShapeDtypeStruct((b, s, BRANCH), F32),
        grid=(b, s // rows),
        in_specs=[
            pl.BlockSpec((1, rows, 2 * BRANCH), lambda i, c: (i, c, 0)),
            pl.BlockSpec((1, rows, BRANCH), lambda i, c: (i, c, 0)),
            pl.BlockSpec((1, rows, BRANCH), lambda i, c: (i, c, 0)),
            pl.BlockSpec((1, rows, SMALL_W), lambda i, c: (i, c, 0)),
            pl.BlockSpec((CONV_WIDTH, 2 * BRANCH), lambda i, c: (0, 0)),
            pl.BlockSpec((1, HEADS), lambda i, c: (0, 0)),
            pl.BlockSpec((1, HEADS), lambda i, c: (0, 0)),
            pl.BlockSpec((1, BRANCH), lambda i, c: (0, 0)),
        ],
        out_specs=pl.BlockSpec((1, rows, BRANCH), lambda i, c: (i, c, 0)),
        scratch_shapes=[
            pltpu.VMEM((CONV_HIST + rows, 2 * BRANCH), F32),
            pltpu.VMEM((HEADS, HEAD_DIM, HEAD_DIM), F32),
            pltpu.VMEM((1, BRANCH), F32),
            pltpu.VMEM((1, BRANCH), F32),
        ],
        compiler_params=_cparams("parallel", "arbitrary"),
        name="mlstm",
    )(qk, v, o_pre, small, conv_w, i_bias, f_bias, norm_w)


def _s5_prep_kernel(lre_ref, lim_ref, ldt_ref, btre_ref, btim_ref, cre_ref, cim_ref, wglu_ref,
                    toep_ref, wst_ref, wout_ref, al_ref, glu_ref):
    gn, gp = S5_GROUPS * S5_CH, S5_GROUPS * S5_STATE
    lr = lre_ref[...]
    li = lim_ref[...]
    dt = jnp.exp(ldt_ref[...])
    lam = lr * dt
    theta = li * dt

    def power(e):
        mg = jnp.exp(lam * e)
        return mg * jnp.cos(theta * e), mg * jnp.sin(theta * e)

    def cmul(xr, xi, yr, yi):
        return xr * yr - xi * yi, xr * yi + xi * yr

    a_re, a_im = power(1.0)
    den = lr * lr + li * li
    z_re = ((a_re - 1.0) * lr + a_im * li) / den
    z_im = (a_im * lr - (a_re - 1.0) * li) / den
    same_group = (_iota((gn, gp), 0) // S5_CH) == (_iota((gn, gp), 1) // S5_STATE)

    def spread(t):
        return jnp.where(same_group, jnp.concatenate([t] * S5_GROUPS, axis=1), 0.0)

    bb_re, bb_im = cmul(z_re, z_im, spread(btre_ref[...]), spread(btim_ref[...]))
    c_re, c_im = spread(cre_ref[...]), spread(cim_ref[...])
    for s in range(S5_SUB):
        wr, wi = cmul(*power(float(S5_SUB - 1 - s)), bb_re, bb_im)
        wst_ref[s * gn:(s + 1) * gn, :] = jnp.concatenate([wr, wi], axis=1).astype(BF16)
        wr, wi = cmul(*power(float(s + 1)), c_re, c_im)
        wout_ref[s * gn:(s + 1) * gn, :] = jnp.concatenate([wr, -wi], axis=1).astype(BF16)
    c_cat = jnp.concatenate([c_re, -c_im], axis=1)
    resp = []
    for tau in range(S5_SUB):
        wr, wi = cmul(*power(float(tau)), bb_re, bb_im)
        resp.append(_dot_nt_hi(jnp.concatenate([wr, wi], axis=1), c_cat))
    zero = jnp.zeros((gn, gn), F32)
    w_glu = wglu_ref[...]
    for s in range(S5_SUB):
        toep_ref[s * gn:(s + 1) * gn, :] = jnp.concatenate(
            [resp[t - s] if t >= s else zero for t in range(S5_SUB)], axis=1).astype(BF16)
        glu_ref[s * gn:(s + 1) * gn, :] = jnp.concatenate(
            [w_glu if t == s else zero for t in range(S5_SUB)], axis=1).astype(BF16)
    l_re, l_im = power(float(S5_SUB))
    al_ref[...] = jnp.concatenate(
        [jnp.concatenate([l_re, l_re], axis=1), jnp.concatenate([-l_im, l_im], axis=1),
         jnp.zeros((6, 2 * gp), F32)], axis=0)


def _s5_prep(lam_re, lam_im, log_dt, b_re, b_im, c_re, c_im, w_glu):
    gn, gp = S5_GROUPS * S5_CH, S5_GROUPS * S5_STATE
    flat = S5_SUB * gn
    row = lambda a: a.reshape(1, gp)
    full = lambda shape: pl.BlockSpec(shape, lambda i: (0,) * len(shape))
    return pl.pallas_call(
        _s5_prep_kernel,
        out_shape=[
            jax.ShapeDtypeStruct((flat, flat), BF16),
            jax.ShapeDtypeStruct((flat, 2 * gp), BF16),
            jax.ShapeDtypeStruct((flat, 2 * gp), BF16),
            jax.ShapeDtypeStruct((8, 2 * gp), F32),
            jax.ShapeDtypeStruct((flat, flat), BF16),
        ],
        grid=(1,),
        in_specs=[full((1, gp))] * 3 + [full((gn, S5_STATE))] * 4 + [full((gn, gn))],
        out_specs=[full((flat, flat)), full((flat, 2 * gp)), full((flat, 2 * gp)), full((8, 2 * gp)),
                   full((flat, flat))],
        compiler_params=_cparams("arbitrary"),
        name="s5_prep",
    )(row(lam_re), row(lam_im), row(jnp.broadcast_to(log_dt[:, None], lam_re.shape)),
      jnp.swapaxes(b_re, 1, 2).reshape(gn, S5_STATE), jnp.swapaxes(b_im, 1, 2).reshape(gn, S5_STATE),
      c_re.reshape(gn, S5_STATE), c_im.reshape(gn, S5_STATE), w_glu)


def _s5_main_kernel(x_ref, toep_ref, wst_ref, wout_ref, al_ref, glu_ref, d_ref, b_ref, o_ref,
                    inj_ref, prev_ref, state_ref):
    @pl.when(pl.program_id(0) == 0)
    def _():
        state_ref[...] = jnp.zeros(state_ref.shape, F32)

    nb, tc, flat = x_ref.shape
    gp = state_ref.shape[1] // 2
    x = x_ref[...].reshape(nb * tc, flat)
    xb = x.astype(BF16)
    n_tiles = inj_ref.shape[0]
    tiles = lambda t: [t[:, j * LANE:(j + 1) * LANE] for j in range(n_tiles)]
    for j, piece in enumerate(tiles(_dot(xb, wst_ref[...]))):
        inj_ref[j] = piece
    a_c = al_ref[0:1, :]
    a_s = al_ref[1:2, :]
    state = state_ref[...]
    for c in range(tc):
        rows = pl.ds(c, nb, stride=tc)
        for j, piece in enumerate(tiles(state)):
            prev_ref[j, rows, :] = piece
        inj = jnp.concatenate([inj_ref[j, rows, :] for j in range(n_tiles)], axis=1)
        state = state * a_c + pltpu.roll(state, gp, axis=1) * a_s + inj
    state_ref[...] = state
    prev = jnp.concatenate([prev_ref[j] for j in range(n_tiles)], axis=1)
    y = _dot(xb, toep_ref[...]) + _dot_nt_bf(prev, wout_ref[...])
    y = jax.nn.gelu(y + d_ref[...] * x)
    y = y * jax.nn.sigmoid(_dot(y.astype(BF16), glu_ref[...]) + b_ref[...])
    o_ref[...] = y.reshape(nb, tc, flat)


def _s5(u, batch, seq, prm):
    gp = S5_GROUPS * S5_STATE
    flat = S5_SUB * u.shape[1]
    rows = seq // S5_SUB
    tc = next(c for c in (32, 16, 8) if rows % c == 0)
    toep, wst, wout, al, glu = _s5_prep(prm["lam_re"], prm["lam_im"], prm["log_dt"], prm["b_re"], prm["b_im"],
                                        prm["c_re"], prm["c_im"], prm["w_glu"])
    full = lambda shape: pl.BlockSpec(shape, lambda i: (0,) * len(shape))
    tile = pl.BlockSpec((batch, tc, flat), lambda i: (0, i, 0))
    y = pl.pallas_call(
        _s5_main_kernel,
        out_shape=jax.ShapeDtypeStruct((batch, rows, flat), F32),
        grid=(rows // tc,),
        in_specs=[tile, full((flat, flat)), full((flat, 2 * gp)), full((flat, 2 * gp)), full((8, 2 * gp)),
                  full((flat, flat)), full((1, flat)), full((1, flat))],
        out_specs=tile,
        scratch_shapes=[pltpu.VMEM((2 * gp // LANE, batch * tc, LANE), F32),
                        pltpu.VMEM((2 * gp // LANE, batch * tc, LANE), F32),
                        pltpu.VMEM((batch, 2 * gp), F32)],
        compiler_params=_cparams("arbitrary"),
        name="s5_main",
    )(u.reshape(batch, rows, flat), toep, wst, wout, al, glu,
      jnp.tile(prm["d"], (1, S5_SUB)), jnp.tile(prm["b_glu"], (1, S5_SUB)))
    return y.reshape(batch * seq, u.shape[1])


def _moba_prep_kernel(q_ref, k_ref, v_ref, pos_ref, inv_ref, qo_ref, ko_ref, vt_ref, sel_ref, km_ref):
    n = pl.program_id(1)
    nbp = km_ref.shape[1]

    @pl.when(n == 0)
    def _():
        km_ref[...] = jnp.zeros(km_ref.shape, F32)

    half = HEAD_DIM // 2
    ang = inv_ref[...] * pos_ref[0, 0]
    cos_t, sin_t = jnp.cos(ang), jnp.sin(ang)
    pairs_per_slab = LANE // HEAD_DIM
    cos = jnp.concatenate([cos_t, cos_t] * pairs_per_slab, axis=0).T
    sin = jnp.concatenate([-sin_t, sin_t] * pairs_per_slab, axis=0).T
    lane = _iota((MOBA_BLOCK, LANE), 1)
    low_half = (lane % HEAD_DIM) < half

    def rope(x):
        swapped = jnp.where(low_half, pltpu.roll(x, LANE - half, axis=1), pltpu.roll(x, half, axis=1))
        return x * cos + swapped * sin

    blk_row = _iota((nbp, MOBA_BLOCK), 0)
    for slab in range(BRANCH // LANE):
        q = rope(q_ref[0, :, slab * LANE:(slab + 1) * LANE])
        k = rope(k_ref[0, :, slab * LANE:(slab + 1) * LANE])
        qo_ref[0, 0, slab] = (q * (HEAD_DIM ** -0.5)).astype(BF16)
        k_mean = jnp.mean(k, axis=0, keepdims=True)
        for sub in range(pairs_per_slab):
            h = slab * pairs_per_slab + sub
            own = (lane // HEAD_DIM) == sub
            gate = _dot_nt_hi(km_ref[h], q)
            rank = jnp.zeros((nbp, MOBA_BLOCK), jnp.int32)
            for m in range(nbp):
                gm = gate[m:m + 1, :]
                beats = (gm > gate) | ((gm == gate) & (m < blk_row))
                rank = rank + jnp.where(beats & (m < n), 1, 0)
            sel_ref[0, 0, h] = ((blk_row < n) & (rank < MOBA_TOPK)).astype(F32)
            ko_ref[0, 0, h] = jnp.where(own, k, 0.0).astype(BF16)
            km_ref[h, pl.ds(n, 1), :] = jnp.where(own[0:1], k_mean, 0.0)
    vt_ref[0, 0] = v_ref[0].T.astype(BF16)


def _moba_prep(qkv, pos, inv):
    b, s, _ = qkv.shape
    nb = s // MOBA_BLOCK
    nbp = -(-nb // 8) * 8
    slabs = BRANCH // LANE
    blk = lambda j: pl.BlockSpec((1, MOBA_BLOCK, BRANCH), lambda i, n, j=j: (i, n, j))
    return pl.pallas_call(
        _moba_prep_kernel,
        out_shape=[
            jax.ShapeDtypeStruct((b, nb, slabs, MOBA_BLOCK, LANE), BF16),
            jax.ShapeDtypeStruct((b, nb, HEADS, MOBA_BLOCK, LANE), BF16),
            jax.ShapeDtypeStruct((b, nb, BRANCH, MOBA_BLOCK), BF16),
            jax.ShapeDtypeStruct((b, nb, HEADS, nbp, MOBA_BLOCK), F32),
        ],
        grid=(b, nb),
        in_specs=[blk(0), blk(1), blk(2),
                  pl.BlockSpec((1, 1, 1, MOBA_BLOCK), lambda i, n: (i, n, 0, 0)),
                  pl.BlockSpec((HEAD_DIM // 2, 1), lambda i, n: (0, 0))],
        out_specs=[pl.BlockSpec((1, 1, slabs, MOBA_BLOCK, LANE), lambda i, n: (i, n, 0, 0, 0)),
                   pl.BlockSpec((1, 1, HEADS, MOBA_BLOCK, LANE), lambda i, n: (i, n, 0, 0, 0)),
                   pl.BlockSpec((1, 1, BRANCH, MOBA_BLOCK), lambda i, n: (i, n, 0, 0)),
                   pl.BlockSpec((1, 1, HEADS, nbp, MOBA_BLOCK), lambda i, n: (i, n, 0, 0, 0))],
        scratch_shapes=[pltpu.VMEM((HEADS, nbp, LANE), F32)],
        compiler_params=_cparams("parallel", "arbitrary"),
        name="moba_prep",
    )(qkv, qkv, qkv, pos, inv)


def _moba_attn_kernel(q_ref, k_ref, vt_ref, sel_ref, o_ref):
    i = pl.program_id(1)
    blk = MOBA_BLOCK
    nbp = sel_ref.shape[3]
    key_i = _iota((blk, blk), 0)
    qry_i = _iota((blk, blk), 1)
    sel_row = _iota((nbp, blk), 0)

    def update(n, carries, masks):
        scores = [lax.dot_general(k_ref[0, n, h], q_ref[0, 0, h * HEAD_DIM // LANE], NT_DIMS,
                                  preferred_element_type=F32) for h in range(HEADS)]
        stats = []
        for h in range(HEADS):
            m_run, l_run, _ = carries[h]
            s = jnp.where(masks[h], scores[h], -jnp.inf)
            m_new = jnp.maximum(m_run, jnp.max(s, axis=0, keepdims=True))
            m_safe = jnp.where(m_new == -jnp.inf, 0.0, m_new)
            alpha = jnp.exp(m_run - m_safe)
            p = jnp.exp(s - m_safe)
            stats.append((m_new, alpha * l_run + jnp.sum(p, axis=0, keepdims=True), alpha, p.astype(BF16)))
        out = []
        for h in range(HEADS):
            m_new, l_new, alpha, p = stats[h]
            vt = vt_ref[0, n, h * HEAD_DIM:(h + 1) * HEAD_DIM, :]
            out.append((m_new, l_new, alpha * carries[h][2] + _dot(vt, p)))
        return tuple(out)

    def body(n, carries):
        masks = [jnp.max(jnp.where(sel_row == n, sel_ref[0, 0, h], 0.0), axis=0, keepdims=True) > 0.0
                 for h in range(HEADS)]
        return update(n, carries, masks)

    init = (jnp.full((1, blk), -jnp.inf, F32), jnp.zeros((1, blk), F32), jnp.zeros((HEAD_DIM, blk), F32))
    carries = lax.fori_loop(0, i, body, (init,) * HEADS)
    carries = update(i, carries, [key_i <= qry_i] * HEADS)
    o_ref[0] = jnp.concatenate([acc / l_run for _, l_run, acc in carries], axis=0).T


def _moba_attn(q, k, vt, sel):
    b, nb = q.shape[:2]
    nbp = sel.shape[3]
    return pl.pallas_call(
        _moba_attn_kernel,
        out_shape=jax.ShapeDtypeStruct((b, nb * MOBA_BLOCK, BRANCH), F32),
        grid=(b, nb),
        in_specs=[
            pl.BlockSpec((1, 1, BRANCH // LANE, MOBA_BLOCK, LANE), lambda i, n: (i, n, 0, 0, 0)),
            pl.BlockSpec((1, nb, HEADS, MOBA_BLOCK, LANE), lambda i, n: (i, 0, 0, 0, 0)),
            pl.BlockSpec((1, nb, BRANCH, MOBA_BLOCK), lambda i, n: (i, 0, 0, 0)),
            pl.BlockSpec((1, 1, HEADS, nbp, MOBA_BLOCK), lambda i, n: (i, n, 0, 0, 0)),
        ],
        out_specs=pl.BlockSpec((1, MOBA_BLOCK, BRANCH), lambda i, n: (i, n, 0)),
        compiler_params=_cparams("parallel", "arbitrary"),
        name="moba_attn",
    )(q, k, vt, sel)


def _moba(qkv, positions):
    b, s, _ = qkv.shape
    inv = ROPE_THETA ** (-jnp.arange(0, HEAD_DIM, 2, dtype=F32) / HEAD_DIM)
    pos = positions.astype(F32).reshape(b, s // MOBA_BLOCK, 1, MOBA_BLOCK)
    return _moba_attn(*_moba_prep(qkv, pos, inv[:, None]))


def _permute_w_in(w_in):
    cuts = [0]
    for width in (3 * BRANCH, BRANCH, HEADS, HEADS, 2 * BRANCH, BRANCH, BRANCH, HEADS, HEADS, BRANCH, 3 * BRANCH):
        cuts.append(cuts[-1] + width)
    seg = [w_in[:, cuts[n]:cuts[n + 1]] for n in range(len(cuts) - 1)]
    gdn_qkv, gdn_z, gdn_b, gdn_a, ml_qk, ml_v, ml_o, ml_i, ml_f, s5_u, moba_qkv = seg
    pad = jnp.zeros((w_in.shape[0], SMALL_W - 4 * HEADS), w_in.dtype)
    return jnp.concatenate([gdn_qkv, gdn_z, ml_qk, ml_v, ml_o, s5_u, moba_qkv, gdn_b, gdn_a, ml_i, ml_f, pad], axis=1)


def _token_tile(t, largest=512):
    for tm in (1024, 512, 256, 128, 64, 32, 16, 8):
        if tm > largest:
            continue
        if t % tm == 0:
            return tm
    raise ValueError(f"token count {t} is not a multiple of 8")


def _chunks_per_step(seq):
    n_chunks = seq // CHUNK
    for cps in (8, 4, 2, 1):
        if n_chunks % cps == 0:
            return cps


def _ff_slice(f):
    for sw in (2 * LANE, LANE):
        if f % sw == 0:
            return sw
    raise ValueError(f"ffn width {f} is not a multiple of {LANE}")


def kernel(x, p, positions, ffn1_norm, ffn1_w_gu, ffn1_w_down, mix_norm, w_in, gdn_conv, gdn_a_log, gdn_dt_bias, gdn_norm, mlstm_conv, mlstm_i_bias, mlstm_f_bias, mlstm_norm, s5_lambda_re, s5_lambda_im, s5_b_re, s5_b_im, s5_c_re, s5_c_im, s5_d, s5_log_dt, s5_w_glu, s5_b_glu, w_gate, w_branch, w_out, ffn2_norm, ffn2_w_gu, ffn2_w_down, ple_norm, ple_w_proj, ple_w_gate, final_norm):
    batch, seq, d = x.shape
    depth = p.shape[0]
    t = batch * seq
    tm = _token_tile(t)
    tm_ffn = _token_tile(t, largest=1024)
    tf = _ff_slice(ffn1_w_down.shape[1])
    cps = _chunks_per_step(seq)
    bf = lambda a: a.astype(BF16)
    row = lambda a: a.reshape(1, -1)
    seq3 = lambda a: a.reshape(batch, seq, a.shape[-1])

    h = x.reshape(t, d)
    for i in range(depth):
        h = _ffn(h, row(ffn1_norm[i]), bf(ffn1_w_gu[i]), bf(ffn1_w_down[i]), tm_ffn, tf)

        gdn_qkv, gdn_z, ml_qk, ml_v, ml_o, s5_u, moba_qkv, small = _inproj(
            h, row(mix_norm[i]), bf(_permute_w_in(w_in[i])), tm)
        y_gdn = _gdn(seq3(gdn_qkv), seq3(gdn_z), seq3(small), gdn_conv[i], row(gdn_a_log[i]),
                     row(gdn_dt_bias[i]), row(gdn_norm[i]), cps)
        y_mlstm = _mlstm(seq3(ml_qk), seq3(ml_v), seq3(ml_o), seq3(small), mlstm_conv[i],
                         row(mlstm_i_bias[i]), row(mlstm_f_bias[i]), row(mlstm_norm[i]), cps)
        s5_prm = dict(lam_re=s5_lambda_re[i], lam_im=s5_lambda_im[i], log_dt=s5_log_dt[i],
                      b_re=s5_b_re[i], b_im=s5_b_im[i], c_re=s5_c_re[i], c_im=s5_c_im[i],
                      d=row(s5_d[i]), w_glu=s5_w_glu[i], b_glu=row(s5_b_glu[i]))
        y_s5 = _s5(s5_u, batch, seq, s5_prm)
        y_moba = _moba(seq3(moba_qkv), positions)
        ys = (y_gdn.reshape(t, BRANCH), y_mlstm.reshape(t, BRANCH), y_s5, y_moba.reshape(t, BRANCH))
        h = _merge(h, row(mix_norm[i]), ys, bf(w_gate[i]), bf(w_branch[i]), bf(w_out[i]), tm)

        ple = (row(ple_norm[i]), p[i].reshape(t, -1), bf(ple_w_proj[i]), bf(ple_w_gate[i]), row(final_norm))
        h = _ffn(h, row(ffn2_norm[i]), bf(ffn2_w_gu[i]), bf(ffn2_w_down[i]), tm_ffn, tf, ple, i == depth - 1)
    return h.reshape(batch, seq, d)
```

```python
import functools
import math

import jax
import jax.numpy as jnp
from jax import lax
from jax.experimental import pallas as pl
from jax.experimental.pallas import tpu as pltpu

F32 = jnp.float32
BF16 = jnp.bfloat16
HIGHEST = lax.Precision.HIGHEST

NORM_EPS = 1e-6
HEADS = 4
HEAD_DIM = 64
BRANCH = HEADS * HEAD_DIM
CHUNK = 64
CONV_WIDTH = 4
CONV_HIST = 8
S5_GROUPS = 16
S5_CH = 16
S5_STATE = 64
S5_SUB = 4
MOBA_BLOCK = 256
MOBA_TOPK = 3
ROPE_THETA = 10000.0
LANE = 128
SMALL_W = LANE
V7X_VMEM_LIMIT = 56 * 1024 * 1024

NT_DIMS = (((1,), (1,)), ((), ()))


def _cparams(*sem):
    return pltpu.CompilerParams(dimension_semantics=sem, vmem_limit_bytes=V7X_VMEM_LIMIT)


def _rms(x, w):
    return x * lax.rsqrt(jnp.mean(x * x, axis=-1, keepdims=True) + NORM_EPS) * w


def _silu(x):
    return x * jax.nn.sigmoid(x)


def _softplus(x):
    return jnp.maximum(x, 0.0) + jnp.log1p(jnp.exp(-jnp.abs(x)))


def _dot(a, b):
    return jnp.dot(a, b, preferred_element_type=F32)


def _dot_bf(a, b):
    return jnp.dot(a.astype(BF16), b.astype(BF16), preferred_element_type=F32)


def _dot_nt_bf(a, b):
    return lax.dot_general(a.astype(BF16), b.astype(BF16), NT_DIMS, preferred_element_type=F32)


def _dot_nt_hi(a, b):
    return lax.dot_general(a, b, NT_DIMS, precision=HIGHEST, preferred_element_type=F32)


def _split_bf(a, parts):
    out = []
    for _ in range(parts):
        piece = a.astype(BF16)
        out.append(piece)
        a = a - piece.astype(F32)
    return out


def _dot_x3(a, b):
    a_hi, a_lo = _split_bf(a, 2)
    b_hi, b_lo = _split_bf(b, 2)
    return _dot(jnp.concatenate([a_hi, a_hi, a_lo], axis=1), jnp.concatenate([b_hi, b_lo, b_hi], axis=0))


def _dot_exact01(a01_x3, b):
    return _dot(a01_x3, jnp.concatenate(_split_bf(b, 3), axis=0))


def _iota(shape, dim):
    return lax.broadcasted_iota(jnp.int32, shape, dim)


def _head_block_ones():
    r = _iota((BRANCH, BRANCH), 0) // HEAD_DIM
    c = _iota((BRANCH, BRANCH), 1) // HEAD_DIM
    return (r == c).astype(BF16)


def _head_sums(x, ones_bd, parts):
    pieces = _split_bf(x, parts)
    out = _dot(pieces[0], ones_bd)
    for piece in pieces[1:]:
        out = out + _dot(piece, ones_bd)
    return out


def _head_bcast(cols):
    rows = cols.shape[0]
    head = _iota((rows, BRANCH), 1) // HEAD_DIM
    out = jnp.broadcast_to(cols[:, 0:1], (rows, BRANCH))
    for h in range(1, HEADS):
        out = jnp.where(head == h, jnp.broadcast_to(cols[:, h:h + 1], (rows, BRANCH)), out)
    return out


def _causal_conv_silu(x_ref, w_ref, pad_ref, first):
    rows = x_ref.shape[1]

    @pl.when(first)
    def _():
        pad_ref[0:CONV_HIST, :] = jnp.zeros((CONV_HIST, pad_ref.shape[1]), F32)

    pad_ref[CONV_HIST:CONV_HIST + rows, :] = x_ref[0]
    w = w_ref[...]
    acc = w[CONV_WIDTH - 1:CONV_WIDTH, :] * pad_ref[CONV_HIST:CONV_HIST + rows, :]
    for back in range(1, CONV_WIDTH):
        tap = CONV_WIDTH - 1 - back
        acc = acc + w[tap:tap + 1, :] * pad_ref[CONV_HIST - back:CONV_HIST - back + rows, :]
    pad_ref[0:CONV_HIST, :] = pad_ref[rows:rows + CONV_HIST, :]
    return _silu(acc)


def _tri_consts():
    r = _iota((3 * CHUNK, 3 * CHUNK), 0)
    k = _iota((3 * CHUNK, 3 * CHUNK), 1) % CHUNK
    i = r % CHUNK
    part = r // CHUNK
    return (((part == 0) & (k <= i)) | ((part == 1) & (k > i)) | (part == 2)).astype(BF16)


def _resident(shape):
    return pl.BlockSpec(shape, lambda *_: (0,) * len(shape), pipeline_mode=pl.Buffered(1))


def _ffn_kernel(x_ref, nw_ref, wgu_ref, wd_ref, *rest, sw, ple, final):
    o_ref = rest[-1]
    x = x_ref[...]
    xn = _rms(x, nw_ref[...]).astype(BF16)
    f = wd_ref.shape[0]

    def gate_up(j):
        return (_dot(xn, wgu_ref[:, j * sw:(j + 1) * sw]), _dot(xn, wgu_ref[:, f + j * sw:f + (j + 1) * sw]))

    ahead = gate_up(0)
    acc = None
    for j in range(f // sw):
        gate, up = ahead
        if (j + 1) * sw < f:
            ahead = gate_up(j + 1)
        part = _dot((_silu(gate) * up).astype(BF16), wd_ref[j * sw:(j + 1) * sw, :])
        acc = part if acc is None else acc + part
    out = x + 0.5 * acc
    if ple:
        pn_ref, p_ref, wp_ref, wg_ref, fw_ref = rest[:-1]
        gate = jax.nn.sigmoid(_dot(_rms(out, pn_ref[...]).astype(BF16), wg_ref[...]))
        out = out + _dot(p_ref[...].astype(BF16), wp_ref[...]) * gate
        if final:
            out = _rms(out, fw_ref[...])
    o_ref[...] = out


def _ffn(h, norm_w, w_gu, w_down, tm, sw, ple=None, final=False):
    t, d = h.shape
    f = w_down.shape[0]
    operands = [h, norm_w, w_gu, w_down]
    in_specs = [pl.BlockSpec((tm, d), lambda i: (i, 0)), _resident((1, d)), _resident((d, 2 * f)), _resident((f, d))]
    if ple is not None:
        pd = ple[1].shape[1]
        operands += list(ple)
        in_specs += [_resident((1, d)), pl.BlockSpec((tm, pd), lambda i: (i, 0)), _resident((pd, d)),
                     _resident((d, d)), _resident((1, d))]
    return pl.pallas_call(
        functools.partial(_ffn_kernel, sw=sw, ple=ple is not None, final=final),
        out_shape=jax.ShapeDtypeStruct((t, d), F32),
        grid=(t // tm,),
        in_specs=in_specs,
        out_specs=pl.BlockSpec((tm, d), lambda i: (i, 0)),
        compiler_params=_cparams("parallel"),
        name="ffn",
    )(*operands)


IN_SEGMENTS = (3 * BRANCH, BRANCH, 2 * BRANCH, BRANCH, BRANCH, BRANCH, 3 * BRANCH, SMALL_W)


def _inproj_kernel(x_ref, nw_ref, w_ref, *o_refs):
    xn = _rms(x_ref[...], nw_ref[...]).astype(BF16)
    start = 0
    for o_ref, width in zip(o_refs, IN_SEGMENTS):
        o_ref[...] = _dot(xn, w_ref[:, start:start + width])
        start += width


def _inproj(h, norm_w, w_perm, tm):
    t, d = h.shape
    n = w_perm.shape[1]
    return pl.pallas_call(
        _inproj_kernel,
        out_shape=[jax.ShapeDtypeStruct((t, w), F32) for w in IN_SEGMENTS],
        grid=(t // tm,),
        in_specs=[
            pl.BlockSpec((tm, d), lambda i: (i, 0)),
            pl.BlockSpec((1, d), lambda i: (0, 0)),
            pl.BlockSpec((d, n), lambda i: (0, 0)),
        ],
        out_specs=[pl.BlockSpec((tm, w), lambda i: (i, 0)) for w in IN_SEGMENTS],
        compiler_params=_cparams("parallel"),
        name="inproj",
    )(h, norm_w, w_perm)


def _merge_kernel(h_ref, nw_ref, y0_ref, y1_ref, y2_ref, y3_ref, wg_ref, wb_ref, wo_ref, o_ref):
    h = h_ref[...]
    u = _rms(h, nw_ref[...]).astype(BF16)
    merged = None
    for b, y_ref in enumerate((y0_ref, y1_ref, y2_ref, y3_ref)):
        term = jax.nn.sigmoid(_dot(u, wg_ref[b])) * _dot(y_ref[...].astype(BF16), wb_ref[b])
        merged = term if merged is None else merged + term
    o_ref[...] = h + _dot(merged.astype(BF16), wo_ref[...])


def _merge(h, norm_w, ys, w_gate, w_branch, w_out, tm):
    t, d = h.shape
    nb, bw, _ = w_branch.shape
    return pl.pallas_call(
        _merge_kernel,
        out_shape=jax.ShapeDtypeStruct((t, d), F32),
        grid=(t // tm,),
        in_specs=[
            pl.BlockSpec((tm, d), lambda i: (i, 0)),
            pl.BlockSpec((1, d), lambda i: (0, 0)),
        ] + [pl.BlockSpec((tm, bw), lambda i: (i, 0)) for _ in range(nb)] + [
            pl.BlockSpec((nb, d, d), lambda i: (0, 0, 0)),
            pl.BlockSpec((nb, bw, d), lambda i: (0, 0, 0)),
            pl.BlockSpec((d, d), lambda i: (0, 0)),
        ],
        out_specs=pl.BlockSpec((tm, d), lambda i: (i, 0)),
        compiler_params=_cparams("parallel"),
        name="merge",
    )(h, norm_w, *ys, w_gate, w_branch, w_out)


def _gdn_kernel(qkv_ref, z_ref, sm_ref, cw_ref, alog_ref, dtb_ref, nw_ref, o_ref, pad_ref, state_ref, *, cps):
    step = pl.program_id(1)

    @pl.when(step == 0)
    def _():
        state_ref[...] = jnp.zeros(state_ref.shape, F32)

    rows = cps * CHUNK
    x = _causal_conv_silu(qkv_ref, cw_ref, pad_ref, step == 0)
    sm = sm_ref[0]
    beta = _head_bcast(jax.nn.sigmoid(sm[:, 0:HEADS]))
    g = _head_bcast(-jnp.exp(alog_ref[...]) * _softplus(sm[:, HEADS:2 * HEADS] + dtb_ref[...]))
    ones_bd = _head_block_ones()
    q, k, v = x[:, 0:BRANCH], x[:, BRANCH:2 * BRANCH], x[:, 2 * BRANCH:3 * BRANCH]
    q = q * lax.rsqrt(_head_sums(q * q, ones_bd, 2) + NORM_EPS) * (HEAD_DIM ** -0.5)
    k = k * lax.rsqrt(_head_sums(k * k, ones_bd, 2) + NORM_EPS)

    row_c = _iota((CHUNK, BRANCH), 0)
    col_c = _iota((CHUNK, BRANCH), 1) % HEAD_DIM
    tri = _tri_consts()
    sums = []
    for ci in range(cps):
        gb = g[ci * CHUNK:(ci + 1) * CHUNK]
        sums.append(_dot_exact01(tri, jnp.concatenate([gb, jnp.where(row_c > col_c, gb, 0.0)], axis=1)))
    stack = lambda r0, c0: jnp.concatenate([s[r0:r0 + CHUNK, c0:c0 + BRANCH] for s in sums], axis=0)
    row = _iota((rows, BRANCH), 0) % CHUNK
    col = _iota((rows, BRANCH), 1) % HEAD_DIM
    e_gc = jnp.exp(stack(0, 0))
    e_rest = jnp.exp(stack(CHUNK, 0))
    e_total = jnp.exp(stack(2 * CHUNK, 0))
    decay = jnp.where(row >= col, jnp.exp(stack(0, BRANCH)), 0.0)
    k_beta = k * beta
    q_dec = q * e_gc
    k_dec = (k * e_rest).astype(BF16)
    rhs_u = v * beta
    rhs_w = k_beta * e_gc

    probs = [(ci, h) for ci in range(cps) for h in range(HEADS)]

    def cut(t, p):
        return t[p[0] * CHUNK:(p[0] + 1) * CHUNK, p[1] * HEAD_DIM:(p[1] + 1) * HEAD_DIM]

    eye = (_iota((CHUNK, CHUNK), 0) == _iota((CHUNK, CHUNK), 1)).astype(BF16)
    gram = {p: _dot_nt_bf(jnp.concatenate([cut(k_beta, p), cut(q, p)], axis=0), cut(k, p)) for p in probs}
    k_dec_t = {p: lax.dot_general(eye, cut(k_dec, p), NT_DIMS, preferred_element_type=F32) for p in probs}
    power, sol, qk = {}, {}, {}
    for p in probs:
        dec = cut(decay, p)
        power[p] = jnp.where(cut(row > col, p), gram[p][0:CHUNK] * dec, 0.0)
        qk[p] = gram[p][CHUNK:2 * CHUNK] * dec
        sol[p] = jnp.concatenate([cut(rhs_u, p), cut(rhs_w, p)], axis=1)
    n_factors = int(math.log2(CHUNK))
    for j in range(n_factors):
        last = j == n_factors - 1
        prods = {p: _dot_x3(power[p], sol[p] if last else jnp.concatenate([sol[p], power[p]], axis=1))
                 for p in probs}
        for p in probs:
            term = prods[p][:, 0:2 * HEAD_DIM]
            sol[p] = sol[p] - term if j == 0 else sol[p] + term
            if not last:
                power[p] = prods[p][:, 2 * HEAD_DIM:3 * HEAD_DIM]
    q_eff, out_loc, inject, mix = {}, {}, {}, {}
    both = {p: _dot_bf(jnp.concatenate([qk[p], k_dec_t[p]], axis=0), sol[p]) for p in probs}
    for p in probs:
        out_loc[p] = both[p][0:CHUNK, 0:HEAD_DIM]
        q_eff[p] = cut(q_dec, p) - both[p][0:CHUNK, HEAD_DIM:2 * HEAD_DIM]
        inject[p] = both[p][CHUNK:2 * CHUNK, 0:HEAD_DIM]
        mix[p] = both[p][CHUNK:2 * CHUNK, HEAD_DIM:2 * HEAD_DIM]
    state = [state_ref[h] for h in range(HEADS)]
    entering, from_state = {}, {}
    for ci in range(cps):
        mixed = [_dot_bf(mix[ci, h], state[h]) for h in range(HEADS)]
        if ci:
            from_state.update({(ci - 1, h): _dot_bf(q_eff[ci - 1, h], entering[ci - 1, h]) for h in range(HEADS)})
        for h in range(HEADS):
            entering[ci, h] = state[h]
            state[h] = state[h] * cut(e_total, (ci, h)) - mixed[h] + inject[ci, h]
    from_state.update({(cps - 1, h): _dot_bf(q_eff[cps - 1, h], entering[cps - 1, h]) for h in range(HEADS)})
    for h in range(HEADS):
        state_ref[h] = state[h]
    out = jnp.concatenate([jnp.concatenate([from_state[ci, h] + out_loc[ci, h] for h in range(HEADS)], axis=1)
                           for ci in range(cps)], axis=0)
    mean_sq = _head_sums(out * out, ones_bd, 2) * (1.0 / HEAD_DIM)
    nw = jnp.concatenate([nw_ref[...]] * HEADS, axis=1)
    o_ref[0] = out * lax.rsqrt(mean_sq + NORM_EPS) * nw * _silu(z_ref[0])


def _gdn(qkv, z, small, conv_w, a_log, dt_bias, norm_w, cps):
    b, s, _ = qkv.shape
    rows = cps * CHUNK
    return pl.pallas_call(
        functools.partial(_gdn_kernel, cps=cps),
        out_shape=jax.ShapeDtypeStruct((b, s, BRANCH), F32),
        grid=(b, s // rows),
        in_specs=[
            pl.BlockSpec((1, rows, 3 * BRANCH), lambda i, c: (i, c, 0)),
            pl.BlockSpec((1, rows, BRANCH), lambda i, c: (i, c, 0)),
            pl.BlockSpec((1, rows, SMALL_W), lambda i, c: (i, c, 0)),
            pl.BlockSpec((CONV_WIDTH, 3 * BRANCH), lambda i, c: (0, 0)),
            pl.BlockSpec((1, HEADS), lambda i, c: (0, 0)),
            pl.BlockSpec((1, HEADS), lambda i, c: (0, 0)),
            pl.BlockSpec((1, HEAD_DIM), lambda i, c: (0, 0)),
        ],
        out_specs=pl.BlockSpec((1, rows, BRANCH), lambda i, c: (i, c, 0)),
        scratch_shapes=[
            pltpu.VMEM((CONV_HIST + rows, 3 * BRANCH), F32),
            pltpu.VMEM((HEADS, HEAD_DIM, HEAD_DIM), F32),
        ],
        compiler_params=_cparams("parallel", "arbitrary"),
        name="gdn",
    )(qkv, z, small, conv_w, a_log, dt_bias, norm_w)


def _mlstm_kernel(qk_ref, v_ref, o_ref_in, sm_ref, cw_ref, ib_ref, fb_ref, nw_ref, out_ref,
                  pad_ref, c_ref, n_ref, m_ref, *, cps):
    step = pl.program_id(1)

    @pl.when(step == 0)
    def _():
        c_ref[...] = jnp.zeros(c_ref.shape, F32)
        n_ref[...] = jnp.zeros(n_ref.shape, F32)
        m_ref[...] = jnp.zeros(m_ref.shape, F32)

    rows = cps * CHUNK
    x = _causal_conv_silu(qk_ref, cw_ref, pad_ref, step == 0)
    q = x[:, 0:BRANCH]
    k = x[:, BRANCH:2 * BRANCH] * (HEAD_DIM ** -0.5)
    v = v_ref[0]
    sm = sm_ref[0]
    log_i = _head_bcast(sm[:, 2 * HEADS:3 * HEADS] + ib_ref[...])
    log_f = _head_bcast(-_softplus(-(sm[:, 3 * HEADS:4 * HEADS] + fb_ref[...])))
    ones_bd = _head_block_ones()

    row_c = _iota((CHUNK, BRANCH), 0)
    col_c = _iota((CHUNK, BRANCH), 1) % HEAD_DIM
    tri = _tri_consts()
    sums = []
    for ci in range(cps):
        lf = log_f[ci * CHUNK:(ci + 1) * CHUNK]
        li = log_i[ci * CHUNK:(ci + 1) * CHUNK]
        sums.append(_dot_exact01(tri, jnp.concatenate(
            [lf, jnp.where(row_c > col_c, lf, 0.0), jnp.where(row_c == col_c, li, 0.0)], axis=1)))
    stack = lambda r0, c0: jnp.concatenate([s[r0:r0 + CHUNK, c0:c0 + BRANCH] for s in sums], axis=0)
    row = _iota((rows, BRANCH), 0) % CHUNK
    col = _iota((rows, BRANCH), 1) % HEAD_DIM
    b_col = stack(0, 0)
    b_last = stack(2 * CHUNK, 0)
    a_key = stack(CHUNK, 0) + log_i
    d_intra = jnp.where(row >= col, stack(0, BRANCH) + stack(2 * CHUNK, 2 * BRANCH), -jnp.inf)
    run = log_i - b_col
    shift = 1
    while shift < CHUNK:
        run = jnp.where(row >= shift, jnp.maximum(run, pltpu.roll(run, shift, axis=0)), run)
        shift *= 2
    m_mem = m_ref[...]
    m_prev, m_next, scale = [], [], []
    for ci in range(cps):
        last = ci * CHUNK + CHUNK - 1
        g_c = b_last[last:last + 1]
        m_new = jnp.maximum(g_c + m_mem, g_c + run[last:last + 1])
        m_prev.append(m_mem)
        m_next.append(m_new)
        scale.append(jnp.exp(g_c + m_mem - m_new))
        m_mem = m_new
    m_ref[...] = m_mem
    per_chunk = lambda rows_: jnp.concatenate([jnp.broadcast_to(r, (CHUNK, BRANCH)) for r in rows_], axis=0)
    m_inter = b_col + per_chunk(m_prev)
    m_t = jnp.maximum(m_inter, b_col + run)
    w_inter = jnp.exp(m_inter - m_t)
    e_intra = jnp.exp(d_intra - m_t)
    k_w = k * jnp.exp(a_key - per_chunk(m_next))
    k_w_bf = k_w.astype(BF16)

    probs = [(ci, h) for ci in range(cps) for h in range(HEADS)]

    def cut(t, p):
        return t[p[0] * CHUNK:(p[0] + 1) * CHUNK, p[1] * HEAD_DIM:(p[1] + 1) * HEAD_DIM]

    def paste(pieces):
        return jnp.concatenate([jnp.concatenate([pieces[ci, h] for h in range(HEADS)], axis=1)
                                for ci in range(cps)], axis=0)

    eye = (_iota((CHUNK, CHUNK), 0) == _iota((CHUNK, CHUNK), 1)).astype(BF16)
    s_qk = {p: _dot_nt_bf(cut(q, p), cut(k, p)) * cut(e_intra, p) for p in probs}
    k_w_t = {p: lax.dot_general(eye, cut(k_w_bf, p), NT_DIMS, preferred_element_type=F32) for p in probs}
    intra = {p: _dot_bf(s_qk[p], cut(v, p)) for p in probs}
    inject = {p: _dot_bf(k_w_t[p], cut(v, p)) for p in probs}
    c_mem = [c_ref[h] for h in range(HEADS)]
    n_mem = n_ref[...]
    c_prev, n_prev = {}, []
    for ci in range(cps):
        n_prev.append(n_mem)
        n_mem = n_mem * scale[ci] + jnp.sum(k_w[ci * CHUNK:(ci + 1) * CHUNK], axis=0, keepdims=True)
        for h in range(HEADS):
            c_prev[ci, h] = c_mem[h]
            c_mem[h] = c_mem[h] * scale[ci][:, h * HEAD_DIM:(h + 1) * HEAD_DIM] + inject[ci, h]
    for h in range(HEADS):
        c_ref[h] = c_mem[h]
    n_ref[...] = n_mem
    inter = {p: _dot_bf(cut(q, p), c_prev[p]) for p in probs}
    num = w_inter * paste(inter) + paste(intra)
    qn = w_inter * _head_sums(q * per_chunk(n_prev), ones_bd, 3) + _head_sums(paste(s_qk), ones_bd, 3)
    h_tilde = num / jnp.maximum(jnp.abs(qn), jnp.exp(-m_t))
    mean_sq = _head_sums(h_tilde * h_tilde, ones_bd, 2) * (1.0 / HEAD_DIM)
    out_ref[0] = jax.nn.sigmoid(o_ref_in[0]) * (h_tilde * lax.rsqrt(mean_sq + NORM_EPS) * nw_ref[...])


def _mlstm(qk, v, o_pre, small, conv_w, i_bias, f_bias, norm_w, cps):
    b, s, _ = qk.shape
    rows = cps * CHUNK
    return pl.pallas_call(
        functools.partial(_mlstm_kernel, cps=cps),
        out_shape=jax.ShapeDtypeStruct((b, s, BRANCH), F32),
        grid=(b, s // rows),
        in_specs=[
            pl.BlockSpec((1, rows, 2 * BRANCH), lambda i, c: (i, c, 0)),
            pl.BlockSpec((1, rows, BRANCH), lambda i, c: (i, c, 0)),
            pl.BlockSpec((1, rows, BRANCH), lambda i, c: (i, c, 0)),
            pl.BlockSpec((1, rows, SMALL_W), lambda i, c: (i, c, 0)),
            pl.BlockSpec((CONV_WIDTH, 2 * BRANCH), lambda i, c: (0, 0)),
            pl.BlockSpec((1, HEADS), lambda i, c: (0, 0)),
            pl.BlockSpec((1, HEADS), lambda i, c: (0, 0)),
            pl.BlockSpec((1, BRANCH), lambda i, c: (0, 0)),
        ],
        out_specs=pl.BlockSpec((1, rows, BRANCH), lambda i, c: (i, c, 0)),
        scratch_shapes=[
            pltpu.VMEM((CONV_HIST + rows, 2 * BRANCH), F32),
            pltpu.VMEM((HEADS, HEAD_DIM, HEAD_DIM), F32),
            pltpu.VMEM((1, BRANCH), F32),
            pltpu.VMEM((1, BRANCH), F32),
        ],
        compiler_params=_cparams("parallel", "arbitrary"),
        name="mlstm",
    )(qk, v, o_pre, small, conv_w, i_bias, f_bias, norm_w)


def _s5_prep_kernel(lre_ref, lim_ref, ldt_ref, btre_ref, btim_ref, cre_ref, cim_ref, wglu_ref,
                    toep_ref, wst_ref, wout_ref, al_ref, glu_ref):
    gn, gp = S5_GROUPS * S5_CH, S5_GROUPS * S5_STATE
    lr = lre_ref[...]
    li = lim_ref[...]
    dt = jnp.exp(ldt_ref[...])
    lam = lr * dt
    theta = li * dt

    def power(e):
        mg = jnp.exp(lam * e)
        return mg * jnp.cos(theta * e), mg * jnp.sin(theta * e)

    def cmul(xr, xi, yr, yi):
        return xr * yr - xi * yi, xr * yi + xi * yr

    a_re, a_im = power(1.0)
    den = lr * lr + li * li
    z_re = ((a_re - 1.0) * lr + a_im * li) / den
    z_im = (a_im * lr - (a_re - 1.0) * li) / den
    same_group = (_iota((gn, gp), 0) // S5_CH) == (_iota((gn, gp), 1) // S5_STATE)

    def spread(t):
        return jnp.where(same_group, jnp.concatenate([t] * S5_GROUPS, axis=1), 0.0)

    bb_re, bb_im = cmul(z_re, z_im, spread(btre_ref[...]), spread(btim_ref[...]))
    c_re, c_im = spread(cre_ref[...]), spread(cim_ref[...])
    for s in range(S5_SUB):
        wr, wi = cmul(*power(float(S5_SUB - 1 - s)), bb_re, bb_im)
        wst_ref[s * gn:(s + 1) * gn, :] = jnp.concatenate([wr, wi], axis=1).astype(BF16)
        wr, wi = cmul(*power(float(s + 1)), c_re, c_im)
        wout_ref[s * gn:(s + 1) * gn, :] = jnp.concatenate([wr, -wi], axis=1).astype(BF16)
    c_cat = jnp.concatenate([c_re, -c_im], axis=1)
    resp = []
    for tau in range(S5_SUB):
        wr, wi = cmul(*power(float(tau)), bb_re, bb_im)
        resp.append(_dot_nt_hi(jnp.concatenate([wr, wi], axis=1), c_cat))
    zero = jnp.zeros((gn, gn), F32)
    w_glu = wglu_ref[...]
    for s in range(S5_SUB):
        toep_ref[s * gn:(s + 1) * gn, :] = jnp.concatenate(
            [resp[t - s] if t >= s else zero for t in range(S5_SUB)], axis=1).astype(BF16)
        glu_ref[s * gn:(s + 1) * gn, :] = jnp.concatenate(
            [w_glu if t == s else zero for t in range(S5_SUB)], axis=1).astype(BF16)
    l_re, l_im = power(float(S5_SUB))
    al_ref[...] = jnp.concatenate(
        [jnp.concatenate([l_re, l_re], axis=1), jnp.concatenate([-l_im, l_im], axis=1),
         jnp.zeros((6, 2 * gp), F32)], axis=0)


def _s5_prep(lam_re, lam_im, log_dt, b_re, b_im, c_re, c_im, w_glu):
    gn, gp = S5_GROUPS * S5_CH, S5_GROUPS * S5_STATE
    flat = S5_SUB * gn
    row = lambda a: a.reshape(1, gp)
    full = lambda shape: pl.BlockSpec(shape, lambda i: (0,) * len(shape))
    return pl.pallas_call(
        _s5_prep_kernel,
        out_shape=[
            jax.ShapeDtypeStruct((flat, flat), BF16),
            jax.ShapeDtypeStruct((flat, 2 * gp), BF16),
            jax.ShapeDtypeStruct((flat, 2 * gp), BF16),
            jax.ShapeDtypeStruct((8, 2 * gp), F32),
            jax.ShapeDtypeStruct((flat, flat), BF16),
        ],
        grid=(1,),
        in_specs=[full((1, gp))] * 3 + [full((gn, S5_STATE))] * 4 + [full((gn, gn))],
        out_specs=[full((flat, flat)), full((flat, 2 * gp)), full((flat, 2 * gp)), full((8, 2 * gp)),
                   full((flat, flat))],
        compiler_params=_cparams("arbitrary"),
        name="s5_prep",
    )(row(lam_re), row(lam_im), row(jnp.broadcast_to(log_dt[:, None], lam_re.shape)),
      jnp.swapaxes(b_re, 1, 2).reshape(gn, S5_STATE), jnp.swapaxes(b_im, 1, 2).reshape(gn, S5_STATE),
      c_re.reshape(gn, S5_STATE), c_im.reshape(gn, S5_STATE), w_glu)


def _s5_main_kernel(x_ref, toep_ref, wst_ref, wout_ref, al_ref, glu_ref, d_ref, b_ref, o_ref,
                    tok_ref, inj_ref, prev_ref, state_ref):
    @pl.when(pl.program_id(0) == 0)
    def _():
        state_ref[...] = jnp.zeros(state_ref.shape, F32)

    nb, toks, ch = x_ref.shape
    tc = toks // S5_SUB
    gp = state_ref.shape[1] // 2
    tiles = lambda t: [t[:, j * LANE:(j + 1) * LANE] for j in range(t.shape[1] // LANE)]
    for j, piece in enumerate(tiles(x_ref[...].reshape(nb * toks, ch))):
        tok_ref[j] = piece
    x = jnp.concatenate([tok_ref[j, pl.ds(s, nb * tc, stride=S5_SUB), :]
                         for s in range(S5_SUB) for j in range(ch // LANE)], axis=1)
    xb = x.astype(BF16)
    for j, piece in enumerate(tiles(_dot(xb, wst_ref[...]))):
        inj_ref[j] = piece
    n_tiles = inj_ref.shape[0]
    a_c = al_ref[0:1, :]
    a_s = al_ref[1:2, :]
    state = state_ref[...]
    for c in range(tc):
        rows = pl.ds(c, nb, stride=tc)
        for j, piece in enumerate(tiles(state)):
            prev_ref[j, rows, :] = piece
        inj = jnp.concatenate([inj_ref[j, rows, :] for j in range(n_tiles)], axis=1)
        state = state * a_c + pltpu.roll(state, gp, axis=1) * a_s + inj
    state_ref[...] = state
    prev = jnp.concatenate([prev_ref[j] for j in range(n_tiles)], axis=1)
    y = _dot(xb, toep_ref[...]) + _dot_nt_bf(prev, wout_ref[...])
    y = jax.nn.gelu(y + d_ref[...] * x)
    y = y * jax.nn.sigmoid(_dot(y.astype(BF16), glu_ref[...]) + b_ref[...])
    for n, piece in enumerate(tiles(y)):
        tok_ref[n % (ch // LANE), pl.ds(n // (ch // LANE), nb * tc, stride=S5_SUB), :] = piece
    o_ref[...] = jnp.concatenate([tok_ref[j] for j in range(ch // LANE)], axis=1).reshape(nb, toks, ch)


def _s5(u, batch, seq, prm):
    gp = S5_GROUPS * S5_STATE
    flat = S5_SUB * u.shape[1]
    rows = seq // S5_SUB
    tc = next(c for c in (32, 16, 8) if rows % c == 0)
    toep, wst, wout, al, glu = _s5_prep(prm["lam_re"], prm["lam_im"], prm["log_dt"], prm["b_re"], prm["b_im"],
                                        prm["c_re"], prm["c_im"], prm["w_glu"])
    ch = u.shape[1]
    tile = pl.BlockSpec((batch, tc * S5_SUB, ch), lambda i: (0, i, 0))
    y = pl.pallas_call(
        _s5_main_kernel,
        out_shape=jax.ShapeDtypeStruct((batch, seq, ch), F32),
        grid=(rows // tc,),
        in_specs=[tile, _resident((flat, flat)), _resident((flat, 2 * gp)), _resident((flat, 2 * gp)),
                  _resident((8, 2 * gp)), _resident((flat, flat)), _resident((1, flat)), _resident((1, flat))],
        out_specs=tile,
        scratch_shapes=[pltpu.VMEM((ch // LANE, batch * tc * S5_SUB, LANE), F32),
                        pltpu.VMEM((2 * gp // LANE, batch * tc, LANE), F32),
                        pltpu.VMEM((2 * gp // LANE, batch * tc, LANE), F32),
                        pltpu.VMEM((batch, 2 * gp), F32)],
        compiler_params=_cparams("arbitrary"),
        name="s5_main",
    )(u.reshape(batch, seq, ch), toep, wst, wout, al, glu,
      jnp.tile(prm["d"], (1, S5_SUB)), jnp.tile(prm["b_glu"], (1, S5_SUB)))
    return y.reshape(batch * seq, ch)


def _moba_prep_kernel(q_ref, k_ref, v_ref, pos_ref, inv_ref, qo_ref, ko_ref, vt_ref, sel_ref, km_ref):
    n = pl.program_id(1)
    nbp = km_ref.shape[1]

    @pl.when(n == 0)
    def _():
        km_ref[...] = jnp.zeros(km_ref.shape, F32)

    half = HEAD_DIM // 2
    ang = inv_ref[...] * pos_ref[0, 0]
    cos_t, sin_t = jnp.cos(ang), jnp.sin(ang)
    pairs_per_slab = LANE // HEAD_DIM
    cos = jnp.concatenate([cos_t, cos_t] * pairs_per_slab, axis=0).T
    sin = jnp.concatenate([-sin_t, sin_t] * pairs_per_slab, axis=0).T
    lane = _iota((MOBA_BLOCK, LANE), 1)
    low_half = (lane % HEAD_DIM) < half

    def rope(x):
        swapped = jnp.where(low_half, pltpu.roll(x, LANE - half, axis=1), pltpu.roll(x, half, axis=1))
        return x * cos + swapped * sin

    blk_row = _iota((nbp, MOBA_BLOCK), 0)
    for slab in range(BRANCH // LANE):
        q = rope(q_ref[0, :, slab * LANE:(slab + 1) * LANE])
        k = rope(k_ref[0, :, slab * LANE:(slab + 1) * LANE])
        qo_ref[0, 0, slab] = (q * (HEAD_DIM ** -0.5)).astype(BF16)
        k_mean = jnp.mean(k, axis=0, keepdims=True)
        for sub in range(pairs_per_slab):
            h = slab * pairs_per_slab + sub
            own = (lane // HEAD_DIM) == sub
            gate = _dot_nt_hi(km_ref[h], q)
            rank = jnp.zeros((nbp, MOBA_BLOCK), jnp.int32)
            for m in range(nbp):
                gm = gate[m:m + 1, :]
                beats = (gm > gate) | ((gm == gate) & (m < blk_row))
                rank = rank + jnp.where(beats & (m < n), 1, 0)
            sel_ref[0, 0, h] = ((blk_row < n) & (rank < MOBA_TOPK)).astype(F32)
            ko_ref[0, 0, h] = jnp.where(own, k, 0.0).astype(BF16)
            km_ref[h, pl.ds(n, 1), :] = jnp.where(own[0:1], k_mean, 0.0)
    vt_ref[0, 0] = v_ref[0].T.astype(BF16)


def _moba_prep(qkv, pos, inv):
    b, s, _ = qkv.shape
    nb = s // MOBA_BLOCK
    nbp = -(-nb // 8) * 8
    slabs = BRANCH // LANE
    blk = lambda j: pl.BlockSpec((1, MOBA_BLOCK, BRANCH), lambda i, n, j=j: (i, n, j))
    return pl.pallas_call(
        _moba_prep_kernel,
        out_shape=[
            jax.ShapeDtypeStruct((b, nb, slabs, MOBA_BLOCK, LANE), BF16),
            jax.ShapeDtypeStruct((b, nb, HEADS, MOBA_BLOCK, LANE), BF16),
            jax.ShapeDtypeStruct((b, nb, BRANCH, MOBA_BLOCK), BF16),
            jax.ShapeDtypeStruct((b, nb, HEADS, nbp, MOBA_BLOCK), F32),
        ],
        grid=(b, nb),
        in_specs=[blk(0), blk(1), blk(2),
                  pl.BlockSpec((1, 1, 1, MOBA_BLOCK), lambda i, n: (i, n, 0, 0)),
                  pl.BlockSpec((HEAD_DIM // 2, 1), lambda i, n: (0, 0))],
        out_specs=[pl.BlockSpec((1, 1, slabs, MOBA_BLOCK, LANE), lambda i, n: (i, n, 0, 0, 0)),
                   pl.BlockSpec((1, 1, HEADS, MOBA_BLOCK, LANE), lambda i, n: (i, n, 0, 0, 0)),
                   pl.BlockSpec((1, 1, BRANCH, MOBA_BLOCK), lambda i, n: (i, n, 0, 0)),
                   pl.BlockSpec((1, 1, HEADS, nbp, MOBA_BLOCK), lambda i, n: (i, n, 0, 0, 0))],
        scratch_shapes=[pltpu.VMEM((HEADS, nbp, LANE), F32)],
        compiler_params=_cparams("parallel", "arbitrary"),
        name="moba_prep",
    )(qkv, qkv, qkv, pos, inv)


def _moba_attn_kernel(q_ref, k_ref, vt_ref, sel_ref, o_ref):
    i = pl.program_id(1)
    blk = MOBA_BLOCK
    nbp = sel_ref.shape[3]
    key_i = _iota((blk, blk), 0)
    qry_i = _iota((blk, blk), 1)
    sel_row = _iota((nbp, blk), 0)

    def update(n, carries, masks):
        scores = [lax.dot_general(k_ref[0, n, h], q_ref[0, 0, h * HEAD_DIM // LANE], NT_DIMS,
                                  preferred_element_type=F32) for h in range(HEADS)]
        stats = []
        for h in range(HEADS):
            m_run, l_run, _ = carries[h]
            s = jnp.where(masks[h], scores[h], -jnp.inf)
            m_new = jnp.maximum(m_run, jnp.max(s, axis=0, keepdims=True))
            m_safe = jnp.where(m_new == -jnp.inf, 0.0, m_new)
            alpha = jnp.exp(m_run - m_safe)
            p = jnp.exp(s - m_safe)
            stats.append((m_new, alpha * l_run + jnp.sum(p, axis=0, keepdims=True), alpha, p.astype(BF16)))
        out = []
        for h in range(HEADS):
            m_new, l_new, alpha, p = stats[h]
            vt = vt_ref[0, n, h * HEAD_DIM:(h + 1) * HEAD_DIM, :]
            out.append((m_new, l_new, alpha * carries[h][2] + _dot(vt, p)))
        return tuple(out)

    def body(n, carries):
        masks = [jnp.max(jnp.where(sel_row == n, sel_ref[0, 0, h], 0.0), axis=0, keepdims=True) > 0.0
                 for h in range(HEADS)]
        return update(n, carries, masks)

    init = (jnp.full((1, blk), -jnp.inf, F32), jnp.zeros((1, blk), F32), jnp.zeros((HEAD_DIM, blk), F32))
    carries = lax.fori_loop(0, i, body, (init,) * HEADS)
    carries = update(i, carries, [key_i <= qry_i] * HEADS)
    o_ref[0] = jnp.concatenate([acc / l_run for _, l_run, acc in carries], axis=0).T


def _moba_attn(q, k, vt, sel):
    b, nb = q.shape[:2]
    nbp = sel.shape[3]
    return pl.pallas_call(
        _moba_attn_kernel,
        out_shape=jax.ShapeDtypeStruct((b, nb * MOBA_BLOCK, BRANCH), F32),
        grid=(b, nb),
        in_specs=[
            pl.BlockSpec((1, 1, BRANCH // LANE, MOBA_BLOCK, LANE), lambda i, n: (i, n, 0, 0, 0)),
            pl.BlockSpec((1, nb, HEADS, MOBA_BLOCK, LANE), lambda i, n: (i, 0, 0, 0, 0)),
            pl.BlockSpec((1, nb, BRANCH, MOBA_BLOCK), lambda i, n: (i, 0, 0, 0)),
            pl.BlockSpec((1, 1, HEADS, nbp, MOBA_BLOCK), lambda i, n: (i, n, 0, 0, 0)),
        ],
        out_specs=pl.BlockSpec((1, MOBA_BLOCK, BRANCH), lambda i, n: (i, n, 0)),
        compiler_params=_cparams("parallel", "arbitrary"),
        name="moba_attn",
    )(q, k, vt, sel)


def _moba(qkv, positions):
    b, s, _ = qkv.shape
    inv = ROPE_THETA ** (-jnp.arange(0, HEAD_DIM, 2, dtype=F32) / HEAD_DIM)
    pos = positions.astype(F32).reshape(b, s // MOBA_BLOCK, 1, MOBA_BLOCK)
    return _moba_attn(*_moba_prep(qkv, pos, inv[:, None]))


def _permute_w_in(w_in):
    cuts = [0]
    for width in (3 * BRANCH, BRANCH, HEADS, HEADS, 2 * BRANCH, BRANCH, BRANCH, HEADS, HEADS, BRANCH, 3 * BRANCH):
        cuts.append(cuts[-1] + width)
    seg = [w_in[:, cuts[n]:cuts[n + 1]] for n in range(len(cuts) - 1)]
    gdn_qkv, gdn_z, gdn_b, gdn_a, ml_qk, ml_v, ml_o, ml_i, ml_f, s5_u, moba_qkv = seg
    pad = jnp.zeros((w_in.shape[0], SMALL_W - 4 * HEADS), w_in.dtype)
    return jnp.concatenate([gdn_qkv, gdn_z, ml_qk, ml_v, ml_o, s5_u, moba_qkv, gdn_b, gdn_a, ml_i, ml_f, pad], axis=1)


def _token_tile(t, largest=512):
    for tm in (1024, 512, 256, 128, 64, 32, 16, 8):
        if tm > largest:
            continue
        if t % tm == 0:
            return tm
    raise ValueError(f"token count {t} is not a multiple of 8")


def _chunks_per_step(seq):
    n_chunks = seq // CHUNK
    for cps in (8, 4, 2, 1):
        if n_chunks % cps == 0:
            return cps


def _ff_slice(f):
    for sw in (2 * LANE, LANE):
        if f % sw == 0:
            return sw
    raise ValueError(f"ffn width {f} is not a multiple of {LANE}")


def kernel(x, p, positions, ffn1_norm, ffn1_w_gu, ffn1_w_down, mix_norm, w_in, gdn_conv, gdn_a_log, gdn_dt_bias, gdn_norm, mlstm_conv, mlstm_i_bias, mlstm_f_bias, mlstm_norm, s5_lambda_re, s5_lambda_im, s5_b_re, s5_b_im, s5_c_re, s5_c_im, s5_d, s5_log_dt, s5_w_glu, s5_b_glu, w_gate, w_branch, w_out, ffn2_norm, ffn2_w_gu, ffn2_w_down, ple_norm, ple_w_proj, ple_w_gate, final_norm):
    batch, seq, d = x.shape
    depth = p.shape[0]
    t = batch * seq
    tm = _token_tile(t)
    tm_ffn = _token_tile(t, largest=1024)
    tf = _ff_slice(ffn1_w_down.shape[1])
    cps = _chunks_per_step(seq)
    bf = lambda a: a.astype(BF16)
    row = lambda a: a.reshape(1, -1)
    seq3 = lambda a: a.reshape(batch, seq, a.shape[-1])

    h = x.reshape(t, d)
    for i in range(depth):
        h = _ffn(h, row(ffn1_norm[i]), bf(ffn1_w_gu[i]), bf(ffn1_w_down[i]), tm_ffn, tf)

        gdn_qkv, gdn_z, ml_qk, ml_v, ml_o, s5_u, moba_qkv, small = _inproj(
            h, row(mix_norm[i]), bf(_permute_w_in(w_in[i])), tm)
        y_gdn = _gdn(seq3(gdn_qkv), seq3(gdn_z), seq3(small), gdn_conv[i], row(gdn_a_log[i]),
                     row(gdn_dt_bias[i]), row(gdn_norm[i]), cps)
        y_mlstm = _mlstm(seq3(ml_qk), seq3(ml_v), seq3(ml_o), seq3(small), mlstm_conv[i],
                         row(mlstm_i_bias[i]), row(mlstm_f_bias[i]), row(mlstm_norm[i]), cps)
        s5_prm = dict(lam_re=s5_lambda_re[i], lam_im=s5_lambda_im[i], log_dt=s5_log_dt[i],
                      b_re=s5_b_re[i], b_im=s5_b_im[i], c_re=s5_c_re[i], c_im=s5_c_im[i],
                      d=row(s5_d[i]), w_glu=s5_w_glu[i], b_glu=row(s5_b_glu[i]))
        y_s5 = _s5(s5_u, batch, seq, s5_prm)
        y_moba = _moba(seq3(moba_qkv), positions)
        ys = (y_gdn.reshape(t, BRANCH), y_mlstm.reshape(t, BRANCH), y_s5, y_moba.reshape(t, BRANCH))
        h = _merge(h, row(mix_norm[i]), ys, bf(w_gate[i]), bf(w_branch[i]), bf(w_out[i]), tm)

        ple = (row(ple_norm[i]), p[i].reshape(t, -1), bf(ple_w_proj[i]), bf(ple_w_gate[i]), row(final_norm))
        h = _ffn(h, row(ffn2_norm[i]), bf(ffn2_w_gu[i]), bf(ffn2_w_down[i]), tm_ffn, tf, ple, i == depth - 1)
    return h.reshape(batch, seq, d)
```

```python
import functools
import math

import jax
import jax.numpy as jnp
from jax import lax
from jax.experimental import pallas as pl
from jax.experimental.pallas import tpu as pltpu

F32 = jnp.float32
BF16 = jnp.bfloat16
HIGHEST = lax.Precision.HIGHEST

NORM_EPS = 1e-6
HEADS = 4
HEAD_DIM = 64
BRANCH = HEADS * HEAD_DIM
CHUNK = 64
CONV_WIDTH = 4
CONV_HIST = 8
S5_GROUPS = 16
S5_CH = 16
S5_STATE = 64
S5_SUB = 4
MOBA_BLOCK = 256
MOBA_TOPK = 3
VT_ROWS = HEAD_DIM + 16
ROPE_THETA = 10000.0
LANE = 128
SMALL_W = LANE
V7X_VMEM_LIMIT = 56 * 1024 * 1024

NT_DIMS = (((1,), (1,)), ((), ()))


def _cparams(*sem):
    return pltpu.CompilerParams(dimension_semantics=sem, vmem_limit_bytes=V7X_VMEM_LIMIT)


def _rms(x, w):
    return x * lax.rsqrt(jnp.mean(x * x, axis=-1, keepdims=True) + NORM_EPS) * w


def _silu(x):
    return x * jax.nn.sigmoid(x)


def _softplus(x):
    return jnp.maximum(x, 0.0) + jnp.log1p(jnp.exp(-jnp.abs(x)))


def _dot(a, b):
    return jnp.dot(a, b, preferred_element_type=F32)


def _dot_bf(a, b):
    return jnp.dot(a.astype(BF16), b.astype(BF16), preferred_element_type=F32)


def _dot_nt_bf(a, b):
    return lax.dot_general(a.astype(BF16), b.astype(BF16), NT_DIMS, preferred_element_type=F32)


def _dot_nt_hi(a, b):
    return lax.dot_general(a, b, NT_DIMS, precision=HIGHEST, preferred_element_type=F32)


def _split_bf(a, parts):
    out = []
    for _ in range(parts):
        piece = a.astype(BF16)
        out.append(piece)
        a = a - piece.astype(F32)
    return out


def _dot_x3(a, b):
    a_hi, a_lo = _split_bf(a, 2)
    b_hi, b_lo = _split_bf(b, 2)
    return _dot(jnp.concatenate([a_hi, a_hi, a_lo], axis=1), jnp.concatenate([b_hi, b_lo, b_hi], axis=0))


def _dot_exact01(a01_x3, b):
    return _dot(a01_x3, jnp.concatenate(_split_bf(b, 3), axis=0))


def _iota(shape, dim):
    return lax.broadcasted_iota(jnp.int32, shape, dim)


def _head_block_ones():
    r = _iota((BRANCH, BRANCH), 0) // HEAD_DIM
    c = _iota((BRANCH, BRANCH), 1) // HEAD_DIM
    return (r == c).astype(BF16)


def _head_sums(x, ones_bd, parts):
    pieces = _split_bf(x, parts)
    out = _dot(pieces[0], ones_bd)
    for piece in pieces[1:]:
        out = out + _dot(piece, ones_bd)
    return out


def _head_bcast(cols):
    rows = cols.shape[0]
    head = _iota((rows, BRANCH), 1) // HEAD_DIM
    out = jnp.broadcast_to(cols[:, 0:1], (rows, BRANCH))
    for h in range(1, HEADS):
        out = jnp.where(head == h, jnp.broadcast_to(cols[:, h:h + 1], (rows, BRANCH)), out)
    return out


def _causal_conv_silu(x_ref, w_ref, pad_ref, first):
    rows = x_ref.shape[1]

    @pl.when(first)
    def _():
        pad_ref[0:CONV_HIST, :] = jnp.zeros((CONV_HIST, pad_ref.shape[1]), F32)

    pad_ref[CONV_HIST:CONV_HIST + rows, :] = x_ref[0]
    w = w_ref[...]
    acc = w[CONV_WIDTH - 1:CONV_WIDTH, :] * pad_ref[CONV_HIST:CONV_HIST + rows, :]
    for back in range(1, CONV_WIDTH):
        tap = CONV_WIDTH - 1 - back
        acc = acc + w[tap:tap + 1, :] * pad_ref[CONV_HIST - back:CONV_HIST - back + rows, :]
    pad_ref[0:CONV_HIST, :] = pad_ref[rows:rows + CONV_HIST, :]
    return _silu(acc)


def _tri_consts():
    r = _iota((3 * CHUNK, 3 * CHUNK), 0)
    k = _iota((3 * CHUNK, 3 * CHUNK), 1) % CHUNK
    i = r % CHUNK
    part = r // CHUNK
    return (((part == 0) & (k <= i)) | ((part == 1) & (k > i)) | (part == 2)).astype(BF16)


def _resident(shape):
    return pl.BlockSpec(shape, lambda *_: (0,) * len(shape), pipeline_mode=pl.Buffered(1))


def _ffn_kernel(x_ref, nw_ref, wgu_ref, wd_ref, *rest, sw, ple, final):
    o_ref = rest[-1]
    x = x_ref[...]
    xn = _rms(x, nw_ref[...]).astype(BF16)
    f = wd_ref.shape[0]

    def gate_up(j):
        return (_dot(xn, wgu_ref[:, j * sw:(j + 1) * sw]), _dot(xn, wgu_ref[:, f + j * sw:f + (j + 1) * sw]))

    ahead = gate_up(0)
    acc = None
    for j in range(f // sw):
        gate, up = ahead
        if (j + 1) * sw < f:
            ahead = gate_up(j + 1)
        part = _dot((_silu(gate) * up).astype(BF16), wd_ref[j * sw:(j + 1) * sw, :])
        acc = part if acc is None else acc + part
    out = x + 0.5 * acc
    if ple:
        pn_ref, p_ref, wp_ref, wg_ref, fw_ref = rest[:-1]
        gate = jax.nn.sigmoid(_dot(_rms(out, pn_ref[...]).astype(BF16), wg_ref[...]))
        out = out + _dot(p_ref[...].astype(BF16), wp_ref[...]) * gate
        if final:
            out = _rms(out, fw_ref[...])
    o_ref[...] = out


def _ffn(h, norm_w, w_gu, w_down, tm, sw, ple=None, final=False):
    t, d = h.shape
    f = w_down.shape[0]
    operands = [h, norm_w, w_gu, w_down]
    in_specs = [pl.BlockSpec((tm, d), lambda i: (i, 0)), _resident((1, d)), _resident((d, 2 * f)), _resident((f, d))]
    if ple is not None:
        pd = ple[1].shape[1]
        operands += list(ple)
        in_specs += [_resident((1, d)), pl.BlockSpec((tm, pd), lambda i: (i, 0)), _resident((pd, d)),
                     _resident((d, d)), _resident((1, d))]
    return pl.pallas_call(
        functools.partial(_ffn_kernel, sw=sw, ple=ple is not None, final=final),
        out_shape=jax.ShapeDtypeStruct((t, d), F32),
        grid=(t // tm,),
        in_specs=in_specs,
        out_specs=pl.BlockSpec((tm, d), lambda i: (i, 0)),
        compiler_params=_cparams("parallel"),
        name="ffn",
    )(*operands)


IN_SEGMENTS = (3 * BRANCH, BRANCH, 2 * BRANCH, BRANCH, BRANCH, BRANCH, 3 * BRANCH, SMALL_W)


def _inproj_kernel(x_ref, nw_ref, w_ref, *o_refs):
    xn = _rms(x_ref[...], nw_ref[...]).astype(BF16)
    start = 0
    for o_ref, width in zip(o_refs, IN_SEGMENTS):
        o_ref[...] = _dot(xn, w_ref[:, start:start + width])
        start += width


def _inproj(h, norm_w, w_perm, tm):
    t, d = h.shape
    n = w_perm.shape[1]
    return pl.pallas_call(
        _inproj_kernel,
        out_shape=[jax.ShapeDtypeStruct((t, w), F32) for w in IN_SEGMENTS],
        grid=(t // tm,),
        in_specs=[
            pl.BlockSpec((tm, d), lambda i: (i, 0)),
            pl.BlockSpec((1, d), lambda i: (0, 0)),
            pl.BlockSpec((d, n), lambda i: (0, 0)),
        ],
        out_specs=[pl.BlockSpec((tm, w), lambda i: (i, 0)) for w in IN_SEGMENTS],
        compiler_params=_cparams("parallel"),
        name="inproj",
    )(h, norm_w, w_perm)


def _merge_kernel(h_ref, nw_ref, y0_ref, y1_ref, y2_ref, y3_ref, wg_ref, wb_ref, wo_ref, o_ref):
    h = h_ref[...]
    u = _rms(h, nw_ref[...]).astype(BF16)
    merged = None
    for b, y_ref in enumerate((y0_ref, y1_ref, y2_ref, y3_ref)):
        term = jax.nn.sigmoid(_dot(u, wg_ref[b])) * _dot(y_ref[...].astype(BF16), wb_ref[b])
        merged = term if merged is None else merged + term
    o_ref[...] = h + _dot(merged.astype(BF16), wo_ref[...])


def _merge(h, norm_w, ys, w_gate, w_branch, w_out, tm):
    t, d = h.shape
    nb, bw, _ = w_branch.shape
    return pl.pallas_call(
        _merge_kernel,
        out_shape=jax.ShapeDtypeStruct((t, d), F32),
        grid=(t // tm,),
        in_specs=[
            pl.BlockSpec((tm, d), lambda i: (i, 0)),
            pl.BlockSpec((1, d), lambda i: (0, 0)),
        ] + [pl.BlockSpec((tm, bw), lambda i: (i, 0)) for _ in range(nb)] + [
            pl.BlockSpec((nb, d, d), lambda i: (0, 0, 0)),
            pl.BlockSpec((nb, bw, d), lambda i: (0, 0, 0)),
            pl.BlockSpec((d, d), lambda i: (0, 0)),
        ],
        out_specs=pl.BlockSpec((tm, d), lambda i: (i, 0)),
        compiler_params=_cparams("parallel"),
        name="merge",
    )(h, norm_w, *ys, w_gate, w_branch, w_out)


def _gdn_kernel(qkv_ref, z_ref, sm_ref, cw_ref, alog_ref, dtb_ref, nw_ref, o_ref, pad_ref, state_ref, *, cps):
    step = pl.program_id(1)

    @pl.when(step == 0)
    def _():
        state_ref[...] = jnp.zeros(state_ref.shape, F32)

    rows = cps * CHUNK
    x = _causal_conv_silu(qkv_ref, cw_ref, pad_ref, step == 0)
    sm = sm_ref[0]
    beta = _head_bcast(jax.nn.sigmoid(sm[:, 0:HEADS]))
    g = _head_bcast(-jnp.exp(alog_ref[...]) * _softplus(sm[:, HEADS:2 * HEADS] + dtb_ref[...]))
    ones_bd = _head_block_ones()
    q, k, v = x[:, 0:BRANCH], x[:, BRANCH:2 * BRANCH], x[:, 2 * BRANCH:3 * BRANCH]
    q = q * lax.rsqrt(_head_sums(q * q, ones_bd, 2) + NORM_EPS) * (HEAD_DIM ** -0.5)
    k = k * lax.rsqrt(_head_sums(k * k, ones_bd, 2) + NORM_EPS)

    row_c = _iota((CHUNK, BRANCH), 0)
    col_c = _iota((CHUNK, BRANCH), 1) % HEAD_DIM
    tri = _tri_consts()
    sums = []
    for ci in range(cps):
        gb = g[ci * CHUNK:(ci + 1) * CHUNK]
        sums.append(_dot_exact01(tri, jnp.concatenate([gb, jnp.where(row_c > col_c, gb, 0.0)], axis=1)))
    stack = lambda r0, c0: jnp.concatenate([s[r0:r0 + CHUNK, c0:c0 + BRANCH] for s in sums], axis=0)
    row = _iota((rows, BRANCH), 0) % CHUNK
    col = _iota((rows, BRANCH), 1) % HEAD_DIM
    e_gc = jnp.exp(stack(0, 0))
    e_rest = jnp.exp(stack(CHUNK, 0))
    e_total = jnp.exp(stack(2 * CHUNK, 0))
    decay = jnp.where(row >= col, jnp.exp(stack(0, BRANCH)), 0.0)
    k_beta = k * beta
    q_dec = q * e_gc
    k_dec = (k * e_rest).astype(BF16)
    rhs_u = v * beta
    rhs_w = k_beta * e_gc

    probs = [(ci, h) for ci in range(cps) for h in range(HEADS)]

    def cut(t, p):
        return t[p[0] * CHUNK:(p[0] + 1) * CHUNK, p[1] * HEAD_DIM:(p[1] + 1) * HEAD_DIM]

    eye = (_iota((CHUNK, CHUNK), 0) == _iota((CHUNK, CHUNK), 1)).astype(BF16)
    gram = {p: _dot_nt_bf(jnp.concatenate([cut(k_beta, p), cut(q, p)], axis=0), cut(k, p)) for p in probs}
    k_dec_t = {p: lax.dot_general(eye, cut(k_dec, p), NT_DIMS, preferred_element_type=F32) for p in probs}
    power, sol, qk = {}, {}, {}
    for p in probs:
        dec = cut(decay, p)
        power[p] = jnp.where(cut(row > col, p), gram[p][0:CHUNK] * dec, 0.0)
        qk[p] = gram[p][CHUNK:2 * CHUNK] * dec
        sol[p] = jnp.concatenate([cut(rhs_u, p), cut(rhs_w, p)], axis=1)
    n_factors = int(math.log2(CHUNK))
    for j in range(n_factors):
        last = j == n_factors - 1
        prods = {p: _dot_x3(power[p], sol[p] if last else jnp.concatenate([sol[p], power[p]], axis=1))
                 for p in probs}
        for p in probs:
            term = prods[p][:, 0:2 * HEAD_DIM]
            sol[p] = sol[p] - term if j == 0 else sol[p] + term
            if not last:
                power[p] = prods[p][:, 2 * HEAD_DIM:3 * HEAD_DIM]
    q_eff, out_loc, inject, mix = {}, {}, {}, {}
    both = {p: _dot_bf(jnp.concatenate([qk[p], k_dec_t[p]], axis=0), sol[p]) for p in probs}
    for p in probs:
        out_loc[p] = both[p][0:CHUNK, 0:HEAD_DIM]
        q_eff[p] = cut(q_dec, p) - both[p][0:CHUNK, HEAD_DIM:2 * HEAD_DIM]
        inject[p] = both[p][CHUNK:2 * CHUNK, 0:HEAD_DIM]
        mix[p] = both[p][CHUNK:2 * CHUNK, HEAD_DIM:2 * HEAD_DIM]
    state = [state_ref[h] for h in range(HEADS)]
    entering, from_state = {}, {}
    for ci in range(cps):
        mixed = [_dot_bf(mix[ci, h], state[h]) for h in range(HEADS)]
        if ci:
            from_state.update({(ci - 1, h): _dot_bf(q_eff[ci - 1, h], entering[ci - 1, h]) for h in range(HEADS)})
        for h in range(HEADS):
            entering[ci, h] = state[h]
            state[h] = state[h] * cut(e_total, (ci, h)) - mixed[h] + inject[ci, h]
    from_state.update({(cps - 1, h): _dot_bf(q_eff[cps - 1, h], entering[cps - 1, h]) for h in range(HEADS)})
    for h in range(HEADS):
        state_ref[h] = state[h]
    out = jnp.concatenate([jnp.concatenate([from_state[ci, h] + out_loc[ci, h] for h in range(HEADS)], axis=1)
                           for ci in range(cps)], axis=0)
    mean_sq = _head_sums(out * out, ones_bd, 2) * (1.0 / HEAD_DIM)
    nw = jnp.concatenate([nw_ref[...]] * HEADS, axis=1)
    o_ref[0] = out * lax.rsqrt(mean_sq + NORM_EPS) * nw * _silu(z_ref[0])


def _gdn(qkv, z, small, conv_w, a_log, dt_bias, norm_w, cps):
    b, s, _ = qkv.shape
    rows = cps * CHUNK
    return pl.pallas_call(
        functools.partial(_gdn_kernel, cps=cps),
        out_shape=jax.ShapeDtypeStruct((b, s, BRANCH), F32),
        grid=(b, s // rows),
        in_specs=[
            pl.BlockSpec((1, rows, 3 * BRANCH), lambda i, c: (i, c, 0)),
            pl.BlockSpec((1, rows, BRANCH), lambda i, c: (i, c, 0)),
            pl.BlockSpec((1, rows, SMALL_W), lambda i, c: (i, c, 0)),
            pl.BlockSpec((CONV_WIDTH, 3 * BRANCH), lambda i, c: (0, 0)),
            pl.BlockSpec((1, HEADS), lambda i, c: (0, 0)),
            pl.BlockSpec((1, HEADS), lambda i, c: (0, 0)),
            pl.BlockSpec((1, HEAD_DIM), lambda i, c: (0, 0)),
        ],
        out_specs=pl.BlockSpec((1, rows, BRANCH), lambda i, c: (i, c, 0)),
        scratch_shapes=[
            pltpu.VMEM((CONV_HIST + rows, 3 * BRANCH), F32),
            pltpu.VMEM((HEADS, HEAD_DIM, HEAD_DIM), F32),
        ],
        compiler_params=_cparams("parallel", "arbitrary"),
        name="gdn",
    )(qkv, z, small, conv_w, a_log, dt_bias, norm_w)


def _mlstm_kernel(qk_ref, v_ref, o_ref_in, sm_ref, cw_ref, ib_ref, fb_ref, nw_ref, out_ref,
                  pad_ref, c_ref, n_ref, m_ref, *, cps):
    step = pl.program_id(1)

    @pl.when(step == 0)
    def _():
        c_ref[...] = jnp.zeros(c_ref.shape, F32)
        n_ref[...] = jnp.zeros(n_ref.shape, F32)
        m_ref[...] = jnp.zeros(m_ref.shape, F32)

    rows = cps * CHUNK
    x = _causal_conv_silu(qk_ref, cw_ref, pad_ref, step == 0)
    q = x[:, 0:BRANCH]
    k = x[:, BRANCH:2 * BRANCH] * (HEAD_DIM ** -0.5)
    v = v_ref[0]
    sm = sm_ref[0]
    log_i = _head_bcast(sm[:, 2 * HEADS:3 * HEADS] + ib_ref[...])
    log_f = _head_bcast(-_softplus(-(sm[:, 3 * HEADS:4 * HEADS] + fb_ref[...])))
    ones_bd = _head_block_ones()

    row_c = _iota((CHUNK, BRANCH), 0)
    col_c = _iota((CHUNK, BRANCH), 1) % HEAD_DIM
    tri = _tri_consts()
    sums = []
    for ci in range(cps):
        lf = log_f[ci * CHUNK:(ci + 1) * CHUNK]
        li = log_i[ci * CHUNK:(ci + 1) * CHUNK]
        sums.append(_dot_exact01(tri, jnp.concatenate(
            [lf, jnp.where(row_c > col_c, lf, 0.0), jnp.where(row_c == col_c, li, 0.0)], axis=1)))
    stack = lambda r0, c0: jnp.concatenate([s[r0:r0 + CHUNK, c0:c0 + BRANCH] for s in sums], axis=0)
    row = _iota((rows, BRANCH), 0) % CHUNK
    col = _iota((rows, BRANCH), 1) % HEAD_DIM
    b_col = stack(0, 0)
    b_last = stack(2 * CHUNK, 0)
    a_key = stack(CHUNK, 0) + log_i
    d_intra = jnp.where(row >= col, stack(0, BRANCH) + stack(2 * CHUNK, 2 * BRANCH), -jnp.inf)
    run = log_i - b_col
    shift = 1
    while shift < CHUNK:
        run = jnp.where(row >= shift, jnp.maximum(run, pltpu.roll(run, shift, axis=0)), run)
        shift *= 2
    m_mem = m_ref[...]
    m_prev, m_next, scale = [], [], []
    for ci in range(cps):
        last = ci * CHUNK + CHUNK - 1
        g_c = b_last[last:last + 1]
        m_new = jnp.maximum(g_c + m_mem, g_c + run[last:last + 1])
        m_prev.append(m_mem)
        m_next.append(m_new)
        scale.append(jnp.exp(g_c + m_mem - m_new))
        m_mem = m_new
    m_ref[...] = m_mem
    per_chunk = lambda rows_: jnp.concatenate([jnp.broadcast_to(r, (CHUNK, BRANCH)) for r in rows_], axis=0)
    m_inter = b_col + per_chunk(m_prev)
    m_t = jnp.maximum(m_inter, b_col + run)
    w_inter = jnp.exp(m_inter - m_t)
    e_intra = jnp.exp(d_intra - m_t)
    k_w = k * jnp.exp(a_key - per_chunk(m_next))
    k_w_bf = k_w.astype(BF16)

    probs = [(ci, h) for ci in range(cps) for h in range(HEADS)]

    def cut(t, p):
        return t[p[0] * CHUNK:(p[0] + 1) * CHUNK, p[1] * HEAD_DIM:(p[1] + 1) * HEAD_DIM]

    def paste(pieces):
        return jnp.concatenate([jnp.concatenate([pieces[ci, h] for h in range(HEADS)], axis=1)
                                for ci in range(cps)], axis=0)

    eye = (_iota((CHUNK, CHUNK), 0) == _iota((CHUNK, CHUNK), 1)).astype(BF16)
    s_qk = {p: _dot_nt_bf(cut(q, p), cut(k, p)) * cut(e_intra, p) for p in probs}
    k_w_t = {p: lax.dot_general(eye, cut(k_w_bf, p), NT_DIMS, preferred_element_type=F32) for p in probs}
    intra = {p: _dot_bf(s_qk[p], cut(v, p)) for p in probs}
    inject = {p: _dot_bf(k_w_t[p], cut(v, p)) for p in probs}
    c_mem = [c_ref[h] for h in range(HEADS)]
    n_mem = n_ref[...]
    c_prev, n_prev = {}, []
    for ci in range(cps):
        n_prev.append(n_mem)
        n_mem = n_mem * scale[ci] + jnp.sum(k_w[ci * CHUNK:(ci + 1) * CHUNK], axis=0, keepdims=True)
        for h in range(HEADS):
            c_prev[ci, h] = c_mem[h]
            c_mem[h] = c_mem[h] * scale[ci][:, h * HEAD_DIM:(h + 1) * HEAD_DIM] + inject[ci, h]
    for h in range(HEADS):
        c_ref[h] = c_mem[h]
    n_ref[...] = n_mem
    inter = {p: _dot_bf(cut(q, p), c_prev[p]) for p in probs}
    num = w_inter * paste(inter) + paste(intra)
    qn = w_inter * _head_sums(q * per_chunk(n_prev), ones_bd, 3) + _head_sums(paste(s_qk), ones_bd, 3)
    h_tilde = num / jnp.maximum(jnp.abs(qn), jnp.exp(-m_t))
    mean_sq = _head_sums(h_tilde * h_tilde, ones_bd, 2) * (1.0 / HEAD_DIM)
    out_ref[0] = jax.nn.sigmoid(o_ref_in[0]) * (h_tilde * lax.rsqrt(mean_sq + NORM_EPS) * nw_ref[...])


def _mlstm(qk, v, o_pre, small, conv_w, i_bias, f_bias, norm_w, cps):
    b, s, _ = qk.shape
    rows = cps * CHUNK
    return pl.pallas_call(
        functools.partial(_mlstm_kernel, cps=cps),
        out_shape=jax.ShapeDtypeStruct((b, s, BRANCH), F32),
        grid=(b, s // rows),
        in_specs=[
            pl.BlockSpec((1, rows, 2 * BRANCH), lambda i, c: (i, c, 0)),
            pl.BlockSpec((1, rows, BRANCH), lambda i, c: (i, c, 0)),
            pl.BlockSpec((1, rows, BRANCH), lambda i, c: (i, c, 0)),
            pl.BlockSpec((1, rows, SMALL_W), lambda i, c: (i, c, 0)),
            pl.BlockSpec((CONV_WIDTH, 2 * BRANCH), lambda i, c: (0, 0)),
            pl.BlockSpec((1, HEADS), lambda i, c: (0, 0)),
            pl.BlockSpec((1, HEADS), lambda i, c: (0, 0)),
            pl.BlockSpec((1, BRANCH), lambda i, c: (0, 0)),
        ],
        out_specs=pl.BlockSpec((1, rows, BRANCH), lambda i, c: (i, c, 0)),
        scratch_shapes=[
            pltpu.VMEM((CONV_HIST + rows, 2 * BRANCH), F32),
            pltpu.VMEM((HEADS, HEAD_DIM, HEAD_DIM), F32),
            pltpu.VMEM((1, BRANCH), F32),
            pltpu.VMEM((1, BRANCH), F32),
        ],
        compiler_params=_cparams("parallel", "arbitrary"),
        name="mlstm",
    )(qk, v, o_pre, small, conv_w, i_bias, f_bias, norm_w)


def _s5_prep_kernel(lre_ref, lim_ref, ldt_ref, btre_ref, btim_ref, cre_ref, cim_ref, wglu_ref,
                    toep_ref, wst_ref, wout_ref, al_ref, glu_ref):
    gn, gp = S5_GROUPS * S5_CH, S5_GROUPS * S5_STATE
    lr = lre_ref[...]
    li = lim_ref[...]
    dt = jnp.exp(ldt_ref[...])
    lam = lr * dt
    theta = li * dt

    def power(e):
        mg = jnp.exp(lam * e)
        return mg * jnp.cos(theta * e), mg * jnp.sin(theta * e)

    def cmul(xr, xi, yr, yi):
        return xr * yr - xi * yi, xr * yi + xi * yr

    a_re, a_im = power(1.0)
    den = lr * lr + li * li
    z_re = ((a_re - 1.0) * lr + a_im * li) / den
    z_im = (a_im * lr - (a_re - 1.0) * li) / den
    same_group = (_iota((gn, gp), 0) // S5_CH) == (_iota((gn, gp), 1) // S5_STATE)

    def spread(t):
        return jnp.where(same_group, jnp.concatenate([t] * S5_GROUPS, axis=1), 0.0)

    bb_re, bb_im = cmul(z_re, z_im, spread(btre_ref[...]), spread(btim_ref[...]))
    c_re, c_im = spread(cre_ref[...]), spread(cim_ref[...])
    for s in range(S5_SUB):
        wr, wi = cmul(*power(float(S5_SUB - 1 - s)), bb_re, bb_im)
        wst_ref[s * gn:(s + 1) * gn, :] = jnp.concatenate([wr, wi], axis=1).astype(BF16)
        wr, wi = cmul(*power(float(s + 1)), c_re, c_im)
        wout_ref[s * gn:(s + 1) * gn, :] = jnp.concatenate([wr, -wi], axis=1).astype(BF16)
    c_cat = jnp.concatenate([c_re, -c_im], axis=1)
    resp = []
    for tau in range(S5_SUB):
        wr, wi = cmul(*power(float(tau)), bb_re, bb_im)
        resp.append(_dot_nt_hi(jnp.concatenate([wr, wi], axis=1), c_cat))
    zero = jnp.zeros((gn, gn), F32)
    w_glu = wglu_ref[...]
    for s in range(S5_SUB):
        toep_ref[s * gn:(s + 1) * gn, :] = jnp.concatenate(
            [resp[t - s] if t >= s else zero for t in range(S5_SUB)], axis=1).astype(BF16)
        glu_ref[s * gn:(s + 1) * gn, :] = jnp.concatenate(
            [w_glu if t == s else zero for t in range(S5_SUB)], axis=1).astype(BF16)
    l_re, l_im = power(float(S5_SUB))
    al_ref[...] = jnp.concatenate(
        [jnp.concatenate([l_re, l_re], axis=1), jnp.concatenate([-l_im, l_im], axis=1),
         jnp.zeros((6, 2 * gp), F32)], axis=0)


def _s5_prep(lam_re, lam_im, log_dt, b_re, b_im, c_re, c_im, w_glu):
    gn, gp = S5_GROUPS * S5_CH, S5_GROUPS * S5_STATE
    flat = S5_SUB * gn
    row = lambda a: a.reshape(1, gp)
    full = lambda shape: pl.BlockSpec(shape, lambda i: (0,) * len(shape))
    return pl.pallas_call(
        _s5_prep_kernel,
        out_shape=[
            jax.ShapeDtypeStruct((flat, flat), BF16),
            jax.ShapeDtypeStruct((flat, 2 * gp), BF16),
            jax.ShapeDtypeStruct((flat, 2 * gp), BF16),
            jax.ShapeDtypeStruct((8, 2 * gp), F32),
            jax.ShapeDtypeStruct((flat, flat), BF16),
        ],
        grid=(1,),
        in_specs=[full((1, gp))] * 3 + [full((gn, S5_STATE))] * 4 + [full((gn, gn))],
        out_specs=[full((flat, flat)), full((flat, 2 * gp)), full((flat, 2 * gp)), full((8, 2 * gp)),
                   full((flat, flat))],
        compiler_params=_cparams("arbitrary"),
        name="s5_prep",
    )(row(lam_re), row(lam_im), row(jnp.broadcast_to(log_dt[:, None], lam_re.shape)),
      jnp.swapaxes(b_re, 1, 2).reshape(gn, S5_STATE), jnp.swapaxes(b_im, 1, 2).reshape(gn, S5_STATE),
      c_re.reshape(gn, S5_STATE), c_im.reshape(gn, S5_STATE), w_glu)


def _s5_main_kernel(x_ref, toep_ref, wst_ref, wout_ref, al_ref, glu_ref, d_ref, b_ref, o_ref,
                    tok_ref, inj_ref, prev_ref, state_ref):
    @pl.when(pl.program_id(0) == 0)
    def _():
        state_ref[...] = jnp.zeros(state_ref.shape, F32)

    nb, toks, ch = x_ref.shape
    tc = toks // S5_SUB
    gp = state_ref.shape[1] // 2
    tiles = lambda t: [t[:, j * LANE:(j + 1) * LANE] for j in range(t.shape[1] // LANE)]
    for j, piece in enumerate(tiles(x_ref[...].reshape(nb * toks, ch))):
        tok_ref[j] = piece
    x = jnp.concatenate([tok_ref[j, pl.ds(s, nb * tc, stride=S5_SUB), :]
                         for s in range(S5_SUB) for j in range(ch // LANE)], axis=1)
    xb = x.astype(BF16)
    for j, piece in enumerate(tiles(_dot(xb, wst_ref[...]))):
        inj_ref[j] = piece
    n_tiles = inj_ref.shape[0]
    a_c = al_ref[0:1, :]
    a_s = al_ref[1:2, :]
    state = state_ref[...]
    for c in range(tc):
        rows = pl.ds(c, nb, stride=tc)
        for j, piece in enumerate(tiles(state)):
            prev_ref[j, rows, :] = piece
        inj = jnp.concatenate([inj_ref[j, rows, :] for j in range(n_tiles)], axis=1)
        state = state * a_c + pltpu.roll(state, gp, axis=1) * a_s + inj
    state_ref[...] = state
    prev = jnp.concatenate([prev_ref[j] for j in range(n_tiles)], axis=1)
    y = _dot(xb, toep_ref[...]) + _dot_nt_bf(prev, wout_ref[...])
    y = jax.nn.gelu(y + d_ref[...] * x)
    y = y * jax.nn.sigmoid(_dot(y.astype(BF16), glu_ref[...]) + b_ref[...])
    for n, piece in enumerate(tiles(y)):
        tok_ref[n % (ch // LANE), pl.ds(n // (ch // LANE), nb * tc, stride=S5_SUB), :] = piece
    o_ref[...] = jnp.concatenate([tok_ref[j] for j in range(ch // LANE)], axis=1).reshape(nb, toks, ch)


def _s5(u, batch, seq, prm):
    gp = S5_GROUPS * S5_STATE
    flat = S5_SUB * u.shape[1]
    rows = seq // S5_SUB
    tc = next(c for c in (32, 16, 8) if rows % c == 0)
    toep, wst, wout, al, glu = _s5_prep(prm["lam_re"], prm["lam_im"], prm["log_dt"], prm["b_re"], prm["b_im"],
                                        prm["c_re"], prm["c_im"], prm["w_glu"])
    ch = u.shape[1]
    tile = pl.BlockSpec((batch, tc * S5_SUB, ch), lambda i: (0, i, 0))
    y = pl.pallas_call(
        _s5_main_kernel,
        out_shape=jax.ShapeDtypeStruct((batch, seq, ch), F32),
        grid=(rows // tc,),
        in_specs=[tile, _resident((flat, flat)), _resident((flat, 2 * gp)), _resident((flat, 2 * gp)),
                  _resident((8, 2 * gp)), _resident((flat, flat)), _resident((1, flat)), _resident((1, flat))],
        out_specs=tile,
        scratch_shapes=[pltpu.VMEM((ch // LANE, batch * tc * S5_SUB, LANE), F32),
                        pltpu.VMEM((2 * gp // LANE, batch * tc, LANE), F32),
                        pltpu.VMEM((2 * gp // LANE, batch * tc, LANE), F32),
                        pltpu.VMEM((batch, 2 * gp), F32)],
        compiler_params=_cparams("arbitrary"),
        name="s5_main",
    )(u.reshape(batch, seq, ch), toep, wst, wout, al, glu,
      jnp.tile(prm["d"], (1, S5_SUB)), jnp.tile(prm["b_glu"], (1, S5_SUB)))
    return y.reshape(batch * seq, ch)


def _moba_prep_kernel(q_ref, k_ref, v_ref, pos_ref, inv_ref, qo_ref, ko_ref, vt_ref, sel_ref, km_ref):
    n = pl.program_id(1)
    nbp = km_ref.shape[1]

    @pl.when(n == 0)
    def _():
        km_ref[...] = jnp.zeros(km_ref.shape, F32)

    half = HEAD_DIM // 2
    ang = inv_ref[...] * pos_ref[0, 0]
    cos_t, sin_t = jnp.cos(ang), jnp.sin(ang)
    pairs_per_slab = LANE // HEAD_DIM
    cos = jnp.concatenate([cos_t, cos_t] * pairs_per_slab, axis=0).T
    sin = jnp.concatenate([-sin_t, sin_t] * pairs_per_slab, axis=0).T
    lane = _iota((MOBA_BLOCK, LANE), 1)
    low_half = (lane % HEAD_DIM) < half

    def rope(x):
        swapped = jnp.where(low_half, pltpu.roll(x, LANE - half, axis=1), pltpu.roll(x, half, axis=1))
        return x * cos + swapped * sin

    blk_row = _iota((nbp, MOBA_BLOCK), 0)
    for slab in range(BRANCH // LANE):
        q = rope(q_ref[0, :, slab * LANE:(slab + 1) * LANE])
        k = rope(k_ref[0, :, slab * LANE:(slab + 1) * LANE])
        qo_ref[0, 0, slab] = (q * (HEAD_DIM ** -0.5)).astype(BF16)
        k_mean = jnp.mean(k, axis=0, keepdims=True)
        for sub in range(pairs_per_slab):
            h = slab * pairs_per_slab + sub
            own = (lane // HEAD_DIM) == sub
            gate = _dot_nt_hi(km_ref[h], q)
            rank = jnp.zeros((nbp, MOBA_BLOCK), jnp.int32)
            for m in range(nbp):
                gm = gate[m:m + 1, :]
                beats = (gm > gate) | ((gm == gate) & (m < blk_row))
                rank = rank + jnp.where(beats & (m < n), 1, 0)
            sel_ref[0, 0, h] = ((blk_row < n) & (rank < MOBA_TOPK)).astype(F32)
            ko_ref[0, 0, h] = jnp.where(own, k, 0.0).astype(BF16)
            km_ref[h, pl.ds(n, 1), :] = jnp.where(own[0:1], k_mean, 0.0)
    v_t = v_ref[0].T
    tail = (_iota((VT_ROWS - HEAD_DIM, MOBA_BLOCK), 0) == 0).astype(F32)
    for h in range(HEADS):
        vt_ref[0, 0, h] = jnp.concatenate([v_t[h * HEAD_DIM:(h + 1) * HEAD_DIM], tail], axis=0).astype(BF16)


def _moba_prep(qkv, pos, inv):
    b, s, _ = qkv.shape
    nb = s // MOBA_BLOCK
    nbp = -(-nb // 8) * 8
    slabs = BRANCH // LANE
    blk = lambda j: pl.BlockSpec((1, MOBA_BLOCK, BRANCH), lambda i, n, j=j: (i, n, j))
    return pl.pallas_call(
        _moba_prep_kernel,
        out_shape=[
            jax.ShapeDtypeStruct((b, nb, slabs, MOBA_BLOCK, LANE), BF16),
            jax.ShapeDtypeStruct((b, nb, HEADS, MOBA_BLOCK, LANE), BF16),
            jax.ShapeDtypeStruct((b, nb, HEADS, VT_ROWS, MOBA_BLOCK), BF16),
            jax.ShapeDtypeStruct((b, nb, HEADS, nbp, MOBA_BLOCK), F32),
        ],
        grid=(b, nb),
        in_specs=[blk(0), blk(1), blk(2),
                  pl.BlockSpec((1, 1, 1, MOBA_BLOCK), lambda i, n: (i, n, 0, 0)),
                  pl.BlockSpec((HEAD_DIM // 2, 1), lambda i, n: (0, 0))],
        out_specs=[pl.BlockSpec((1, 1, slabs, MOBA_BLOCK, LANE), lambda i, n: (i, n, 0, 0, 0)),
                   pl.BlockSpec((1, 1, HEADS, MOBA_BLOCK, LANE), lambda i, n: (i, n, 0, 0, 0)),
                   pl.BlockSpec((1, 1, HEADS, VT_ROWS, MOBA_BLOCK), lambda i, n: (i, n, 0, 0, 0)),
                   pl.BlockSpec((1, 1, HEADS, nbp, MOBA_BLOCK), lambda i, n: (i, n, 0, 0, 0))],
        scratch_shapes=[pltpu.VMEM((HEADS, nbp, LANE), F32)],
        compiler_params=_cparams("parallel", "arbitrary"),
        name="moba_prep",
    )(qkv, qkv, qkv, pos, inv)


def _moba_attn_kernel(q_ref, k_ref, vt_ref, sel_ref, o_ref):
    i = pl.program_id(1)
    blk = MOBA_BLOCK
    nbp = sel_ref.shape[3]
    key_i = _iota((blk, blk), 0)
    qry_i = _iota((blk, blk), 1)
    sel_row = _iota((nbp, blk), 0)

    def update(blocks, carries, masks):
        scores = [[lax.dot_general(k_ref[0, n, h], q_ref[0, 0, h * HEAD_DIM // LANE], NT_DIMS,
                                   preferred_element_type=F32) for h in range(HEADS)] for n in blocks]
        stats = []
        for h in range(HEADS):
            m_run = carries[h][0]
            s = [jnp.where(masks[b][h], scores[b][h], -jnp.inf) for b in range(len(blocks))]
            m_new = m_run
            for s_b in s:
                m_new = jnp.maximum(m_new, jnp.max(s_b, axis=0, keepdims=True))
            m_safe = jnp.where(m_new == -jnp.inf, 0.0, m_new)
            p = jnp.concatenate([jnp.exp(s_b - m_safe).astype(BF16) for s_b in s], axis=0)
            stats.append((m_new, jnp.exp(m_run - m_safe), p))
        return tuple((stats[h][0], stats[h][1] * carries[h][1]
                      + _dot(jnp.concatenate([vt_ref[0, n, h] for n in blocks], axis=1), stats[h][2]))
                     for h in range(HEADS))

    def picked(n):
        return [jnp.max(jnp.where(sel_row == n, sel_ref[0, 0, h], 0.0), axis=0, keepdims=True) > 0.0
                for h in range(HEADS)]

    def body(t, carries):
        return update([2 * t, 2 * t + 1], carries, [picked(2 * t), picked(2 * t + 1)])

    init = (jnp.full((1, blk), -jnp.inf, F32), jnp.zeros((VT_ROWS, blk), F32))
    carries = lax.fori_loop(0, i // 2, body, (init,) * HEADS)
    left = jnp.maximum(i - 1, 0)
    odd = (i % 2) == 1
    carries = update([left, i], carries, [[m & odd for m in picked(left)], [key_i <= qry_i] * HEADS])
    o_ref[0] = jnp.concatenate([acc[0:HEAD_DIM] / acc[HEAD_DIM:HEAD_DIM + 1] for _, acc in carries], axis=0).T


def _moba_attn(q, k, vt, sel):
    b, nb = q.shape[:2]
    nbp = sel.shape[3]
    return pl.pallas_call(
        _moba_attn_kernel,
        out_shape=jax.ShapeDtypeStruct((b, nb * MOBA_BLOCK, BRANCH), F32),
        grid=(b, nb),
        in_specs=[
            pl.BlockSpec((1, 1, BRANCH // LANE, MOBA_BLOCK, LANE), lambda i, n: (i, n, 0, 0, 0)),
            pl.BlockSpec((1, nb, HEADS, MOBA_BLOCK, LANE), lambda i, n: (i, 0, 0, 0, 0)),
            pl.BlockSpec((1, nb, HEADS, VT_ROWS, MOBA_BLOCK), lambda i, n: (i, 0, 0, 0, 0)),
            pl.BlockSpec((1, 1, HEADS, nbp, MOBA_BLOCK), lambda i, n: (i, n, 0, 0, 0)),
        ],
        out_specs=pl.BlockSpec((1, MOBA_BLOCK, BRANCH), lambda i, n: (i, n, 0)),
        compiler_params=_cparams("parallel", "arbitrary"),
        name="moba_attn",
    )(q, k, vt, sel)


def _moba(qkv, positions):
    b, s, _ = qkv.shape
    inv = ROPE_THETA ** (-jnp.arange(0, HEAD_DIM, 2, dtype=F32) / HEAD_DIM)
    pos = positions.astype(F32).reshape(b, s // MOBA_BLOCK, 1, MOBA_BLOCK)
    return _moba_attn(*_moba_prep(qkv, pos, inv[:, None]))


def _permute_w_in(w_in):
    cuts = [0]
    for width in (3 * BRANCH, BRANCH, HEADS, HEADS, 2 * BRANCH, BRANCH, BRANCH, HEADS, HEADS, BRANCH, 3 * BRANCH):
        cuts.append(cuts[-1] + width)
    seg = [w_in[:, cuts[n]:cuts[n + 1]] for n in range(len(cuts) - 1)]
    gdn_qkv, gdn_z, gdn_b, gdn_a, ml_qk, ml_v, ml_o, ml_i, ml_f, s5_u, moba_qkv = seg
    pad = jnp.zeros((w_in.shape[0], SMALL_W - 4 * HEADS), w_in.dtype)
    return jnp.concatenate([gdn_qkv, gdn_z, ml_qk, ml_v, ml_o, s5_u, moba_qkv, gdn_b, gdn_a, ml_i, ml_f, pad], axis=1)


def _token_tile(t, largest=512):
    for tm in (1024, 512, 256, 128, 64, 32, 16, 8):
        if tm > largest:
            continue
        if t % tm == 0:
            return tm
    raise ValueError(f"token count {t} is not a multiple of 8")


def _chunks_per_step(seq):
    n_chunks = seq // CHUNK
    for cps in (8, 4, 2, 1):
        if n_chunks % cps == 0:
            return cps


def _ff_slice(f):
    for sw in (2 * LANE, LANE):
        if f % sw == 0:
            return sw
    raise ValueError(f"ffn width {f} is not a multiple of {LANE}")


def kernel(x, p, positions, ffn1_norm, ffn1_w_gu, ffn1_w_down, mix_norm, w_in, gdn_conv, gdn_a_log, gdn_dt_bias, gdn_norm, mlstm_conv, mlstm_i_bias, mlstm_f_bias, mlstm_norm, s5_lambda_re, s5_lambda_im, s5_b_re, s5_b_im, s5_c_re, s5_c_im, s5_d, s5_log_dt, s5_w_glu, s5_b_glu, w_gate, w_branch, w_out, ffn2_norm, ffn2_w_gu, ffn2_w_down, ple_norm, ple_w_proj, ple_w_gate, final_norm):
    batch, seq, d = x.shape
    depth = p.shape[0]
    t = batch * seq
    tm = _token_tile(t)
    tm_ffn = _token_tile(t, largest=1024)
    tf = _ff_slice(ffn1_w_down.shape[1])
    cps = _chunks_per_step(seq)
    bf = lambda a: a.astype(BF16)
    row = lambda a: a.reshape(1, -1)
    seq3 = lambda a: a.reshape(batch, seq, a.shape[-1])

    h = x.reshape(t, d)
    for i in range(depth):
        h = _ffn(h, row(ffn1_norm[i]), bf(ffn1_w_gu[i]), bf(ffn1_w_down[i]), tm_ffn, tf)

        gdn_qkv, gdn_z, ml_qk, ml_v, ml_o, s5_u, moba_qkv, small = _inproj(
            h, row(mix_norm[i]), bf(_permute_w_in(w_in[i])), tm)
        y_gdn = _gdn(seq3(gdn_qkv), seq3(gdn_z), seq3(small), gdn_conv[i], row(gdn_a_log[i]),
                     row(gdn_dt_bias[i]), row(gdn_norm[i]), cps)
        y_mlstm = _mlstm(seq3(ml_qk), seq3(ml_v), seq3(ml_o), seq3(small), mlstm_conv[i],
                         row(mlstm_i_bias[i]), row(mlstm_f_bias[i]), row(mlstm_norm[i]), cps)
        s5_prm = dict(lam_re=s5_lambda_re[i], lam_im=s5_lambda_im[i], log_dt=s5_log_dt[i],
                      b_re=s5_b_re[i], b_im=s5_b_im[i], c_re=s5_c_re[i], c_im=s5_c_im[i],
                      d=row(s5_d[i]), w_glu=s5_w_glu[i], b_glu=row(s5_b_glu[i]))
        y_s5 = _s5(s5_u, batch, seq, s5_prm)
        y_moba = _moba(seq3(moba_qkv), positions)
        ys = (y_gdn.reshape(t, BRANCH), y_mlstm.reshape(t, BRANCH), y_s5, y_moba.reshape(t, BRANCH))
        h = _merge(h, row(mix_norm[i]), ys, bf(w_gate[i]), bf(w_branch[i]), bf(w_out[i]), tm)

        ple = (row(ple_norm[i]), p[i].reshape(t, -1), bf(ple_w_proj[i]), bf(ple_w_gate[i]), row(final_norm))
        h = _ffn(h, row(ffn2_norm[i]), bf(ffn2_w_gu[i]), bf(ffn2_w_down[i]), tm_ffn, tf, ple, i == depth - 1)
    return h.reshape(batch, seq, d)
```

```python
import functools
import math

import jax
import jax.numpy as jnp
from jax import lax
from jax.experimental import pallas as pl
from jax.experimental.pallas import tpu as pltpu

F32 = jnp.float32
BF16 = jnp.bfloat16
HIGHEST = lax.Precision.HIGHEST

NORM_EPS = 1e-6
HEADS = 4
HEAD_DIM = 64
BRANCH = HEADS * HEAD_DIM
CHUNK = 64
CONV_WIDTH = 4
CONV_HIST = 8
GDN_GROUP = 4
S5_GROUPS = 16
S5_CH = 16
S5_STATE = 64
S5_SUB = 4
MOBA_BLOCK = 256
MOBA_TOPK = 3
VT_ROWS = HEAD_DIM + 16
ROPE_THETA = 10000.0
LANE = 128
SMALL_W = LANE
V7X_VMEM_LIMIT = 56 * 1024 * 1024

NT_DIMS = (((1,), (1,)), ((), ()))


def _cparams(*sem):
    return pltpu.CompilerParams(dimension_semantics=sem, vmem_limit_bytes=V7X_VMEM_LIMIT)


def _rms(x, w):
    return x * lax.rsqrt(jnp.mean(x * x, axis=-1, keepdims=True) + NORM_EPS) * w


def _silu(x):
    return x * jax.nn.sigmoid(x)


def _softplus(x):
    return jnp.maximum(x, 0.0) + jnp.log1p(jnp.exp(-jnp.abs(x)))


def _dot(a, b):
    return jnp.dot(a, b, preferred_element_type=F32)


def _dot_bf(a, b):
    return jnp.dot(a.astype(BF16), b.astype(BF16), preferred_element_type=F32)


def _dot_nt_bf(a, b):
    return lax.dot_general(a.astype(BF16), b.astype(BF16), NT_DIMS, preferred_element_type=F32)


def _dot_nt_hi(a, b):
    return lax.dot_general(a, b, NT_DIMS, precision=HIGHEST, preferred_element_type=F32)


def _split_bf(a, parts):
    out = []
    for _ in range(parts):
        piece = a.astype(BF16)
        out.append(piece)
        a = a - piece.astype(F32)
    return out


def _dot_x3(a, b):
    a_hi, a_lo = _split_bf(a, 2)
    b_hi, b_lo = _split_bf(b, 2)
    return _dot(jnp.concatenate([a_hi, a_hi, a_lo], axis=1), jnp.concatenate([b_hi, b_lo, b_hi], axis=0))


def _dot_exact01(a01_x3, b):
    return _dot(a01_x3, jnp.concatenate(_split_bf(b, 3), axis=0))


def _iota(shape, dim):
    return lax.broadcasted_iota(jnp.int32, shape, dim)


def _head_block_ones():
    r = _iota((BRANCH, BRANCH), 0) // HEAD_DIM
    c = _iota((BRANCH, BRANCH), 1) // HEAD_DIM
    return (r == c).astype(BF16)


def _head_sums(x, ones_bd, parts):
    pieces = _split_bf(x, parts)
    out = _dot(pieces[0], ones_bd)
    for piece in pieces[1:]:
        out = out + _dot(piece, ones_bd)
    return out


def _head_bcast(cols):
    rows = cols.shape[0]
    head = _iota((rows, BRANCH), 1) // HEAD_DIM
    out = jnp.broadcast_to(cols[:, 0:1], (rows, BRANCH))
    for h in range(1, HEADS):
        out = jnp.where(head == h, jnp.broadcast_to(cols[:, h:h + 1], (rows, BRANCH)), out)
    return out


def _causal_conv_silu(x_ref, w_ref, pad_ref, first):
    rows = x_ref.shape[1]

    @pl.when(first)
    def _():
        pad_ref[0:CONV_HIST, :] = jnp.zeros((CONV_HIST, pad_ref.shape[1]), F32)

    pad_ref[CONV_HIST:CONV_HIST + rows, :] = x_ref[0]
    w = w_ref[...]
    acc = w[CONV_WIDTH - 1:CONV_WIDTH, :] * pad_ref[CONV_HIST:CONV_HIST + rows, :]
    for back in range(1, CONV_WIDTH):
        tap = CONV_WIDTH - 1 - back
        acc = acc + w[tap:tap + 1, :] * pad_ref[CONV_HIST - back:CONV_HIST - back + rows, :]
    pad_ref[0:CONV_HIST, :] = pad_ref[rows:rows + CONV_HIST, :]
    return _silu(acc)


def _tri_consts():
    r = _iota((3 * CHUNK, 3 * CHUNK), 0)
    k = _iota((3 * CHUNK, 3 * CHUNK), 1) % CHUNK
    i = r % CHUNK
    part = r // CHUNK
    return (((part == 0) & (k <= i)) | ((part == 1) & (k > i)) | (part == 2)).astype(BF16)


def _resident(shape):
    return pl.BlockSpec(shape, lambda *_: (0,) * len(shape), pipeline_mode=pl.Buffered(1))


def _ffn_kernel(x_ref, nw_ref, wgu_ref, wd_ref, *rest, sw, ple, final):
    o_ref = rest[-1]
    x = x_ref[...]
    xn = _rms(x, nw_ref[...]).astype(BF16)
    f = wd_ref.shape[0]

    def gate_up(j):
        return (_dot(xn, wgu_ref[:, j * sw:(j + 1) * sw]), _dot(xn, wgu_ref[:, f + j * sw:f + (j + 1) * sw]))

    ahead = gate_up(0)
    acc = None
    for j in range(f // sw):
        gate, up = ahead
        if (j + 1) * sw < f:
            ahead = gate_up(j + 1)
        part = _dot((_silu(gate) * up).astype(BF16), wd_ref[j * sw:(j + 1) * sw, :])
        acc = part if acc is None else acc + part
    out = x + 0.5 * acc
    if ple:
        pn_ref, p_ref, wp_ref, wg_ref, fw_ref = rest[:-1]
        gate = jax.nn.sigmoid(_dot(_rms(out, pn_ref[...]).astype(BF16), wg_ref[...]))
        out = out + _dot(p_ref[...].astype(BF16), wp_ref[...]) * gate
        if final:
            out = _rms(out, fw_ref[...])
    o_ref[...] = out


def _ffn(h, norm_w, w_gu, w_down, tm, sw, ple=None, final=False):
    t, d = h.shape
    f = w_down.shape[0]
    operands = [h, norm_w, w_gu, w_down]
    in_specs = [pl.BlockSpec((tm, d), lambda i: (i, 0)), _resident((1, d)), _resident((d, 2 * f)), _resident((f, d))]
    if ple is not None:
        pd = ple[1].shape[1]
        operands += list(ple)
        in_specs += [_resident((1, d)), pl.BlockSpec((tm, pd), lambda i: (i, 0)), _resident((pd, d)),
                     _resident((d, d)), _resident((1, d))]
    return pl.pallas_call(
        functools.partial(_ffn_kernel, sw=sw, ple=ple is not None, final=final),
        out_shape=jax.ShapeDtypeStruct((t, d), F32),
        grid=(t // tm,),
        in_specs=in_specs,
        out_specs=pl.BlockSpec((tm, d), lambda i: (i, 0)),
        compiler_params=_cparams("parallel"),
        name="ffn",
    )(*operands)


IN_SEGMENTS = (3 * BRANCH, BRANCH, 2 * BRANCH, BRANCH, BRANCH, BRANCH, 3 * BRANCH, SMALL_W)


def _inproj_kernel(x_ref, nw_ref, w_ref, *o_refs):
    xn = _rms(x_ref[...], nw_ref[...]).astype(BF16)
    start = 0
    for o_ref, width in zip(o_refs, IN_SEGMENTS):
        o_ref[...] = _dot(xn, w_ref[:, start:start + width])
        start += width


def _inproj(h, norm_w, w_perm, tm):
    t, d = h.shape
    n = w_perm.shape[1]
    return pl.pallas_call(
        _inproj_kernel,
        out_shape=[jax.ShapeDtypeStruct((t, w), F32) for w in IN_SEGMENTS],
        grid=(t // tm,),
        in_specs=[
            pl.BlockSpec((tm, d), lambda i: (i, 0)),
            pl.BlockSpec((1, d), lambda i: (0, 0)),
            pl.BlockSpec((d, n), lambda i: (0, 0)),
        ],
        out_specs=[pl.BlockSpec((tm, w), lambda i: (i, 0)) for w in IN_SEGMENTS],
        compiler_params=_cparams("parallel"),
        name="inproj",
    )(h, norm_w, w_perm)


def _merge_kernel(h_ref, nw_ref, y0_ref, y1_ref, y2_ref, y3_ref, wg_ref, wb_ref, wo_ref, o_ref):
    h = h_ref[...]
    u = _rms(h, nw_ref[...]).astype(BF16)
    merged = None
    for b, y_ref in enumerate((y0_ref, y1_ref, y2_ref, y3_ref)):
        term = jax.nn.sigmoid(_dot(u, wg_ref[b])) * _dot(y_ref[...].astype(BF16), wb_ref[b])
        merged = term if merged is None else merged + term
    o_ref[...] = h + _dot(merged.astype(BF16), wo_ref[...])


def _merge(h, norm_w, ys, w_gate, w_branch, w_out, tm):
    t, d = h.shape
    nb, bw, _ = w_branch.shape
    return pl.pallas_call(
        _merge_kernel,
        out_shape=jax.ShapeDtypeStruct((t, d), F32),
        grid=(t // tm,),
        in_specs=[
            pl.BlockSpec((tm, d), lambda i: (i, 0)),
            pl.BlockSpec((1, d), lambda i: (0, 0)),
        ] + [pl.BlockSpec((tm, bw), lambda i: (i, 0)) for _ in range(nb)] + [
            pl.BlockSpec((nb, d, d), lambda i: (0, 0, 0)),
            pl.BlockSpec((nb, bw, d), lambda i: (0, 0, 0)),
            pl.BlockSpec((d, d), lambda i: (0, 0)),
        ],
        out_specs=pl.BlockSpec((tm, d), lambda i: (i, 0)),
        compiler_params=_cparams("parallel"),
        name="merge",
    )(h, norm_w, *ys, w_gate, w_branch, w_out)


def _gdn_kernel(qkv_ref, z_ref, sm_ref, cw_ref, alog_ref, dtb_ref, nw_ref, o_ref, pad_ref, state_ref, *, cps):
    step = pl.program_id(1)

    @pl.when(step == 0)
    def _():
        state_ref[...] = jnp.zeros(state_ref.shape, F32)

    x_all = _causal_conv_silu(qkv_ref, cw_ref, pad_ref, step == 0)
    ones_bd = _head_block_ones()
    tri = _tri_consts()
    eye = (_iota((CHUNK, CHUNK), 0) == _iota((CHUNK, CHUNK), 1)).astype(BF16)
    row_c = _iota((CHUNK, BRANCH), 0)
    col_c = _iota((CHUNK, BRANCH), 1) % HEAD_DIM
    gcs = GDN_GROUP if cps % GDN_GROUP == 0 else cps
    rows = gcs * CHUNK
    row = _iota((rows, BRANCH), 0) % CHUNK
    col = _iota((rows, BRANCH), 1) % HEAD_DIM
    q_eff, out_loc, inject, mix, e_tot = {}, {}, {}, {}, {}
    for grp in range(cps // gcs):
        x = x_all[grp * rows:(grp + 1) * rows]
        sm = sm_ref[0, grp * rows:(grp + 1) * rows, :]
        beta = _head_bcast(jax.nn.sigmoid(sm[:, 0:HEADS]))
        g = _head_bcast(-jnp.exp(alog_ref[...]) * _softplus(sm[:, HEADS:2 * HEADS] + dtb_ref[...]))
        q, k, v = x[:, 0:BRANCH], x[:, BRANCH:2 * BRANCH], x[:, 2 * BRANCH:3 * BRANCH]
        q = q * lax.rsqrt(_head_sums(q * q, ones_bd, 2) + NORM_EPS) * (HEAD_DIM ** -0.5)
        k = k * lax.rsqrt(_head_sums(k * k, ones_bd, 2) + NORM_EPS)
        sums = []
        for ci in range(gcs):
            gb = g[ci * CHUNK:(ci + 1) * CHUNK]
            sums.append(_dot_exact01(tri, jnp.concatenate([gb, jnp.where(row_c > col_c, gb, 0.0)], axis=1)))
        stack = lambda r0, c0: jnp.concatenate([s[r0:r0 + CHUNK, c0:c0 + BRANCH] for s in sums], axis=0)
        e_gc = jnp.exp(stack(0, 0))
        e_rest = jnp.exp(stack(CHUNK, 0))
        e_total = jnp.exp(stack(2 * CHUNK, 0))
        decay = jnp.where(row >= col, jnp.exp(stack(0, BRANCH)), 0.0)
        k_beta = k * beta
        q_dec = q * e_gc
        k_dec = (k * e_rest).astype(BF16)
        rhs_u = v * beta
        rhs_w = k_beta * e_gc

        probs = [(ci, h) for ci in range(gcs) for h in range(HEADS)]

        def cut(t, p):
            return t[p[0] * CHUNK:(p[0] + 1) * CHUNK, p[1] * HEAD_DIM:(p[1] + 1) * HEAD_DIM]

        gram = {p: _dot_nt_bf(jnp.concatenate([cut(k_beta, p), cut(q, p)], axis=0), cut(k, p)) for p in probs}
        k_dec_t = {p: lax.dot_general(eye, cut(k_dec, p), NT_DIMS, preferred_element_type=F32) for p in probs}
        power, sol, qk = {}, {}, {}
        for p in probs:
            dec = cut(decay, p)
            power[p] = jnp.where(cut(row > col, p), gram[p][0:CHUNK] * dec, 0.0)
            qk[p] = gram[p][CHUNK:2 * CHUNK] * dec
            sol[p] = jnp.concatenate([cut(rhs_u, p), cut(rhs_w, p)], axis=1)
        n_factors = int(math.log2(CHUNK))
        for j in range(n_factors):
            last = j == n_factors - 1
            prods = {p: _dot_x3(power[p], sol[p] if last else jnp.concatenate([sol[p], power[p]], axis=1))
                     for p in probs}
            for p in probs:
                term = prods[p][:, 0:2 * HEAD_DIM]
                sol[p] = sol[p] - term if j == 0 else sol[p] + term
                if not last:
                    power[p] = prods[p][:, 2 * HEAD_DIM:3 * HEAD_DIM]
        both = {p: _dot_bf(jnp.concatenate([qk[p], k_dec_t[p]], axis=0), sol[p]) for p in probs}
        for p in probs:
            key = (grp * gcs + p[0], p[1])
            out_loc[key] = both[p][0:CHUNK, 0:HEAD_DIM]
            q_eff[key] = cut(q_dec, p) - both[p][0:CHUNK, HEAD_DIM:2 * HEAD_DIM]
            inject[key] = both[p][CHUNK:2 * CHUNK, 0:HEAD_DIM]
            mix[key] = both[p][CHUNK:2 * CHUNK, HEAD_DIM:2 * HEAD_DIM]
            e_tot[key] = cut(e_total, p)
    state = [state_ref[h] for h in range(HEADS)]
    entering, from_state = {}, {}
    for ci in range(cps):
        mixed = [_dot_bf(mix[ci, h], state[h]) for h in range(HEADS)]
        if ci:
            from_state.update({(ci - 1, h): _dot_bf(q_eff[ci - 1, h], entering[ci - 1, h]) for h in range(HEADS)})
        for h in range(HEADS):
            entering[ci, h] = state[h]
            state[h] = state[h] * e_tot[ci, h] - mixed[h] + inject[ci, h]
    from_state.update({(cps - 1, h): _dot_bf(q_eff[cps - 1, h], entering[cps - 1, h]) for h in range(HEADS)})
    for h in range(HEADS):
        state_ref[h] = state[h]
    out = jnp.concatenate([jnp.concatenate([from_state[ci, h] + out_loc[ci, h] for h in range(HEADS)], axis=1)
                           for ci in range(cps)], axis=0)
    mean_sq = _head_sums(out * out, ones_bd, 2) * (1.0 / HEAD_DIM)
    nw = jnp.concatenate([nw_ref[...]] * HEADS, axis=1)
    o_ref[0] = out * lax.rsqrt(mean_sq + NORM_EPS) * nw * _silu(z_ref[0])


def _gdn(qkv, z, small, conv_w, a_log, dt_bias, norm_w, cps):
    b, s, _ = qkv.shape
    rows = cps * CHUNK
    return pl.pallas_call(
        functools.partial(_gdn_kernel, cps=cps),
        out_shape=jax.ShapeDtypeStruct((b, s, BRANCH), F32),
        grid=(b, s // rows),
        in_specs=[
            pl.BlockSpec((1, rows, 3 * BRANCH), lambda i, c: (i, c, 0)),
            pl.BlockSpec((1, rows, BRANCH), lambda i, c: (i, c, 0)),
            pl.BlockSpec((1, rows, SMALL_W), lambda i, c: (i, c, 0)),
            pl.BlockSpec((CONV_WIDTH, 3 * BRANCH), lambda i, c: (0, 0)),
            pl.BlockSpec((1, HEADS), lambda i, c: (0, 0)),
            pl.BlockSpec((1, HEADS), lambda i, c: (0, 0)),
            pl.BlockSpec((1, HEAD_DIM), lambda i, c: (0, 0)),
        ],
        out_specs=pl.BlockSpec((1, rows, BRANCH), lambda i, c: (i, c, 0)),
        scratch_shapes=[
            pltpu.VMEM((CONV_HIST + rows, 3 * BRANCH), F32),
            pltpu.VMEM((HEADS, HEAD_DIM, HEAD_DIM), F32),
        ],
        compiler_params=_cparams("parallel", "arbitrary"),
        name="gdn",
    )(qkv, z, small, conv_w, a_log, dt_bias, norm_w)


def _mlstm_kernel(qk_ref, v_ref, o_ref_in, sm_ref, cw_ref, ib_ref, fb_ref, nw_ref, out_ref,
                  pad_ref, c_ref, n_ref, m_ref, *, cps):
    step = pl.program_id(1)

    @pl.when(step == 0)
    def _():
        c_ref[...] = jnp.zeros(c_ref.shape, F32)
        n_ref[...] = jnp.zeros(n_ref.shape, F32)
        m_ref[...] = jnp.zeros(m_ref.shape, F32)

    rows = cps * CHUNK
    x = _causal_conv_silu(qk_ref, cw_ref, pad_ref, step == 0)
    q = x[:, 0:BRANCH]
    k = x[:, BRANCH:2 * BRANCH] * (HEAD_DIM ** -0.5)
    v = v_ref[0]
    sm = sm_ref[0]
    log_i = _head_bcast(sm[:, 2 * HEADS:3 * HEADS] + ib_ref[...])
    log_f = _head_bcast(-_softplus(-(sm[:, 3 * HEADS:4 * HEADS] + fb_ref[...])))
    ones_bd = _head_block_ones()

    row_c = _iota((CHUNK, BRANCH), 0)
    col_c = _iota((CHUNK, BRANCH), 1) % HEAD_DIM
    tri = _tri_consts()
    sums = []
    for ci in range(cps):
        lf = log_f[ci * CHUNK:(ci + 1) * CHUNK]
        li = log_i[ci * CHUNK:(ci + 1) * CHUNK]
        sums.append(_dot_exact01(tri, jnp.concatenate(
            [lf, jnp.where(row_c > col_c, lf, 0.0), jnp.where(row_c == col_c, li, 0.0)], axis=1)))
    stack = lambda r0, c0: jnp.concatenate([s[r0:r0 + CHUNK, c0:c0 + BRANCH] for s in sums], axis=0)
    row = _iota((rows, BRANCH), 0) % CHUNK
    col = _iota((rows, BRANCH), 1) % HEAD_DIM
    b_col = stack(0, 0)
    b_last = stack(2 * CHUNK, 0)
    a_key = stack(CHUNK, 0) + log_i
    d_intra = jnp.where(row >= col, stack(0, BRANCH) + stack(2 * CHUNK, 2 * BRANCH), -jnp.inf)
    run = log_i - b_col
    shift = 1
    while shift < CHUNK:
        run = jnp.where(row >= shift, jnp.maximum(run, pltpu.roll(run, shift, axis=0)), run)
        shift *= 2
    m_mem = m_ref[...]
    m_prev, m_next, scale = [], [], []
    for ci in range(cps):
        last = ci * CHUNK + CHUNK - 1
        g_c = b_last[last:last + 1]
        m_new = jnp.maximum(g_c + m_mem, g_c + run[last:last + 1])
        m_prev.append(m_mem)
        m_next.append(m_new)
        scale.append(jnp.exp(g_c + m_mem - m_new))
        m_mem = m_new
    m_ref[...] = m_mem
    per_chunk = lambda rows_: jnp.concatenate([jnp.broadcast_to(r, (CHUNK, BRANCH)) for r in rows_], axis=0)
    m_inter = b_col + per_chunk(m_prev)
    m_t = jnp.maximum(m_inter, b_col + run)
    w_inter = jnp.exp(m_inter - m_t)
    e_intra = jnp.exp(d_intra - m_t)
    k_w = k * jnp.exp(a_key - per_chunk(m_next))
    k_w_bf = k_w.astype(BF16)

    probs = [(ci, h) for ci in range(cps) for h in range(HEADS)]

    def cut(t, p):
        return t[p[0] * CHUNK:(p[0] + 1) * CHUNK, p[1] * HEAD_DIM:(p[1] + 1) * HEAD_DIM]

    def paste(pieces):
        return jnp.concatenate([jnp.concatenate([pieces[ci, h] for h in range(HEADS)], axis=1)
                                for ci in range(cps)], axis=0)

    eye = (_iota((CHUNK, CHUNK), 0) == _iota((CHUNK, CHUNK), 1)).astype(BF16)
    s_qk = {p: _dot_nt_bf(cut(q, p), cut(k, p)) * cut(e_intra, p) for p in probs}
    k_w_t = {p: lax.dot_general(eye, cut(k_w_bf, p), NT_DIMS, preferred_element_type=F32) for p in probs}
    intra = {p: _dot_bf(s_qk[p], cut(v, p)) for p in probs}
    inject = {p: _dot_bf(k_w_t[p], cut(v, p)) for p in probs}
    c_mem = [c_ref[h] for h in range(HEADS)]
    n_mem = n_ref[...]
    c_prev, n_prev = {}, []
    for ci in range(cps):
        n_prev.append(n_mem)
        n_mem = n_mem * scale[ci] + jnp.sum(k_w[ci * CHUNK:(ci + 1) * CHUNK], axis=0, keepdims=True)
        for h in range(HEADS):
            c_prev[ci, h] = c_mem[h]
            c_mem[h] = c_mem[h] * scale[ci][:, h * HEAD_DIM:(h + 1) * HEAD_DIM] + inject[ci, h]
    for h in range(HEADS):
        c_ref[h] = c_mem[h]
    n_ref[...] = n_mem
    inter = {p: _dot_bf(cut(q, p), c_prev[p]) for p in probs}
    num = w_inter * paste(inter) + paste(intra)
    qn = w_inter * _head_sums(q * per_chunk(n_prev), ones_bd, 3) + _head_sums(paste(s_qk), ones_bd, 3)
    h_tilde = num / jnp.maximum(jnp.abs(qn), jnp.exp(-m_t))
    mean_sq = _head_sums(h_tilde * h_tilde, ones_bd, 2) * (1.0 / HEAD_DIM)
    out_ref[0] = jax.nn.sigmoid(o_ref_in[0]) * (h_tilde * lax.rsqrt(mean_sq + NORM_EPS) * nw_ref[...])


def _mlstm(qk, v, o_pre, small, conv_w, i_bias, f_bias, norm_w, cps):
    b, s, _ = qk.shape
    rows = cps * CHUNK
    return pl.pallas_call(
        functools.partial(_mlstm_kernel, cps=cps),
        out_shape=jax.ShapeDtypeStruct((b, s, BRANCH), F32),
        grid=(b, s // rows),
        in_specs=[
            pl.BlockSpec((1, rows, 2 * BRANCH), lambda i, c: (i, c, 0)),
            pl.BlockSpec((1, rows, BRANCH), lambda i, c: (i, c, 0)),
            pl.BlockSpec((1, rows, BRANCH), lambda i, c: (i, c, 0)),
            pl.BlockSpec((1, rows, SMALL_W), lambda i, c: (i, c, 0)),
            pl.BlockSpec((CONV_WIDTH, 2 * BRANCH), lambda i, c: (0, 0)),
            pl.BlockSpec((1, HEADS), lambda i, c: (0, 0)),
            pl.BlockSpec((1, HEADS), lambda i, c: (0, 0)),
            pl.BlockSpec((1, BRANCH), lambda i, c: (0, 0)),
        ],
        out_specs=pl.BlockSpec((1, rows, BRANCH), lambda i, c: (i, c, 0)),
        scratch_shapes=[
            pltpu.VMEM((CONV_HIST + rows, 2 * BRANCH), F32),
            pltpu.VMEM((HEADS, HEAD_DIM, HEAD_DIM), F32),
            pltpu.VMEM((1, BRANCH), F32),
            pltpu.VMEM((1, BRANCH), F32),
        ],
        compiler_params=_cparams("parallel", "arbitrary"),
        name="mlstm",
    )(qk, v, o_pre, small, conv_w, i_bias, f_bias, norm_w)


def _s5_prep_kernel(lre_ref, lim_ref, ldt_ref, btre_ref, btim_ref, cre_ref, cim_ref, wglu_ref,
                    toep_ref, wst_ref, wout_ref, al_ref, glu_ref):
    gn, gp = S5_GROUPS * S5_CH, S5_GROUPS * S5_STATE
    lr = lre_ref[...]
    li = lim_ref[...]
    dt = jnp.exp(ldt_ref[...])
    lam = lr * dt
    theta = li * dt

    def power(e):
        mg = jnp.exp(lam * e)
        return mg * jnp.cos(theta * e), mg * jnp.sin(theta * e)

    def cmul(xr, xi, yr, yi):
        return xr * yr - xi * yi, xr * yi + xi * yr

    a_re, a_im = power(1.0)
    den = lr * lr + li * li
    z_re = ((a_re - 1.0) * lr + a_im * li) / den
    z_im = (a_im * lr - (a_re - 1.0) * li) / den
    same_group = (_iota((gn, gp), 0) // S5_CH) == (_iota((gn, gp), 1) // S5_STATE)

    def spread(t):
        return jnp.where(same_group, jnp.concatenate([t] * S5_GROUPS, axis=1), 0.0)

    bb_re, bb_im = cmul(z_re, z_im, spread(btre_ref[...]), spread(btim_ref[...]))
    c_re, c_im = spread(cre_ref[...]), spread(cim_ref[...])
    for s in range(S5_SUB):
        wr, wi = cmul(*power(float(S5_SUB - 1 - s)), bb_re, bb_im)
        wst_ref[s * gn:(s + 1) * gn, :] = jnp.concatenate([wr, wi], axis=1).astype(BF16)
        wr, wi = cmul(*power(float(s + 1)), c_re, c_im)
        wout_ref[s * gn:(s + 1) * gn, :] = jnp.concatenate([wr, -wi], axis=1).astype(BF16)
    c_cat = jnp.concatenate([c_re, -c_im], axis=1)
    resp = []
    for tau in range(S5_SUB):
        wr, wi = cmul(*power(float(tau)), bb_re, bb_im)
        resp.append(_dot_nt_hi(jnp.concatenate([wr, wi], axis=1), c_cat))
    zero = jnp.zeros((gn, gn), F32)
    w_glu = wglu_ref[...]
    for s in range(S5_SUB):
        toep_ref[s * gn:(s + 1) * gn, :] = jnp.concatenate(
            [resp[t - s] if t >= s else zero for t in range(S5_SUB)], axis=1).astype(BF16)
        glu_ref[s * gn:(s + 1) * gn, :] = jnp.concatenate(
            [w_glu if t == s else zero for t in range(S5_SUB)], axis=1).astype(BF16)
    l_re, l_im = power(float(S5_SUB))
    al_ref[...] = jnp.concatenate(
        [jnp.concatenate([l_re, l_re], axis=1), jnp.concatenate([-l_im, l_im], axis=1),
         jnp.zeros((6, 2 * gp), F32)], axis=0)


def _s5_prep(lam_re, lam_im, log_dt, b_re, b_im, c_re, c_im, w_glu):
    gn, gp = S5_GROUPS * S5_CH, S5_GROUPS * S5_STATE
    flat = S5_SUB * gn
    row = lambda a: a.reshape(1, gp)
    full = lambda shape: pl.BlockSpec(shape, lambda i: (0,) * len(shape))
    return pl.pallas_call(
        _s5_prep_kernel,
        out_shape=[
            jax.ShapeDtypeStruct((flat, flat), BF16),
            jax.ShapeDtypeStruct((flat, 2 * gp), BF16),
            jax.ShapeDtypeStruct((flat, 2 * gp), BF16),
            jax.ShapeDtypeStruct((8, 2 * gp), F32),
            jax.ShapeDtypeStruct((flat, flat), BF16),
        ],
        grid=(1,),
        in_specs=[full((1, gp))] * 3 + [full((gn, S5_STATE))] * 4 + [full((gn, gn))],
        out_specs=[full((flat, flat)), full((flat, 2 * gp)), full((flat, 2 * gp)), full((8, 2 * gp)),
                   full((flat, flat))],
        compiler_params=_cparams("arbitrary"),
        name="s5_prep",
    )(row(lam_re), row(lam_im), row(jnp.broadcast_to(log_dt[:, None], lam_re.shape)),
      jnp.swapaxes(b_re, 1, 2).reshape(gn, S5_STATE), jnp.swapaxes(b_im, 1, 2).reshape(gn, S5_STATE),
      c_re.reshape(gn, S5_STATE), c_im.reshape(gn, S5_STATE), w_glu)


def _s5_main_kernel(x_ref, toep_ref, wst_ref, wout_ref, al_ref, glu_ref, d_ref, b_ref, o_ref,
                    tok_ref, inj_ref, prev_ref, state_ref):
    @pl.when(pl.program_id(0) == 0)
    def _():
        state_ref[...] = jnp.zeros(state_ref.shape, F32)

    nb, toks, ch = x_ref.shape
    tc = toks // S5_SUB
    gp = state_ref.shape[1] // 2
    tiles = lambda t: [t[:, j * LANE:(j + 1) * LANE] for j in range(t.shape[1] // LANE)]
    for j, piece in enumerate(tiles(x_ref[...].reshape(nb * toks, ch))):
        tok_ref[j] = piece
    x = jnp.concatenate([tok_ref[j, pl.ds(s, nb * tc, stride=S5_SUB), :]
                         for s in range(S5_SUB) for j in range(ch // LANE)], axis=1)
    xb = x.astype(BF16)
    for j, piece in enumerate(tiles(_dot(xb, wst_ref[...]))):
        inj_ref[j] = piece
    n_tiles = inj_ref.shape[0]
    a_c = al_ref[0:1, :]
    a_s = al_ref[1:2, :]
    state = state_ref[...]
    for c in range(tc):
        rows = pl.ds(c, nb, stride=tc)
        for j, piece in enumerate(tiles(state)):
            prev_ref[j, rows, :] = piece
        inj = jnp.concatenate([inj_ref[j, rows, :] for j in range(n_tiles)], axis=1)
        state = state * a_c + pltpu.roll(state, gp, axis=1) * a_s + inj
    state_ref[...] = state
    prev = jnp.concatenate([prev_ref[j] for j in range(n_tiles)], axis=1)
    y = _dot(xb, toep_ref[...]) + _dot_nt_bf(prev, wout_ref[...])
    y = jax.nn.gelu(y + d_ref[...] * x)
    y = y * jax.nn.sigmoid(_dot(y.astype(BF16), glu_ref[...]) + b_ref[...])
    for n, piece in enumerate(tiles(y)):
        tok_ref[n % (ch // LANE), pl.ds(n // (ch // LANE), nb * tc, stride=S5_SUB), :] = piece
    o_ref[...] = jnp.concatenate([tok_ref[j] for j in range(ch // LANE)], axis=1).reshape(nb, toks, ch)


def _s5(u, batch, seq, prm):
    gp = S5_GROUPS * S5_STATE
    flat = S5_SUB * u.shape[1]
    rows = seq // S5_SUB
    tc = next(c for c in (32, 16, 8) if rows % c == 0)
    toep, wst, wout, al, glu = _s5_prep(prm["lam_re"], prm["lam_im"], prm["log_dt"], prm["b_re"], prm["b_im"],
                                        prm["c_re"], prm["c_im"], prm["w_glu"])
    ch = u.shape[1]
    tile = pl.BlockSpec((batch, tc * S5_SUB, ch), lambda i: (0, i, 0))
    y = pl.pallas_call(
        _s5_main_kernel,
        out_shape=jax.ShapeDtypeStruct((batch, seq, ch), F32),
        grid=(rows // tc,),
        in_specs=[tile, _resident((flat, flat)), _resident((flat, 2 * gp)), _resident((flat, 2 * gp)),
                  _resident((8, 2 * gp)), _resident((flat, flat)), _resident((1, flat)), _resident((1, flat))],
        out_specs=tile,
        scratch_shapes=[pltpu.VMEM((ch // LANE, batch * tc * S5_SUB, LANE), F32),
                        pltpu.VMEM((2 * gp // LANE, batch * tc, LANE), F32),
                        pltpu.VMEM((2 * gp // LANE, batch * tc, LANE), F32),
                        pltpu.VMEM((batch, 2 * gp), F32)],
        compiler_params=_cparams("arbitrary"),
        name="s5_main",
    )(u.reshape(batch, seq, ch), toep, wst, wout, al, glu,
      jnp.tile(prm["d"], (1, S5_SUB)), jnp.tile(prm["b_glu"], (1, S5_SUB)))
    return y.reshape(batch * seq, ch)


def _moba_prep_kernel(q_ref, k_ref, v_ref, pos_ref, inv_ref, qo_ref, ko_ref, vt_ref, sel_ref, km_ref):
    n = pl.program_id(1)
    nbp = km_ref.shape[1]

    @pl.when(n == 0)
    def _():
        km_ref[...] = jnp.zeros(km_ref.shape, F32)

    half = HEAD_DIM // 2
    ang = inv_ref[...] * pos_ref[0, 0]
    cos_t, sin_t = jnp.cos(ang), jnp.sin(ang)
    pairs_per_slab = LANE // HEAD_DIM
    cos = jnp.concatenate([cos_t, cos_t] * pairs_per_slab, axis=0).T
    sin = jnp.concatenate([-sin_t, sin_t] * pairs_per_slab, axis=0).T
    lane = _iota((MOBA_BLOCK, LANE), 1)
    low_half = (lane % HEAD_DIM) < half

    def rope(x):
        swapped = jnp.where(low_half, pltpu.roll(x, LANE - half, axis=1), pltpu.roll(x, half, axis=1))
        return x * cos + swapped * sin

    blk_row = _iota((nbp, MOBA_BLOCK), 0)
    for slab in range(BRANCH // LANE):
        q = rope(q_ref[0, :, slab * LANE:(slab + 1) * LANE])
        k = rope(k_ref[0, :, slab * LANE:(slab + 1) * LANE])
        qo_ref[0, 0, slab] = (q * (HEAD_DIM ** -0.5)).astype(BF16)
        k_mean = jnp.mean(k, axis=0, keepdims=True)
        for sub in range(pairs_per_slab):
            h = slab * pairs_per_slab + sub
            own = (lane // HEAD_DIM) == sub
            gate = _dot_nt_hi(km_ref[h], q)
            rank = jnp.zeros((nbp, MOBA_BLOCK), jnp.int32)
            for m in range(nbp):
                gm = gate[m:m + 1, :]
                beats = (gm > gate) | ((gm == gate) & (m < blk_row))
                rank = rank + jnp.where(beats & (m < n), 1, 0)
            sel_ref[0, 0, h] = ((blk_row < n) & (rank < MOBA_TOPK)).astype(F32)
            ko_ref[0, 0, h] = jnp.where(own, k, 0.0).astype(BF16)
            km_ref[h, pl.ds(n, 1), :] = jnp.where(own[0:1], k_mean, 0.0)
    v_t = v_ref[0].T
    tail = (_iota((VT_ROWS - HEAD_DIM, MOBA_BLOCK), 0) == 0).astype(F32)
    for h in range(HEADS):
        vt_ref[0, 0, h] = jnp.concatenate([v_t[h * HEAD_DIM:(h + 1) * HEAD_DIM], tail], axis=0).astype(BF16)


def _moba_prep(qkv, pos, inv):
    b, s, _ = qkv.shape
    nb = s // MOBA_BLOCK
    nbp = -(-nb // 8) * 8
    slabs = BRANCH // LANE
    blk = lambda j: pl.BlockSpec((1, MOBA_BLOCK, BRANCH), lambda i, n, j=j: (i, n, j))
    return pl.pallas_call(
        _moba_prep_kernel,
        out_shape=[
            jax.ShapeDtypeStruct((b, nb, slabs, MOBA_BLOCK, LANE), BF16),
            jax.ShapeDtypeStruct((b, nb, HEADS, MOBA_BLOCK, LANE), BF16),
            jax.ShapeDtypeStruct((b, nb, HEADS, VT_ROWS, MOBA_BLOCK), BF16),
            jax.ShapeDtypeStruct((b, nb, HEADS, nbp, MOBA_BLOCK), F32),
        ],
        grid=(b, nb),
        in_specs=[blk(0), blk(1), blk(2),
                  pl.BlockSpec((1, 1, 1, MOBA_BLOCK), lambda i, n: (i, n, 0, 0)),
                  pl.BlockSpec((HEAD_DIM // 2, 1), lambda i, n: (0, 0))],
        out_specs=[pl.BlockSpec((1, 1, slabs, MOBA_BLOCK, LANE), lambda i, n: (i, n, 0, 0, 0)),
                   pl.BlockSpec((1, 1, HEADS, MOBA_BLOCK, LANE), lambda i, n: (i, n, 0, 0, 0)),
                   pl.BlockSpec((1, 1, HEADS, VT_ROWS, MOBA_BLOCK), lambda i, n: (i, n, 0, 0, 0)),
                   pl.BlockSpec((1, 1, HEADS, nbp, MOBA_BLOCK), lambda i, n: (i, n, 0, 0, 0))],
        scratch_shapes=[pltpu.VMEM((HEADS, nbp, LANE), F32)],
        compiler_params=_cparams("parallel", "arbitrary"),
        name="moba_prep",
    )(qkv, qkv, qkv, pos, inv)


def _moba_attn_kernel(q_ref, k_ref, vt_ref, sel_ref, o_ref):
    i = pl.program_id(1)
    blk = MOBA_BLOCK
    nbp = sel_ref.shape[3]
    key_i = _iota((blk, blk), 0)
    qry_i = _iota((blk, blk), 1)
    sel_row = _iota((nbp, blk), 0)

    def update(blocks, carries, masks):
        scores = [[lax.dot_general(k_ref[0, n, h], q_ref[0, 0, h * HEAD_DIM // LANE], NT_DIMS,
                                   preferred_element_type=F32) for h in range(HEADS)] for n in blocks]
        stats = []
        for h in range(HEADS):
            m_run = carries[h][0]
            s = [jnp.where(masks[b][h], scores[b][h], -jnp.inf) for b in range(len(blocks))]
            m_new = m_run
            for s_b in s:
                m_new = jnp.maximum(m_new, jnp.max(s_b, axis=0, keepdims=True))
            m_safe = jnp.where(m_new == -jnp.inf, 0.0, m_new)
            p = jnp.concatenate([jnp.exp(s_b - m_safe).astype(BF16) for s_b in s], axis=0)
            stats.append((m_new, jnp.exp(m_run - m_safe), p))
        return tuple((stats[h][0], stats[h][1] * carries[h][1]
                      + _dot(jnp.concatenate([vt_ref[0, n, h] for n in blocks], axis=1), stats[h][2]))
                     for h in range(HEADS))

    def picked(n):
        return [jnp.max(jnp.where(sel_row == n, sel_ref[0, 0, h], 0.0), axis=0, keepdims=True) > 0.0
                for h in range(HEADS)]

    def body(t, carries):
        return update([2 * t, 2 * t + 1], carries, [picked(2 * t), picked(2 * t + 1)])

    init = (jnp.full((1, blk), -jnp.inf, F32), jnp.zeros((VT_ROWS, blk), F32))
    carries = lax.fori_loop(0, i // 2, body, (init,) * HEADS)
    causal = [key_i <= qry_i] * HEADS
    carries = lax.cond((i % 2) == 1,
                       lambda c: update([i - 1, i], c, [picked(i - 1), causal]),
                       lambda c: update([i], c, [causal]), carries)
    o_ref[0] = jnp.concatenate([acc[0:HEAD_DIM] / acc[HEAD_DIM:HEAD_DIM + 1] for _, acc in carries], axis=0).T


def _moba_attn(q, k, vt, sel):
    b, nb = q.shape[:2]
    nbp = sel.shape[3]
    return pl.pallas_call(
        _moba_attn_kernel,
        out_shape=jax.ShapeDtypeStruct((b, nb * MOBA_BLOCK, BRANCH), F32),
        grid=(b, nb),
        in_specs=[
            pl.BlockSpec((1, 1, BRANCH // LANE, MOBA_BLOCK, LANE), lambda i, n: (i, n, 0, 0, 0)),
            pl.BlockSpec((1, nb, HEADS, MOBA_BLOCK, LANE), lambda i, n: (i, 0, 0, 0, 0)),
            pl.BlockSpec((1, nb, HEADS, VT_ROWS, MOBA_BLOCK), lambda i, n: (i, 0, 0, 0, 0)),
            pl.BlockSpec((1, 1, HEADS, nbp, MOBA_BLOCK), lambda i, n: (i, n, 0, 0, 0)),
        ],
        out_specs=pl.BlockSpec((1, MOBA_BLOCK, BRANCH), lambda i, n: (i, n, 0)),
        compiler_params=_cparams("parallel", "arbitrary"),
        name="moba_attn",
    )(q, k, vt, sel)


def _moba(qkv, positions):
    b, s, _ = qkv.shape
    inv = ROPE_THETA ** (-jnp.arange(0, HEAD_DIM, 2, dtype=F32) / HEAD_DIM)
    pos = positions.astype(F32).reshape(b, s // MOBA_BLOCK, 1, MOBA_BLOCK)
    return _moba_attn(*_moba_prep(qkv, pos, inv[:, None]))


def _permute_w_in(w_in):
    cuts = [0]
    for width in (3 * BRANCH, BRANCH, HEADS, HEADS, 2 * BRANCH, BRANCH, BRANCH, HEADS, HEADS, BRANCH, 3 * BRANCH):
        cuts.append(cuts[-1] + width)
    seg = [w_in[:, cuts[n]:cuts[n + 1]] for n in range(len(cuts) - 1)]
    gdn_qkv, gdn_z, gdn_b, gdn_a, ml_qk, ml_v, ml_o, ml_i, ml_f, s5_u, moba_qkv = seg
    pad = jnp.zeros((w_in.shape[0], SMALL_W - 4 * HEADS), w_in.dtype)
    return jnp.concatenate([gdn_qkv, gdn_z, ml_qk, ml_v, ml_o, s5_u, moba_qkv, gdn_b, gdn_a, ml_i, ml_f, pad], axis=1)


def _token_tile(t, largest=512):
    for tm in (1024, 512, 256, 128, 64, 32, 16, 8):
        if tm > largest:
            continue
        if t % tm == 0:
            return tm
    raise ValueError(f"token count {t} is not a multiple of 8")


def _chunks_per_step(seq, most=8):
    n_chunks = seq // CHUNK
    for cps in (16, 8, 4, 2, 1):
        if cps <= most and n_chunks % cps == 0:
            return cps


def _ff_slice(f):
    for sw in (2 * LANE, LANE):
        if f % sw == 0:
            return sw
    raise ValueError(f"ffn width {f} is not a multiple of {LANE}")


def kernel(x, p, positions, ffn1_norm, ffn1_w_gu, ffn1_w_down, mix_norm, w_in, gdn_conv, gdn_a_log, gdn_dt_bias, gdn_norm, mlstm_conv, mlstm_i_bias, mlstm_f_bias, mlstm_norm, s5_lambda_re, s5_lambda_im, s5_b_re, s5_b_im, s5_c_re, s5_c_im, s5_d, s5_log_dt, s5_w_glu, s5_b_glu, w_gate, w_branch, w_out, ffn2_norm, ffn2_w_gu, ffn2_w_down, ple_norm, ple_w_proj, ple_w_gate, final_norm):
    batch, seq, d = x.shape
    depth = p.shape[0]
    t = batch * seq
    tm = _token_tile(t)
    tm_ffn = _token_tile(t, largest=1024)
    tf = _ff_slice(ffn1_w_down.shape[1])
    cps = _chunks_per_step(seq)
    bf = lambda a: a.astype(BF16)
    row = lambda a: a.reshape(1, -1)
    seq3 = lambda a: a.reshape(batch, seq, a.shape[-1])

    h = x.reshape(t, d)
    for i in range(depth):
        h = _ffn(h, row(ffn1_norm[i]), bf(ffn1_w_gu[i]), bf(ffn1_w_down[i]), tm_ffn, tf)

        gdn_qkv, gdn_z, ml_qk, ml_v, ml_o, s5_u, moba_qkv, small = _inproj(
            h, row(mix_norm[i]), bf(_permute_w_in(w_in[i])), tm)
        y_gdn = _gdn(seq3(gdn_qkv), seq3(gdn_z), seq3(small), gdn_conv[i], row(gdn_a_log[i]),
                     row(gdn_dt_bias[i]), row(gdn_norm[i]), cps)
        y_mlstm = _mlstm(seq3(ml_qk), seq3(ml_v), seq3(ml_o), seq3(small), mlstm_conv[i],
                         row(mlstm_i_bias[i]), row(mlstm_f_bias[i]), row(mlstm_norm[i]),
                         _chunks_per_step(seq, most=16))
        s5_prm = dict(lam_re=s5_lambda_re[i], lam_im=s5_lambda_im[i], log_dt=s5_log_dt[i],
                      b_re=s5_b_re[i], b_im=s5_b_im[i], c_re=s5_c_re[i], c_im=s5_c_im[i],
                      d=row(s5_d[i]), w_glu=s5_w_glu[i], b_glu=row(s5_b_glu[i]))
        y_s5 = _s5(s5_u, batch, seq, s5_prm)
        y_moba = _moba(seq3(moba_qkv), positions)
        ys = (y_gdn.reshape(t, BRANCH), y_mlstm.reshape(t, BRANCH), y_s5, y_moba.reshape(t, BRANCH))
        h = _merge(h, row(mix_norm[i]), ys, bf(w_gate[i]), bf(w_branch[i]), bf(w_out[i]), tm)

        ple = (row(ple_norm[i]), p[i].reshape(t, -1), bf(ple_w_proj[i]), bf(ple_w_gate[i]), row(final_norm))
        h = _ffn(h, row(ffn2_norm[i]), bf(ffn2_w_gu[i]), bf(ffn2_w_down[i]), tm_ffn, tf, ple, i == depth - 1)
    return h.reshape(batch, seq, d)
```

```python
import functools
import math

import jax
import jax.numpy as jnp
from jax import lax
from jax.experimental import pallas as pl
from jax.experimental.pallas import tpu as pltpu

F32 = jnp.float32
BF16 = jnp.bfloat16
HIGHEST = lax.Precision.HIGHEST

NORM_EPS = 1e-6
HEADS = 4
HEAD_DIM = 64
BRANCH = HEADS * HEAD_DIM
CHUNK = 64
CONV_WIDTH = 4
CONV_HIST = 8
GDN_GROUP = 4
S5_GROUPS = 16
S5_CH = 16
S5_STATE = 64
S5_SUB = 4
MOBA_BLOCK = 256
MOBA_TOPK = 3
VT_ROWS = HEAD_DIM + 16
ROPE_THETA = 10000.0
LANE = 128
SMALL_W = LANE
V7X_VMEM_LIMIT = 56 * 1024 * 1024

NT_DIMS = (((1,), (1,)), ((), ()))


def _cparams(*sem):
    return pltpu.CompilerParams(dimension_semantics=sem, vmem_limit_bytes=V7X_VMEM_LIMIT)


def _rms(x, w):
    return x * lax.rsqrt(jnp.mean(x * x, axis=-1, keepdims=True) + NORM_EPS) * w


def _silu(x):
    return x * jax.nn.sigmoid(x)


def _softplus(x):
    return jnp.maximum(x, 0.0) + jnp.log1p(jnp.exp(-jnp.abs(x)))


def _dot(a, b):
    return jnp.dot(a, b, preferred_element_type=F32)


def _dot_bf(a, b):
    return jnp.dot(a.astype(BF16), b.astype(BF16), preferred_element_type=F32)


def _dot_nt_bf(a, b):
    return lax.dot_general(a.astype(BF16), b.astype(BF16), NT_DIMS, preferred_element_type=F32)


def _dot_nt_hi(a, b):
    return lax.dot_general(a, b, NT_DIMS, precision=HIGHEST, preferred_element_type=F32)


def _split_bf(a, parts):
    out = []
    for _ in range(parts):
        piece = a.astype(BF16)
        out.append(piece)
        a = a - piece.astype(F32)
    return out


def _dot_x3(a, b):
    a_hi, a_lo = _split_bf(a, 2)
    b_hi, b_lo = _split_bf(b, 2)
    return _dot(jnp.concatenate([a_hi, a_hi, a_lo], axis=1), jnp.concatenate([b_hi, b_lo, b_hi], axis=0))


def _dot_exact01(a01_x3, b):
    return _dot(a01_x3, jnp.concatenate(_split_bf(b, 3), axis=0))


def _iota(shape, dim):
    return lax.broadcasted_iota(jnp.int32, shape, dim)


def _head_block_ones():
    r = _iota((BRANCH, BRANCH), 0) // HEAD_DIM
    c = _iota((BRANCH, BRANCH), 1) // HEAD_DIM
    return (r == c).astype(BF16)


def _head_sums(x, ones_bd, parts):
    pieces = _split_bf(x, parts)
    out = _dot(pieces[0], ones_bd)
    for piece in pieces[1:]:
        out = out + _dot(piece, ones_bd)
    return out


def _head_bcast(cols):
    rows = cols.shape[0]
    head = _iota((rows, BRANCH), 1) // HEAD_DIM
    out = jnp.broadcast_to(cols[:, 0:1], (rows, BRANCH))
    for h in range(1, HEADS):
        out = jnp.where(head == h, jnp.broadcast_to(cols[:, h:h + 1], (rows, BRANCH)), out)
    return out


def _causal_conv_silu(x_ref, w_ref, pad_ref, first):
    rows = x_ref.shape[1]

    @pl.when(first)
    def _():
        pad_ref[0:CONV_HIST, :] = jnp.zeros((CONV_HIST, pad_ref.shape[1]), F32)

    pad_ref[CONV_HIST:CONV_HIST + rows, :] = x_ref[0]
    w = w_ref[...]
    acc = w[CONV_WIDTH - 1:CONV_WIDTH, :] * pad_ref[CONV_HIST:CONV_HIST + rows, :]
    for back in range(1, CONV_WIDTH):
        tap = CONV_WIDTH - 1 - back
        acc = acc + w[tap:tap + 1, :] * pad_ref[CONV_HIST - back:CONV_HIST - back + rows, :]
    pad_ref[0:CONV_HIST, :] = pad_ref[rows:rows + CONV_HIST, :]
    return _silu(acc)


def _tri_consts():
    r = _iota((3 * CHUNK, 3 * CHUNK), 0)
    k = _iota((3 * CHUNK, 3 * CHUNK), 1) % CHUNK
    i = r % CHUNK
    part = r // CHUNK
    return (((part == 0) & (k <= i)) | ((part == 1) & (k > i)) | (part == 2)).astype(BF16)


def _resident(shape):
    return pl.BlockSpec(shape, lambda *_: (0,) * len(shape), pipeline_mode=pl.Buffered(1))


def _ffn_kernel(x_ref, nw_ref, wgu_ref, wd_ref, *rest, sw, ple, final):
    o_ref = rest[-1]
    x = x_ref[...]
    xn = _rms(x, nw_ref[...]).astype(BF16)
    f = wd_ref.shape[0]

    def gate_up(j):
        return (_dot(xn, wgu_ref[:, j * sw:(j + 1) * sw]), _dot(xn, wgu_ref[:, f + j * sw:f + (j + 1) * sw]))

    ahead = gate_up(0)
    acc = None
    for j in range(f // sw):
        gate, up = ahead
        if (j + 1) * sw < f:
            ahead = gate_up(j + 1)
        part = _dot((_silu(gate) * up).astype(BF16), wd_ref[j * sw:(j + 1) * sw, :])
        acc = part if acc is None else acc + part
    out = x + 0.5 * acc
    if ple:
        pn_ref, p_ref, wp_ref, wg_ref, fw_ref = rest[:-1]
        gate = jax.nn.sigmoid(_dot(_rms(out, pn_ref[...]).astype(BF16), wg_ref[...]))
        out = out + _dot(p_ref[...].astype(BF16), wp_ref[...]) * gate
        if final:
            out = _rms(out, fw_ref[...])
    o_ref[...] = out


def _ffn(h, norm_w, w_gu, w_down, tm, sw, ple=None, final=False):
    t, d = h.shape
    f = w_down.shape[0]
    operands = [h, norm_w, w_gu, w_down]
    in_specs = [pl.BlockSpec((tm, d), lambda i: (i, 0)), _resident((1, d)), _resident((d, 2 * f)), _resident((f, d))]
    if ple is not None:
        pd = ple[1].shape[1]
        operands += list(ple)
        in_specs += [_resident((1, d)), pl.BlockSpec((tm, pd), lambda i: (i, 0)), _resident((pd, d)),
                     _resident((d, d)), _resident((1, d))]
    return pl.pallas_call(
        functools.partial(_ffn_kernel, sw=sw, ple=ple is not None, final=final),
        out_shape=jax.ShapeDtypeStruct((t, d), F32),
        grid=(t // tm,),
        in_specs=in_specs,
        out_specs=pl.BlockSpec((tm, d), lambda i: (i, 0)),
        compiler_params=_cparams("parallel"),
        name="ffn",
    )(*operands)


IN_SEGMENTS = (3 * BRANCH, BRANCH, 2 * BRANCH, BRANCH, BRANCH, BRANCH, 3 * BRANCH, SMALL_W)


def _inproj_kernel(x_ref, nw_ref, w_ref, *o_refs):
    xn = _rms(x_ref[...], nw_ref[...]).astype(BF16)
    start = 0
    for o_ref, width in zip(o_refs, IN_SEGMENTS):
        o_ref[...] = _dot(xn, w_ref[:, start:start + width])
        start += width


def _inproj(h, norm_w, w_perm, tm):
    t, d = h.shape
    n = w_perm.shape[1]
    return pl.pallas_call(
        _inproj_kernel,
        out_shape=[jax.ShapeDtypeStruct((t, w), F32) for w in IN_SEGMENTS],
        grid=(t // tm,),
        in_specs=[
            pl.BlockSpec((tm, d), lambda i: (i, 0)),
            pl.BlockSpec((1, d), lambda i: (0, 0)),
            pl.BlockSpec((d, n), lambda i: (0, 0)),
        ],
        out_specs=[pl.BlockSpec((tm, w), lambda i: (i, 0)) for w in IN_SEGMENTS],
        compiler_params=_cparams("parallel"),
        name="inproj",
    )(h, norm_w, w_perm)


def _merge_kernel(h_ref, nw_ref, y0_ref, y1_ref, y2_ref, y3_ref, wg_ref, wb_ref, wo_ref, o_ref):
    h = h_ref[...]
    u = _rms(h, nw_ref[...]).astype(BF16)
    merged = None
    for b, y_ref in enumerate((y0_ref, y1_ref, y2_ref, y3_ref)):
        term = jax.nn.sigmoid(_dot(u, wg_ref[b])) * _dot(y_ref[...].astype(BF16), wb_ref[b])
        merged = term if merged is None else merged + term
    o_ref[...] = h + _dot(merged.astype(BF16), wo_ref[...])


def _merge(h, norm_w, ys, w_gate, w_branch, w_out, tm):
    t, d = h.shape
    nb, bw, _ = w_branch.shape
    return pl.pallas_call(
        _merge_kernel,
        out_shape=jax.ShapeDtypeStruct((t, d), F32),
        grid=(t // tm,),
        in_specs=[
            pl.BlockSpec((tm, d), lambda i: (i, 0)),
            pl.BlockSpec((1, d), lambda i: (0, 0)),
        ] + [pl.BlockSpec((tm, bw), lambda i: (i, 0)) for _ in range(nb)] + [
            pl.BlockSpec((nb, d, d), lambda i: (0, 0, 0)),
            pl.BlockSpec((nb, bw, d), lambda i: (0, 0, 0)),
            pl.BlockSpec((d, d), lambda i: (0, 0)),
        ],
        out_specs=pl.BlockSpec((tm, d), lambda i: (i, 0)),
        compiler_params=_cparams("parallel"),
        name="merge",
    )(h, norm_w, *ys, w_gate, w_branch, w_out)


def _gdn_kernel(qkv_ref, z_ref, sm_ref, cw_ref, alog_ref, dtb_ref, nw_ref, o_ref, pad_ref, state_ref, *, cps):
    step = pl.program_id(1)

    @pl.when(step == 0)
    def _():
        state_ref[...] = jnp.zeros(state_ref.shape, F32)

    x_all = _causal_conv_silu(qkv_ref, cw_ref, pad_ref, step == 0)
    ones_bd = _head_block_ones()
    tri = _tri_consts()
    eye = (_iota((CHUNK, CHUNK), 0) == _iota((CHUNK, CHUNK), 1)).astype(BF16)
    row_c = _iota((CHUNK, BRANCH), 0)
    col_c = _iota((CHUNK, BRANCH), 1) % HEAD_DIM
    gcs = GDN_GROUP if cps % GDN_GROUP == 0 else cps
    rows = gcs * CHUNK
    row = _iota((rows, BRANCH), 0) % CHUNK
    col = _iota((rows, BRANCH), 1) % HEAD_DIM
    q_eff, out_loc, inject, mix, e_tot = {}, {}, {}, {}, {}
    for grp in range(cps // gcs):
        x = x_all[grp * rows:(grp + 1) * rows]
        sm = sm_ref[0, grp * rows:(grp + 1) * rows, :]
        beta = _head_bcast(jax.nn.sigmoid(sm[:, 0:HEADS]))
        g = _head_bcast(-jnp.exp(alog_ref[...]) * _softplus(sm[:, HEADS:2 * HEADS] + dtb_ref[...]))
        q, k, v = x[:, 0:BRANCH], x[:, BRANCH:2 * BRANCH], x[:, 2 * BRANCH:3 * BRANCH]
        q = q * lax.rsqrt(_head_sums(q * q, ones_bd, 2) + NORM_EPS) * (HEAD_DIM ** -0.5)
        k = k * lax.rsqrt(_head_sums(k * k, ones_bd, 2) + NORM_EPS)
        sums = []
        for ci in range(gcs):
            gb = g[ci * CHUNK:(ci + 1) * CHUNK]
            sums.append(_dot_exact01(tri, jnp.concatenate([gb, jnp.where(row_c > col_c, gb, 0.0)], axis=1)))
        stack = lambda r0, c0: jnp.concatenate([s[r0:r0 + CHUNK, c0:c0 + BRANCH] for s in sums], axis=0)
        e_gc = jnp.exp(stack(0, 0))
        e_rest = jnp.exp(stack(CHUNK, 0))
        e_total = jnp.exp(stack(2 * CHUNK, 0))
        decay = jnp.where(row >= col, jnp.exp(stack(0, BRANCH)), 0.0)
        k_beta = k * beta
        q_dec = q * e_gc
        k_dec = (k * e_rest).astype(BF16)
        rhs_u = v * beta
        rhs_w = k_beta * e_gc

        probs = [(ci, h) for ci in range(gcs) for h in range(HEADS)]

        def cut(t, p):
            return t[p[0] * CHUNK:(p[0] + 1) * CHUNK, p[1] * HEAD_DIM:(p[1] + 1) * HEAD_DIM]

        gram = {p: _dot_nt_bf(jnp.concatenate([cut(k_beta, p), cut(q, p)], axis=0), cut(k, p)) for p in probs}
        k_dec_t = {p: lax.dot_general(eye, cut(k_dec, p), NT_DIMS, preferred_element_type=F32) for p in probs}
        power, sol, qk = {}, {}, {}
        for p in probs:
            dec = cut(decay, p)
            power[p] = jnp.where(cut(row > col, p), gram[p][0:CHUNK] * dec, 0.0)
            qk[p] = gram[p][CHUNK:2 * CHUNK] * dec
            sol[p] = jnp.concatenate([cut(rhs_u, p), cut(rhs_w, p)], axis=1)
        n_factors = int(math.log2(CHUNK))
        for j in range(n_factors):
            last = j == n_factors - 1
            prods = {p: _dot_x3(power[p], sol[p] if last else jnp.concatenate([sol[p], power[p]], axis=1))
                     for p in probs}
            for p in probs:
                term = prods[p][:, 0:2 * HEAD_DIM]
                sol[p] = sol[p] - term if j == 0 else sol[p] + term
                if not last:
                    power[p] = prods[p][:, 2 * HEAD_DIM:3 * HEAD_DIM]
        both = {p: _dot_bf(jnp.concatenate([qk[p], k_dec_t[p]], axis=0), sol[p]) for p in probs}
        for p in probs:
            key = (grp * gcs + p[0], p[1])
            out_loc[key] = both[p][0:CHUNK, 0:HEAD_DIM]
            q_eff[key] = cut(q_dec, p) - both[p][0:CHUNK, HEAD_DIM:2 * HEAD_DIM]
            inject[key] = both[p][CHUNK:2 * CHUNK, 0:HEAD_DIM]
            mix[key] = both[p][CHUNK:2 * CHUNK, HEAD_DIM:2 * HEAD_DIM]
            e_tot[key] = cut(e_total, p)
    state = [state_ref[h] for h in range(HEADS)]
    entering, from_state = {}, {}
    for ci in range(cps):
        mixed = [_dot_bf(mix[ci, h], state[h]) for h in range(HEADS)]
        if ci:
            from_state.update({(ci - 1, h): _dot_bf(q_eff[ci - 1, h], entering[ci - 1, h]) for h in range(HEADS)})
        for h in range(HEADS):
            entering[ci, h] = state[h]
            state[h] = state[h] * e_tot[ci, h] - mixed[h] + inject[ci, h]
    from_state.update({(cps - 1, h): _dot_bf(q_eff[cps - 1, h], entering[cps - 1, h]) for h in range(HEADS)})
    for h in range(HEADS):
        state_ref[h] = state[h]
    out = jnp.concatenate([jnp.concatenate([from_state[ci, h] + out_loc[ci, h] for h in range(HEADS)], axis=1)
                           for ci in range(cps)], axis=0)
    mean_sq = _head_sums(out * out, ones_bd, 2) * (1.0 / HEAD_DIM)
    nw = jnp.concatenate([nw_ref[...]] * HEADS, axis=1)
    o_ref[0] = out * lax.rsqrt(mean_sq + NORM_EPS) * nw * _silu(z_ref[0])


def _gdn(qkv, z, small, conv_w, a_log, dt_bias, norm_w, cps):
    b, s, _ = qkv.shape
    rows = cps * CHUNK
    return pl.pallas_call(
        functools.partial(_gdn_kernel, cps=cps),
        out_shape=jax.ShapeDtypeStruct((b, s, BRANCH), F32),
        grid=(b, s // rows),
        in_specs=[
            pl.BlockSpec((1, rows, 3 * BRANCH), lambda i, c: (i, c, 0)),
            pl.BlockSpec((1, rows, BRANCH), lambda i, c: (i, c, 0)),
            pl.BlockSpec((1, rows, SMALL_W), lambda i, c: (i, c, 0)),
            pl.BlockSpec((CONV_WIDTH, 3 * BRANCH), lambda i, c: (0, 0)),
            pl.BlockSpec((1, HEADS), lambda i, c: (0, 0)),
            pl.BlockSpec((1, HEADS), lambda i, c: (0, 0)),
            pl.BlockSpec((1, HEAD_DIM), lambda i, c: (0, 0)),
        ],
        out_specs=pl.BlockSpec((1, rows, BRANCH), lambda i, c: (i, c, 0)),
        scratch_shapes=[
            pltpu.VMEM((CONV_HIST + rows, 3 * BRANCH), F32),
            pltpu.VMEM((HEADS, HEAD_DIM, HEAD_DIM), F32),
        ],
        compiler_params=_cparams("parallel", "arbitrary"),
        name="gdn",
    )(qkv, z, small, conv_w, a_log, dt_bias, norm_w)


def _mlstm_kernel(qk_ref, v_ref, o_ref_in, sm_ref, cw_ref, ib_ref, fb_ref, nw_ref, out_ref,
                  pad_ref, c_ref, n_ref, m_ref, *, cps):
    step = pl.program_id(1)

    @pl.when(step == 0)
    def _():
        c_ref[...] = jnp.zeros(c_ref.shape, F32)
        n_ref[...] = jnp.zeros(n_ref.shape, F32)
        m_ref[...] = jnp.zeros(m_ref.shape, F32)

    rows = cps * CHUNK
    x = _causal_conv_silu(qk_ref, cw_ref, pad_ref, step == 0)
    q = x[:, 0:BRANCH]
    k = x[:, BRANCH:2 * BRANCH] * (HEAD_DIM ** -0.5)
    v = v_ref[0]
    sm = sm_ref[0]
    log_i = _head_bcast(sm[:, 2 * HEADS:3 * HEADS] + ib_ref[...])
    log_f = _head_bcast(-_softplus(-(sm[:, 3 * HEADS:4 * HEADS] + fb_ref[...])))
    ones_bd = _head_block_ones()

    row_c = _iota((CHUNK, BRANCH), 0)
    col_c = _iota((CHUNK, BRANCH), 1) % HEAD_DIM
    tri = _tri_consts()
    sums = []
    for ci in range(cps):
        lf = log_f[ci * CHUNK:(ci + 1) * CHUNK]
        li = log_i[ci * CHUNK:(ci + 1) * CHUNK]
        sums.append(_dot_exact01(tri, jnp.concatenate(
            [lf, jnp.where(row_c > col_c, lf, 0.0), jnp.where(row_c == col_c, li, 0.0)], axis=1)))
    stack = lambda r0, c0: jnp.concatenate([s[r0:r0 + CHUNK, c0:c0 + BRANCH] for s in sums], axis=0)
    row = _iota((rows, BRANCH), 0) % CHUNK
    col = _iota((rows, BRANCH), 1) % HEAD_DIM
    b_col = stack(0, 0)
    b_last = stack(2 * CHUNK, 0)
    a_key = stack(CHUNK, 0) + log_i
    d_intra = jnp.where(row >= col, stack(0, BRANCH) + stack(2 * CHUNK, 2 * BRANCH), -jnp.inf)
    run = log_i - b_col
    shift = 1
    while shift < CHUNK:
        run = jnp.where(row >= shift, jnp.maximum(run, pltpu.roll(run, shift, axis=0)), run)
        shift *= 2
    m_mem = m_ref[...]
    m_prev, m_next, scale = [], [], []
    for ci in range(cps):
        last = ci * CHUNK + CHUNK - 1
        g_c = b_last[last:last + 1]
        m_new = jnp.maximum(g_c + m_mem, g_c + run[last:last + 1])
        m_prev.append(m_mem)
        m_next.append(m_new)
        scale.append(jnp.exp(g_c + m_mem - m_new))
        m_mem = m_new
    m_ref[...] = m_mem
    per_chunk = lambda rows_: jnp.concatenate([jnp.broadcast_to(r, (CHUNK, BRANCH)) for r in rows_], axis=0)
    m_inter = b_col + per_chunk(m_prev)
    m_t = jnp.maximum(m_inter, b_col + run)
    w_inter = jnp.exp(m_inter - m_t)
    e_intra = jnp.exp(d_intra - m_t)
    k_w = k * jnp.exp(a_key - per_chunk(m_next))
    k_w_bf = k_w.astype(BF16)

    probs = [(ci, h) for ci in range(cps) for h in range(HEADS)]

    def cut(t, p):
        return t[p[0] * CHUNK:(p[0] + 1) * CHUNK, p[1] * HEAD_DIM:(p[1] + 1) * HEAD_DIM]

    def paste(pieces):
        return jnp.concatenate([jnp.concatenate([pieces[ci, h] for h in range(HEADS)], axis=1)
                                for ci in range(cps)], axis=0)

    eye = (_iota((CHUNK, CHUNK), 0) == _iota((CHUNK, CHUNK), 1)).astype(BF16)
    s_qk = {p: _dot_nt_bf(cut(q, p), cut(k, p)) * cut(e_intra, p) for p in probs}
    k_w_t = {p: lax.dot_general(eye, cut(k_w_bf, p), NT_DIMS, preferred_element_type=F32) for p in probs}
    intra = {p: _dot_bf(s_qk[p], cut(v, p)) for p in probs}
    inject = {p: _dot_bf(k_w_t[p], cut(v, p)) for p in probs}
    c_mem = [c_ref[h] for h in range(HEADS)]
    n_mem = n_ref[...]
    c_prev, n_prev = {}, []
    for ci in range(cps):
        n_prev.append(n_mem)
        n_mem = n_mem * scale[ci] + jnp.sum(k_w[ci * CHUNK:(ci + 1) * CHUNK], axis=0, keepdims=True)
        for h in range(HEADS):
            c_prev[ci, h] = c_mem[h]
            c_mem[h] = c_mem[h] * scale[ci][:, h * HEAD_DIM:(h + 1) * HEAD_DIM] + inject[ci, h]
    for h in range(HEADS):
        c_ref[h] = c_mem[h]
    n_ref[...] = n_mem
    inter = {p: _dot_bf(cut(q, p), c_prev[p]) for p in probs}
    num = w_inter * paste(inter) + paste(intra)
    qn = w_inter * _head_sums(q * per_chunk(n_prev), ones_bd, 3) + _head_sums(paste(s_qk), ones_bd, 3)
    h_tilde = num / jnp.maximum(jnp.abs(qn), jnp.exp(-m_t))
    mean_sq = _head_sums(h_tilde * h_tilde, ones_bd, 2) * (1.0 / HEAD_DIM)
    out_ref[0] = jax.nn.sigmoid(o_ref_in[0]) * (h_tilde * lax.rsqrt(mean_sq + NORM_EPS) * nw_ref[...])


def _mlstm(qk, v, o_pre, small, conv_w, i_bias, f_bias, norm_w, cps):
    b, s, _ = qk.shape
    rows = cps * CHUNK
    return pl.pallas_call(
        functools.partial(_mlstm_kernel, cps=cps),
        out_shape=jax.ShapeDtypeStruct((b, s, BRANCH), F32),
        grid=(b, s // rows),
        in_specs=[
            pl.BlockSpec((1, rows, 2 * BRANCH), lambda i, c: (i, c, 0)),
            pl.BlockSpec((1, rows, BRANCH), lambda i, c: (i, c, 0)),
            pl.BlockSpec((1, rows, BRANCH), lambda i, c: (i, c, 0)),
            pl.BlockSpec((1, rows, SMALL_W), lambda i, c: (i, c, 0)),
            pl.BlockSpec((CONV_WIDTH, 2 * BRANCH), lambda i, c: (0, 0)),
            pl.BlockSpec((1, HEADS), lambda i, c: (0, 0)),
            pl.BlockSpec((1, HEADS), lambda i, c: (0, 0)),
            pl.BlockSpec((1, BRANCH), lambda i, c: (0, 0)),
        ],
        out_specs=pl.BlockSpec((1, rows, BRANCH), lambda i, c: (i, c, 0)),
        scratch_shapes=[
            pltpu.VMEM((CONV_HIST + rows, 2 * BRANCH), F32),
            pltpu.VMEM((HEADS, HEAD_DIM, HEAD_DIM), F32),
            pltpu.VMEM((1, BRANCH), F32),
            pltpu.VMEM((1, BRANCH), F32),
        ],
        compiler_params=_cparams("parallel", "arbitrary"),
        name="mlstm",
    )(qk, v, o_pre, small, conv_w, i_bias, f_bias, norm_w)


def _s5_prep_kernel(lre_ref, lim_ref, ldt_ref, btre_ref, btim_ref, cre_ref, cim_ref, wglu_ref,
                    toep_ref, wst_ref, wout_ref, al_ref, glu_ref):
    gn, gp = S5_GROUPS * S5_CH, S5_GROUPS * S5_STATE
    lr = lre_ref[...]
    li = lim_ref[...]
    dt = jnp.exp(ldt_ref[...])
    lam = lr * dt
    theta = li * dt

    def power(e):
        mg = jnp.exp(lam * e)
        return mg * jnp.cos(theta * e), mg * jnp.sin(theta * e)

    def cmul(xr, xi, yr, yi):
        return xr * yr - xi * yi, xr * yi + xi * yr

    a_re, a_im = power(1.0)
    den = lr * lr + li * li
    z_re = ((a_re - 1.0) * lr + a_im * li) / den
    z_im = (a_im * lr - (a_re - 1.0) * li) / den
    same_group = (_iota((gn, gp), 0) // S5_CH) == (_iota((gn, gp), 1) // S5_STATE)

    def spread(t):
        return jnp.where(same_group, jnp.concatenate([t] * S5_GROUPS, axis=1), 0.0)

    bb_re, bb_im = cmul(z_re, z_im, spread(btre_ref[...]), spread(btim_ref[...]))
    c_re, c_im = spread(cre_ref[...]), spread(cim_ref[...])
    for s in range(S5_SUB):
        wr, wi = cmul(*power(float(S5_SUB - 1 - s)), bb_re, bb_im)
        wst_ref[s * gn:(s + 1) * gn, :] = jnp.concatenate([wr, wi], axis=1).astype(BF16)
        wr, wi = cmul(*power(float(s + 1)), c_re, c_im)
        wout_ref[s * gn:(s + 1) * gn, :] = jnp.concatenate([wr, -wi], axis=1).astype(BF16)
    c_cat = jnp.concatenate([c_re, -c_im], axis=1)
    resp = []
    for tau in range(S5_SUB):
        wr, wi = cmul(*power(float(tau)), bb_re, bb_im)
        resp.append(_dot_nt_hi(jnp.concatenate([wr, wi], axis=1), c_cat))
    zero = jnp.zeros((gn, gn), F32)
    w_glu = wglu_ref[...]
    for s in range(S5_SUB):
        toep_ref[s * gn:(s + 1) * gn, :] = jnp.concatenate(
            [resp[t - s] if t >= s else zero for t in range(S5_SUB)], axis=1).astype(BF16)
        glu_ref[s * gn:(s + 1) * gn, :] = jnp.concatenate(
            [w_glu if t == s else zero for t in range(S5_SUB)], axis=1).astype(BF16)
    l_re, l_im = power(float(S5_SUB))
    al_ref[...] = jnp.concatenate(
        [jnp.concatenate([l_re, l_re], axis=1), jnp.concatenate([-l_im, l_im], axis=1),
         jnp.zeros((6, 2 * gp), F32)], axis=0)


def _s5_prep(lam_re, lam_im, log_dt, b_re, b_im, c_re, c_im, w_glu):
    gn, gp = S5_GROUPS * S5_CH, S5_GROUPS * S5_STATE
    flat = S5_SUB * gn
    row = lambda a: a.reshape(1, gp)
    full = lambda shape: pl.BlockSpec(shape, lambda i: (0,) * len(shape))
    return pl.pallas_call(
        _s5_prep_kernel,
        out_shape=[
            jax.ShapeDtypeStruct((flat, flat), BF16),
            jax.ShapeDtypeStruct((flat, 2 * gp), BF16),
            jax.ShapeDtypeStruct((flat, 2 * gp), BF16),
            jax.ShapeDtypeStruct((8, 2 * gp), F32),
            jax.ShapeDtypeStruct((flat, flat), BF16),
        ],
        grid=(1,),
        in_specs=[full((1, gp))] * 3 + [full((gn, S5_STATE))] * 4 + [full((gn, gn))],
        out_specs=[full((flat, flat)), full((flat, 2 * gp)), full((flat, 2 * gp)), full((8, 2 * gp)),
                   full((flat, flat))],
        compiler_params=_cparams("arbitrary"),
        name="s5_prep",
    )(row(lam_re), row(lam_im), row(jnp.broadcast_to(log_dt[:, None], lam_re.shape)),
      jnp.swapaxes(b_re, 1, 2).reshape(gn, S5_STATE), jnp.swapaxes(b_im, 1, 2).reshape(gn, S5_STATE),
      c_re.reshape(gn, S5_STATE), c_im.reshape(gn, S5_STATE), w_glu)


def _s5_main_kernel(x_ref, toep_ref, wst_ref, wout_ref, al_ref, glu_ref, d_ref, b_ref, o_ref,
                    tok_ref, inj_ref, prev_ref, state_ref):
    @pl.when(pl.program_id(0) == 0)
    def _():
        state_ref[...] = jnp.zeros(state_ref.shape, F32)

    nb, toks, ch = x_ref.shape
    tc = toks // S5_SUB
    gp = state_ref.shape[1] // 2
    n_lane = ch // LANE
    for j in range(n_lane):
        tok_ref[j] = x_ref[...].reshape(nb * toks, ch)[:, j * LANE:(j + 1) * LANE]
    x = jnp.concatenate(
        [jnp.concatenate([tok_ref[j, pl.ds(S5_SUB * c + s, nb, stride=toks), :]
                          for s in range(S5_SUB) for j in range(n_lane)], axis=1) for c in range(tc)], axis=0)
    xb = x.astype(BF16)
    inj_ref[...] = _dot(xb, wst_ref[...])
    a_c = al_ref[0:1, :]
    a_s = al_ref[1:2, :]
    state = state_ref[...]
    for c in range(tc):
        prev_ref[c * nb:(c + 1) * nb, :] = state
        state = state * a_c + pltpu.roll(state, gp, axis=1) * a_s + inj_ref[c * nb:(c + 1) * nb, :]
    state_ref[...] = state
    y = _dot(xb, toep_ref[...]) + _dot_nt_bf(prev_ref[...], wout_ref[...])
    y = jax.nn.gelu(y + d_ref[...] * x)
    y = y * jax.nn.sigmoid(_dot(y.astype(BF16), glu_ref[...]) + b_ref[...])
    for c in range(tc):
        for s in range(S5_SUB):
            for j in range(n_lane):
                lanes = slice((s * n_lane + j) * LANE, (s * n_lane + j + 1) * LANE)
                tok_ref[j, pl.ds(S5_SUB * c + s, nb, stride=toks), :] = y[c * nb:(c + 1) * nb, lanes]
    o_ref[...] = jnp.concatenate([tok_ref[j] for j in range(n_lane)], axis=1).reshape(nb, toks, ch)


def _s5(u, batch, seq, prm):
    gp = S5_GROUPS * S5_STATE
    flat = S5_SUB * u.shape[1]
    rows = seq // S5_SUB
    tc = next(c for c in (32, 16, 8) if rows % c == 0)
    toep, wst, wout, al, glu = _s5_prep(prm["lam_re"], prm["lam_im"], prm["log_dt"], prm["b_re"], prm["b_im"],
                                        prm["c_re"], prm["c_im"], prm["w_glu"])
    ch = u.shape[1]
    tile = pl.BlockSpec((batch, tc * S5_SUB, ch), lambda i: (0, i, 0))
    y = pl.pallas_call(
        _s5_main_kernel,
        out_shape=jax.ShapeDtypeStruct((batch, seq, ch), F32),
        grid=(rows // tc,),
        in_specs=[tile, _resident((flat, flat)), _resident((flat, 2 * gp)), _resident((flat, 2 * gp)),
                  _resident((8, 2 * gp)), _resident((flat, flat)), _resident((1, flat)), _resident((1, flat))],
        out_specs=tile,
        scratch_shapes=[pltpu.VMEM((ch // LANE, batch * tc * S5_SUB, LANE), F32),
                        pltpu.VMEM((batch * tc, 2 * gp), F32),
                        pltpu.VMEM((batch * tc, 2 * gp), F32),
                        pltpu.VMEM((batch, 2 * gp), F32)],
        compiler_params=_cparams("arbitrary"),
        name="s5_main",
    )(u.reshape(batch, seq, ch), toep, wst, wout, al, glu,
      jnp.tile(prm["d"], (1, S5_SUB)), jnp.tile(prm["b_glu"], (1, S5_SUB)))
    return y.reshape(batch * seq, ch)


def _moba_prep_kernel(q_ref, k_ref, v_ref, pos_ref, inv_ref, qo_ref, ko_ref, vt_ref, sel_ref, km_ref):
    n = pl.program_id(1)
    nbp = km_ref.shape[1]

    @pl.when(n == 0)
    def _():
        km_ref[...] = jnp.zeros(km_ref.shape, F32)

    half = HEAD_DIM // 2
    ang = inv_ref[...] * pos_ref[0, 0]
    cos_t, sin_t = jnp.cos(ang), jnp.sin(ang)
    pairs_per_slab = LANE // HEAD_DIM
    cos = jnp.concatenate([cos_t, cos_t] * pairs_per_slab, axis=0).T
    sin = jnp.concatenate([-sin_t, sin_t] * pairs_per_slab, axis=0).T
    lane = _iota((MOBA_BLOCK, LANE), 1)
    low_half = (lane % HEAD_DIM) < half

    def rope(x):
        swapped = jnp.where(low_half, pltpu.roll(x, LANE - half, axis=1), pltpu.roll(x, half, axis=1))
        return x * cos + swapped * sin

    blk_row = _iota((nbp, MOBA_BLOCK), 0)
    for slab in range(BRANCH // LANE):
        q = rope(q_ref[0, :, slab * LANE:(slab + 1) * LANE])
        k = rope(k_ref[0, :, slab * LANE:(slab + 1) * LANE])
        qo_ref[0, 0, slab] = (q * (HEAD_DIM ** -0.5)).astype(BF16)
        k_mean = jnp.mean(k, axis=0, keepdims=True)
        for sub in range(pairs_per_slab):
            h = slab * pairs_per_slab + sub
            own = (lane // HEAD_DIM) == sub
            gate = _dot_nt_hi(km_ref[h], q)
            rank = jnp.zeros((nbp, MOBA_BLOCK), jnp.int32)
            for m in range(nbp):
                gm = gate[m:m + 1, :]
                beats = (gm > gate) | ((gm == gate) & (m < blk_row))
                rank = rank + jnp.where(beats & (m < n), 1, 0)
            sel_ref[0, 0, h] = ((blk_row < n) & (rank < MOBA_TOPK)).astype(F32)
            ko_ref[0, 0, h] = jnp.where(own, k, 0.0).astype(BF16)
            km_ref[h, pl.ds(n, 1), :] = jnp.where(own[0:1], k_mean, 0.0)
    v_t = v_ref[0].T
    tail = (_iota((VT_ROWS - HEAD_DIM, MOBA_BLOCK), 0) == 0).astype(F32)
    for h in range(HEADS):
        vt_ref[0, 0, h] = jnp.concatenate([v_t[h * HEAD_DIM:(h + 1) * HEAD_DIM], tail], axis=0).astype(BF16)


def _moba_prep(qkv, pos, inv):
    b, s, _ = qkv.shape
    nb = s // MOBA_BLOCK
    nbp = -(-nb // 8) * 8
    slabs = BRANCH // LANE
    blk = lambda j: pl.BlockSpec((1, MOBA_BLOCK, BRANCH), lambda i, n, j=j: (i, n, j))
    return pl.pallas_call(
        _moba_prep_kernel,
        out_shape=[
            jax.ShapeDtypeStruct((b, nb, slabs, MOBA_BLOCK, LANE), BF16),
            jax.ShapeDtypeStruct((b, nb, HEADS, MOBA_BLOCK, LANE), BF16),
            jax.ShapeDtypeStruct((b, nb, HEADS, VT_ROWS, MOBA_BLOCK), BF16),
            jax.ShapeDtypeStruct((b, nb, HEADS, nbp, MOBA_BLOCK), F32),
        ],
        grid=(b, nb),
        in_specs=[blk(0), blk(1), blk(2),
                  pl.BlockSpec((1, 1, 1, MOBA_BLOCK), lambda i, n: (i, n, 0, 0)),
                  pl.BlockSpec((HEAD_DIM // 2, 1), lambda i, n: (0, 0))],
        out_specs=[pl.BlockSpec((1, 1, slabs, MOBA_BLOCK, LANE), lambda i, n: (i, n, 0, 0, 0)),
                   pl.BlockSpec((1, 1, HEADS, MOBA_BLOCK, LANE), lambda i, n: (i, n, 0, 0, 0)),
                   pl.BlockSpec((1, 1, HEADS, VT_ROWS, MOBA_BLOCK), lambda i, n: (i, n, 0, 0, 0)),
                   pl.BlockSpec((1, 1, HEADS, nbp, MOBA_BLOCK), lambda i, n: (i, n, 0, 0, 0))],
        scratch_shapes=[pltpu.VMEM((HEADS, nbp, LANE), F32)],
        compiler_params=_cparams("parallel", "arbitrary"),
        name="moba_prep",
    )(qkv, qkv, qkv, pos, inv)


def _moba_attn_kernel(q_ref, k_ref, vt_ref, sel_ref, o_ref):
    i = pl.program_id(1)
    blk = MOBA_BLOCK
    nbp = sel_ref.shape[3]
    key_i = _iota((blk, blk), 0)
    qry_i = _iota((blk, blk), 1)
    sel_row = _iota((nbp, blk), 0)

    def update(blocks, carries, masks):
        scores = [[lax.dot_general(k_ref[0, n, h], q_ref[0, 0, h * HEAD_DIM // LANE], NT_DIMS,
                                   preferred_element_type=F32) for h in range(HEADS)] for n in blocks]
        stats = []
        for h in range(HEADS):
            m_run = carries[h][0]
            s = [jnp.where(masks[b][h], scores[b][h], -jnp.inf) for b in range(len(blocks))]
            m_new = m_run
            for s_b in s:
                m_new = jnp.maximum(m_new, jnp.max(s_b, axis=0, keepdims=True))
            m_safe = jnp.where(m_new == -jnp.inf, 0.0, m_new)
            p = jnp.concatenate([jnp.exp(s_b - m_safe).astype(BF16) for s_b in s], axis=0)
            stats.append((m_new, jnp.exp(m_run - m_safe), p))
        return tuple((stats[h][0], stats[h][1] * carries[h][1]
                      + _dot(jnp.concatenate([vt_ref[0, n, h] for n in blocks], axis=1), stats[h][2]))
                     for h in range(HEADS))

    def picked(n):
        return [jnp.max(jnp.where(sel_row == n, sel_ref[0, 0, h], 0.0), axis=0, keepdims=True) > 0.0
                for h in range(HEADS)]

    def body(t, carries):
        return update([2 * t, 2 * t + 1], carries, [picked(2 * t), picked(2 * t + 1)])

    init = (jnp.full((1, blk), -jnp.inf, F32), jnp.zeros((VT_ROWS, blk), F32))
    carries = lax.fori_loop(0, i // 2, body, (init,) * HEADS)
    causal = [key_i <= qry_i] * HEADS
    carries = lax.cond((i % 2) == 1,
                       lambda c: update([i - 1, i], c, [picked(i - 1), causal]),
                       lambda c: update([i], c, [causal]), carries)
    o_ref[0] = jnp.concatenate([acc[0:HEAD_DIM] / acc[HEAD_DIM:HEAD_DIM + 1] for _, acc in carries], axis=0).T


def _moba_attn(q, k, vt, sel):
    b, nb = q.shape[:2]
    nbp = sel.shape[3]
    return pl.pallas_call(
        _moba_attn_kernel,
        out_shape=jax.ShapeDtypeStruct((b, nb * MOBA_BLOCK, BRANCH), F32),
        grid=(b, nb),
        in_specs=[
            pl.BlockSpec((1, 1, BRANCH // LANE, MOBA_BLOCK, LANE), lambda i, n: (i, n, 0, 0, 0)),
            pl.BlockSpec((1, nb, HEADS, MOBA_BLOCK, LANE), lambda i, n: (i, 0, 0, 0, 0)),
            pl.BlockSpec((1, nb, HEADS, VT_ROWS, MOBA_BLOCK), lambda i, n: (i, 0, 0, 0, 0)),
            pl.BlockSpec((1, 1, HEADS, nbp, MOBA_BLOCK), lambda i, n: (i, n, 0, 0, 0)),
        ],
        out_specs=pl.BlockSpec((1, MOBA_BLOCK, BRANCH), lambda i, n: (i, n, 0)),
        compiler_params=_cparams("parallel", "arbitrary"),
        name="moba_attn",
    )(q, k, vt, sel)


def _moba(qkv, positions):
    b, s, _ = qkv.shape
    inv = ROPE_THETA ** (-jnp.arange(0, HEAD_DIM, 2, dtype=F32) / HEAD_DIM)
    pos = positions.astype(F32).reshape(b, s // MOBA_BLOCK, 1, MOBA_BLOCK)
    return _moba_attn(*_moba_prep(qkv, pos, inv[:, None]))


def _permute_w_in(w_in):
    cuts = [0]
    for width in (3 * BRANCH, BRANCH, HEADS, HEADS, 2 * BRANCH, BRANCH, BRANCH, HEADS, HEADS, BRANCH, 3 * BRANCH):
        cuts.append(cuts[-1] + width)
    seg = [w_in[:, cuts[n]:cuts[n + 1]] for n in range(len(cuts) - 1)]
    gdn_qkv, gdn_z, gdn_b, gdn_a, ml_qk, ml_v, ml_o, ml_i, ml_f, s5_u, moba_qkv = seg
    pad = jnp.zeros((w_in.shape[0], SMALL_W - 4 * HEADS), w_in.dtype)
    return jnp.concatenate([gdn_qkv, gdn_z, ml_qk, ml_v, ml_o, s5_u, moba_qkv, gdn_b, gdn_a, ml_i, ml_f, pad], axis=1)


def _token_tile(t, largest=512):
    for tm in (1024, 512, 256, 128, 64, 32, 16, 8):
        if tm > largest:
            continue
        if t % tm == 0:
            return tm
    raise ValueError(f"token count {t} is not a multiple of 8")


def _chunks_per_step(seq, most=8):
    n_chunks = seq // CHUNK
    for cps in (16, 8, 4, 2, 1):
        if cps <= most and n_chunks % cps == 0:
            return cps


def _ff_slice(f):
    for sw in (2 * LANE, LANE):
        if f % sw == 0:
            return sw
    raise ValueError(f"ffn width {f} is not a multiple of {LANE}")


def kernel(x, p, positions, ffn1_norm, ffn1_w_gu, ffn1_w_down, mix_norm, w_in, gdn_conv, gdn_a_log, gdn_dt_bias, gdn_norm, mlstm_conv, mlstm_i_bias, mlstm_f_bias, mlstm_norm, s5_lambda_re, s5_lambda_im, s5_b_re, s5_b_im, s5_c_re, s5_c_im, s5_d, s5_log_dt, s5_w_glu, s5_b_glu, w_gate, w_branch, w_out, ffn2_norm, ffn2_w_gu, ffn2_w_down, ple_norm, ple_w_proj, ple_w_gate, final_norm):
    batch, seq, d = x.shape
    depth = p.shape[0]
    t = batch * seq
    tm = _token_tile(t)
    tm_ffn = _token_tile(t, largest=1024)
    tf = _ff_slice(ffn1_w_down.shape[1])
    cps = _chunks_per_step(seq)
    bf = lambda a: a.astype(BF16)
    row = lambda a: a.reshape(1, -1)
    seq3 = lambda a: a.reshape(batch, seq, a.shape[-1])

    h = x.reshape(t, d)
    for i in range(depth):
        h = _ffn(h, row(ffn1_norm[i]), bf(ffn1_w_gu[i]), bf(ffn1_w_down[i]), tm_ffn, tf)

        gdn_qkv, gdn_z, ml_qk, ml_v, ml_o, s5_u, moba_qkv, small = _inproj(
            h, row(mix_norm[i]), bf(_permute_w_in(w_in[i])), tm)
        y_gdn = _gdn(seq3(gdn_qkv), seq3(gdn_z), seq3(small), gdn_conv[i], row(gdn_a_log[i]),
                     row(gdn_dt_bias[i]), row(gdn_norm[i]), cps)
        y_mlstm = _mlstm(seq3(ml_qk), seq3(ml_v), seq3(ml_o), seq3(small), mlstm_conv[i],
                         row(mlstm_i_bias[i]), row(mlstm_f_bias[i]), row(mlstm_norm[i]),
                         _chunks_per_step(seq, most=16))
        s5_prm = dict(lam_re=s5_lambda_re[i], lam_im=s5_lambda_im[i], log_dt=s5_log_dt[i],
                      b_re=s5_b_re[i], b_im=s5_b_im[i], c_re=s5_c_re[i], c_im=s5_c_im[i],
                      d=row(s5_d[i]), w_glu=s5_w_glu[i], b_glu=row(s5_b_glu[i]))
        y_s5 = _s5(s5_u, batch, seq, s5_prm)
        y_moba = _moba(seq3(moba_qkv), positions)
        ys = (y_gdn.reshape(t, BRANCH), y_mlstm.reshape(t, BRANCH), y_s5, y_moba.reshape(t, BRANCH))
        h = _merge(h, row(mix_norm[i]), ys, bf(w_gate[i]), bf(w_branch[i]), bf(w_out[i]), tm)

        ple = (row(ple_norm[i]), p[i].reshape(t, -1), bf(ple_w_proj[i]), bf(ple_w_gate[i]), row(final_norm))
        h = _ffn(h, row(ffn2_norm[i]), bf(ffn2_w_gu[i]), bf(ffn2_w_down[i]), tm_ffn, tf, ple, i == depth - 1)
    return h.reshape(batch, seq, d)
```

```python
import functools
import math

import jax
import jax.numpy as jnp
from jax import lax
from jax.experimental import pallas as pl
from jax.experimental.pallas import tpu as pltpu

F32 = jnp.float32
BF16 = jnp.bfloat16
HIGHEST = lax.Precision.HIGHEST

NORM_EPS = 1e-6
HEADS = 4
HEAD_DIM = 64
BRANCH = HEADS * HEAD_DIM
CHUNK = 64
CONV_WIDTH = 4
CONV_HIST = 8
GDN_GROUP = 4
S5_GROUPS = 16
S5_CH = 16
S5_STATE = 64
S5_SUB = 4
MOBA_BLOCK = 256
MOBA_TOPK = 3
VT_ROWS = HEAD_DIM + 16
ROPE_THETA = 10000.0
LANE = 128
SMALL_W = LANE
V7X_VMEM_LIMIT = 56 * 1024 * 1024

NT_DIMS = (((1,), (1,)), ((), ()))


def _cparams(*sem):
    return pltpu.CompilerParams(dimension_semantics=sem, vmem_limit_bytes=V7X_VMEM_LIMIT)


def _rms(x, w):
    return x * lax.rsqrt(jnp.mean(x * x, axis=-1, keepdims=True) + NORM_EPS) * w


def _silu(x):
    return x * jax.nn.sigmoid(x)


def _softplus(x):
    return jnp.maximum(x, 0.0) + jnp.log1p(jnp.exp(-jnp.abs(x)))


def _dot(a, b):
    return jnp.dot(a, b, preferred_element_type=F32)


def _dot_bf(a, b):
    return jnp.dot(a.astype(BF16), b.astype(BF16), preferred_element_type=F32)


def _dot_nt_bf(a, b):
    return lax.dot_general(a.astype(BF16), b.astype(BF16), NT_DIMS, preferred_element_type=F32)


def _dot_nt_hi(a, b):
    return lax.dot_general(a, b, NT_DIMS, precision=HIGHEST, preferred_element_type=F32)


def _split_bf(a, parts):
    out = []
    for _ in range(parts):
        piece = a.astype(BF16)
        out.append(piece)
        a = a - piece.astype(F32)
    return out


def _dot_x3(a, b):
    a_hi, a_lo = _split_bf(a, 2)
    b_hi, b_lo = _split_bf(b, 2)
    return _dot(jnp.concatenate([a_hi, a_hi, a_lo], axis=1), jnp.concatenate([b_hi, b_lo, b_hi], axis=0))


def _dot_exact01(a01_x3, b):
    return _dot(a01_x3, jnp.concatenate(_split_bf(b, 3), axis=0))


def _iota(shape, dim):
    return lax.broadcasted_iota(jnp.int32, shape, dim)


def _head_block_ones():
    r = _iota((BRANCH, BRANCH), 0) // HEAD_DIM
    c = _iota((BRANCH, BRANCH), 1) // HEAD_DIM
    return (r == c).astype(BF16)


def _head_sums(x, ones_bd, parts):
    pieces = _split_bf(x, parts)
    out = _dot(pieces[0], ones_bd)
    for piece in pieces[1:]:
        out = out + _dot(piece, ones_bd)
    return out


def _head_bcast(cols):
    rows = cols.shape[0]
    head = _iota((rows, BRANCH), 1) // HEAD_DIM
    out = jnp.broadcast_to(cols[:, 0:1], (rows, BRANCH))
    for h in range(1, HEADS):
        out = jnp.where(head == h, jnp.broadcast_to(cols[:, h:h + 1], (rows, BRANCH)), out)
    return out


def _causal_conv_silu(x_ref, w_ref, pad_ref, first):
    rows = x_ref.shape[1]

    @pl.when(first)
    def _():
        pad_ref[0:CONV_HIST, :] = jnp.zeros((CONV_HIST, pad_ref.shape[1]), F32)

    pad_ref[CONV_HIST:CONV_HIST + rows, :] = x_ref[0]
    w = w_ref[...]
    acc = w[CONV_WIDTH - 1:CONV_WIDTH, :] * pad_ref[CONV_HIST:CONV_HIST + rows, :]
    for back in range(1, CONV_WIDTH):
        tap = CONV_WIDTH - 1 - back
        acc = acc + w[tap:tap + 1, :] * pad_ref[CONV_HIST - back:CONV_HIST - back + rows, :]
    pad_ref[0:CONV_HIST, :] = pad_ref[rows:rows + CONV_HIST, :]
    return _silu(acc)


def _tri_consts():
    r = _iota((3 * CHUNK, 3 * CHUNK), 0)
    k = _iota((3 * CHUNK, 3 * CHUNK), 1) % CHUNK
    i = r % CHUNK
    part = r // CHUNK
    return (((part == 0) & (k <= i)) | ((part == 1) & (k > i)) | (part == 2)).astype(BF16)


def _resident(shape):
    return pl.BlockSpec(shape, lambda *_: (0,) * len(shape), pipeline_mode=pl.Buffered(1))


def _ffn_kernel(x_ref, nw_ref, wgu_ref, wd_ref, *rest, sw, ple, final):
    o_ref = rest[-1]
    x = x_ref[...]
    xn = _rms(x, nw_ref[...]).astype(BF16)
    f = wd_ref.shape[0]

    def gate_up(j):
        return (_dot(xn, wgu_ref[:, j * sw:(j + 1) * sw]), _dot(xn, wgu_ref[:, f + j * sw:f + (j + 1) * sw]))

    ahead = gate_up(0)
    acc = None
    for j in range(f // sw):
        gate, up = ahead
        if (j + 1) * sw < f:
            ahead = gate_up(j + 1)
        part = _dot((_silu(gate) * up).astype(BF16), wd_ref[j * sw:(j + 1) * sw, :])
        acc = part if acc is None else acc + part
    out = x + 0.5 * acc
    if ple:
        pn_ref, p_ref, wp_ref, wg_ref, fw_ref = rest[:-1]
        gate = jax.nn.sigmoid(_dot(_rms(out, pn_ref[...]).astype(BF16), wg_ref[...]))
        out = out + _dot(p_ref[...].astype(BF16), wp_ref[...]) * gate
        if final:
            out = _rms(out, fw_ref[...])
    o_ref[...] = out


def _ffn(h, norm_w, w_gu, w_down, tm, sw, ple=None, final=False):
    t, d = h.shape
    f = w_down.shape[0]
    operands = [h, norm_w, w_gu, w_down]
    in_specs = [pl.BlockSpec((tm, d), lambda i: (i, 0)), _resident((1, d)), _resident((d, 2 * f)), _resident((f, d))]
    if ple is not None:
        pd = ple[1].shape[1]
        operands += list(ple)
        in_specs += [_resident((1, d)), pl.BlockSpec((tm, pd), lambda i: (i, 0)), _resident((pd, d)),
                     _resident((d, d)), _resident((1, d))]
    return pl.pallas_call(
        functools.partial(_ffn_kernel, sw=sw, ple=ple is not None, final=final),
        out_shape=jax.ShapeDtypeStruct((t, d), F32),
        grid=(t // tm,),
        in_specs=in_specs,
        out_specs=pl.BlockSpec((tm, d), lambda i: (i, 0)),
        compiler_params=_cparams("parallel"),
        name="ffn",
    )(*operands)


IN_SEGMENTS = (3 * BRANCH, BRANCH, 2 * BRANCH, BRANCH, BRANCH, BRANCH, 3 * BRANCH, SMALL_W)


def _inproj_kernel(x_ref, nw_ref, w_ref, *o_refs):
    xn = _rms(x_ref[...], nw_ref[...]).astype(BF16)
    start = 0
    for o_ref, width in zip(o_refs, IN_SEGMENTS):
        o_ref[...] = _dot(xn, w_ref[:, start:start + width])
        start += width


def _inproj(h, norm_w, w_perm, tm):
    t, d = h.shape
    n = w_perm.shape[1]
    return pl.pallas_call(
        _inproj_kernel,
        out_shape=[jax.ShapeDtypeStruct((t, w), F32) for w in IN_SEGMENTS],
        grid=(t // tm,),
        in_specs=[
            pl.BlockSpec((tm, d), lambda i: (i, 0)),
            pl.BlockSpec((1, d), lambda i: (0, 0)),
            pl.BlockSpec((d, n), lambda i: (0, 0)),
        ],
        out_specs=[pl.BlockSpec((tm, w), lambda i: (i, 0)) for w in IN_SEGMENTS],
        compiler_params=_cparams("parallel"),
        name="inproj",
    )(h, norm_w, w_perm)


def _merge_kernel(h_ref, nw_ref, y0_ref, y1_ref, y2_ref, y3_ref, wg_ref, wb_ref, wo_ref, o_ref):
    h = h_ref[...]
    u = _rms(h, nw_ref[...]).astype(BF16)
    merged = None
    for b, y_ref in enumerate((y0_ref, y1_ref, y2_ref, y3_ref)):
        term = jax.nn.sigmoid(_dot(u, wg_ref[b])) * _dot(y_ref[...].astype(BF16), wb_ref[b])
        merged = term if merged is None else merged + term
    o_ref[...] = h + _dot(merged.astype(BF16), wo_ref[...])


def _merge(h, norm_w, ys, w_gate, w_branch, w_out, tm):
    t, d = h.shape
    nb, bw, _ = w_branch.shape
    return pl.pallas_call(
        _merge_kernel,
        out_shape=jax.ShapeDtypeStruct((t, d), F32),
        grid=(t // tm,),
        in_specs=[
            pl.BlockSpec((tm, d), lambda i: (i, 0)),
            pl.BlockSpec((1, d), lambda i: (0, 0)),
        ] + [pl.BlockSpec((tm, bw), lambda i: (i, 0)) for _ in range(nb)] + [
            _resident((nb, d, d)),
            _resident((nb, bw, d)),
            _resident((d, d)),
        ],
        out_specs=pl.BlockSpec((tm, d), lambda i: (i, 0)),
        compiler_params=_cparams("parallel"),
        name="merge",
    )(h, norm_w, *ys, w_gate, w_branch, w_out)


def _gdn_kernel(qkv_ref, z_ref, sm_ref, cw_ref, alog_ref, dtb_ref, nw_ref, o_ref, pad_ref, state_ref, *, cps):
    step = pl.program_id(1)

    @pl.when(step == 0)
    def _():
        state_ref[...] = jnp.zeros(state_ref.shape, F32)

    x_all = _causal_conv_silu(qkv_ref, cw_ref, pad_ref, step == 0)
    ones_bd = _head_block_ones()
    tri = _tri_consts()
    eye = (_iota((CHUNK, CHUNK), 0) == _iota((CHUNK, CHUNK), 1)).astype(BF16)
    row_c = _iota((CHUNK, BRANCH), 0)
    col_c = _iota((CHUNK, BRANCH), 1) % HEAD_DIM
    gcs = GDN_GROUP if cps % GDN_GROUP == 0 else cps
    rows = gcs * CHUNK
    row = _iota((rows, BRANCH), 0) % CHUNK
    col = _iota((rows, BRANCH), 1) % HEAD_DIM
    q_eff, out_loc, inject, mix, e_tot = {}, {}, {}, {}, {}
    for grp in range(cps // gcs):
        x = x_all[grp * rows:(grp + 1) * rows]
        sm = sm_ref[0, grp * rows:(grp + 1) * rows, :]
        beta = _head_bcast(jax.nn.sigmoid(sm[:, 0:HEADS]))
        g = _head_bcast(-jnp.exp(alog_ref[...]) * _softplus(sm[:, HEADS:2 * HEADS] + dtb_ref[...]))
        q, k, v = x[:, 0:BRANCH], x[:, BRANCH:2 * BRANCH], x[:, 2 * BRANCH:3 * BRANCH]
        q = q * lax.rsqrt(_head_sums(q * q, ones_bd, 2) + NORM_EPS) * (HEAD_DIM ** -0.5)
        k = k * lax.rsqrt(_head_sums(k * k, ones_bd, 2) + NORM_EPS)
        sums = []
        for ci in range(gcs):
            gb = g[ci * CHUNK:(ci + 1) * CHUNK]
            sums.append(_dot_exact01(tri, jnp.concatenate([gb, jnp.where(row_c > col_c, gb, 0.0)], axis=1)))
        stack = lambda r0, c0: jnp.concatenate([s[r0:r0 + CHUNK, c0:c0 + BRANCH] for s in sums], axis=0)
        e_gc = jnp.exp(stack(0, 0))
        e_rest = jnp.exp(stack(CHUNK, 0))
        e_total = jnp.exp(stack(2 * CHUNK, 0))
        decay = jnp.where(row >= col, jnp.exp(stack(0, BRANCH)), 0.0)
        k_beta = k * beta
        q_dec = q * e_gc
        k_dec = (k * e_rest).astype(BF16)
        rhs_u = v * beta
        rhs_w = k_beta * e_gc

        probs = [(ci, h) for ci in range(gcs) for h in range(HEADS)]

        def cut(t, p):
            return t[p[0] * CHUNK:(p[0] + 1) * CHUNK, p[1] * HEAD_DIM:(p[1] + 1) * HEAD_DIM]

        gram = {p: _dot_nt_bf(jnp.concatenate([cut(k_beta, p), cut(q, p)], axis=0), cut(k, p)) for p in probs}
        k_dec_t = {p: lax.dot_general(eye, cut(k_dec, p), NT_DIMS, preferred_element_type=F32) for p in probs}
        power, sol, qk = {}, {}, {}
        for p in probs:
            dec = cut(decay, p)
            power[p] = jnp.where(cut(row > col, p), gram[p][0:CHUNK] * dec, 0.0)
            qk[p] = gram[p][CHUNK:2 * CHUNK] * dec
            sol[p] = jnp.concatenate([cut(rhs_u, p), cut(rhs_w, p)], axis=1)
        n_factors = int(math.log2(CHUNK))
        for j in range(n_factors):
            last = j == n_factors - 1
            prods = {p: _dot_x3(power[p], sol[p] if last else jnp.concatenate([sol[p], power[p]], axis=1))
                     for p in probs}
            for p in probs:
                term = prods[p][:, 0:2 * HEAD_DIM]
                sol[p] = sol[p] - term if j == 0 else sol[p] + term
                if not last:
                    power[p] = prods[p][:, 2 * HEAD_DIM:3 * HEAD_DIM]
        both = {p: _dot_bf(jnp.concatenate([qk[p], k_dec_t[p]], axis=0), sol[p]) for p in probs}
        for p in probs:
            key = (grp * gcs + p[0], p[1])
            out_loc[key] = both[p][0:CHUNK, 0:HEAD_DIM]
            q_eff[key] = cut(q_dec, p) - both[p][0:CHUNK, HEAD_DIM:2 * HEAD_DIM]
            inject[key] = both[p][CHUNK:2 * CHUNK, 0:HEAD_DIM]
            mix[key] = both[p][CHUNK:2 * CHUNK, HEAD_DIM:2 * HEAD_DIM]
            e_tot[key] = cut(e_total, p)
    state = [state_ref[h] for h in range(HEADS)]
    entering, from_state = {}, {}
    for ci in range(cps):
        mixed = [_dot_bf(mix[ci, h], state[h]) for h in range(HEADS)]
        if ci:
            from_state.update({(ci - 1, h): _dot_bf(q_eff[ci - 1, h], entering[ci - 1, h]) for h in range(HEADS)})
        for h in range(HEADS):
            entering[ci, h] = state[h]
            state[h] = state[h] * e_tot[ci, h] - mixed[h] + inject[ci, h]
    from_state.update({(cps - 1, h): _dot_bf(q_eff[cps - 1, h], entering[cps - 1, h]) for h in range(HEADS)})
    for h in range(HEADS):
        state_ref[h] = state[h]
    out = jnp.concatenate([jnp.concatenate([from_state[ci, h] + out_loc[ci, h] for h in range(HEADS)], axis=1)
                           for ci in range(cps)], axis=0)
    mean_sq = _head_sums(out * out, ones_bd, 2) * (1.0 / HEAD_DIM)
    nw = jnp.concatenate([nw_ref[...]] * HEADS, axis=1)
    o_ref[0] = out * lax.rsqrt(mean_sq + NORM_EPS) * nw * _silu(z_ref[0])


def _gdn(qkv, z, small, conv_w, a_log, dt_bias, norm_w, cps):
    b, s, _ = qkv.shape
    rows = cps * CHUNK
    return pl.pallas_call(
        functools.partial(_gdn_kernel, cps=cps),
        out_shape=jax.ShapeDtypeStruct((b, s, BRANCH), F32),
        grid=(b, s // rows),
        in_specs=[
            pl.BlockSpec((1, rows, 3 * BRANCH), lambda i, c: (i, c, 0)),
            pl.BlockSpec((1, rows, BRANCH), lambda i, c: (i, c, 0)),
            pl.BlockSpec((1, rows, SMALL_W), lambda i, c: (i, c, 0)),
            pl.BlockSpec((CONV_WIDTH, 3 * BRANCH), lambda i, c: (0, 0)),
            pl.BlockSpec((1, HEADS), lambda i, c: (0, 0)),
            pl.BlockSpec((1, HEADS), lambda i, c: (0, 0)),
            pl.BlockSpec((1, HEAD_DIM), lambda i, c: (0, 0)),
        ],
        out_specs=pl.BlockSpec((1, rows, BRANCH), lambda i, c: (i, c, 0)),
        scratch_shapes=[
            pltpu.VMEM((CONV_HIST + rows, 3 * BRANCH), F32),
            pltpu.VMEM((HEADS, HEAD_DIM, HEAD_DIM), F32),
        ],
        compiler_params=_cparams("parallel", "arbitrary"),
        name="gdn",
    )(qkv, z, small, conv_w, a_log, dt_bias, norm_w)


def _mlstm_kernel(qk_ref, v_ref, o_ref_in, sm_ref, cw_ref, ib_ref, fb_ref, nw_ref, out_ref,
                  pad_ref, c_ref, n_ref, m_ref, *, cps):
    step = pl.program_id(1)

    @pl.when(step == 0)
    def _():
        c_ref[...] = jnp.zeros(c_ref.shape, F32)
        n_ref[...] = jnp.zeros(n_ref.shape, F32)
        m_ref[...] = jnp.zeros(m_ref.shape, F32)

    rows = cps * CHUNK
    x = _causal_conv_silu(qk_ref, cw_ref, pad_ref, step == 0)
    q = x[:, 0:BRANCH]
    k = x[:, BRANCH:2 * BRANCH] * (HEAD_DIM ** -0.5)
    v = v_ref[0]
    sm = sm_ref[0]
    log_i = _head_bcast(sm[:, 2 * HEADS:3 * HEADS] + ib_ref[...])
    log_f = _head_bcast(-_softplus(-(sm[:, 3 * HEADS:4 * HEADS] + fb_ref[...])))
    ones_bd = _head_block_ones()

    row_c = _iota((CHUNK, BRANCH), 0)
    col_c = _iota((CHUNK, BRANCH), 1) % HEAD_DIM
    tri = _tri_consts()
    sums = []
    for ci in range(cps):
        lf = log_f[ci * CHUNK:(ci + 1) * CHUNK]
        li = log_i[ci * CHUNK:(ci + 1) * CHUNK]
        sums.append(_dot_exact01(tri, jnp.concatenate(
            [lf, jnp.where(row_c > col_c, lf, 0.0), jnp.where(row_c == col_c, li, 0.0)], axis=1)))
    stack = lambda r0, c0: jnp.concatenate([s[r0:r0 + CHUNK, c0:c0 + BRANCH] for s in sums], axis=0)
    row = _iota((rows, BRANCH), 0) % CHUNK
    col = _iota((rows, BRANCH), 1) % HEAD_DIM
    b_col = stack(0, 0)
    b_last = stack(2 * CHUNK, 0)
    a_key = stack(CHUNK, 0) + log_i
    d_intra = jnp.where(row >= col, stack(0, BRANCH) + stack(2 * CHUNK, 2 * BRANCH), -jnp.inf)
    run = log_i - b_col
    shift = 1
    while shift < CHUNK:
        run = jnp.where(row >= shift, jnp.maximum(run, pltpu.roll(run, shift, axis=0)), run)
        shift *= 2
    m_mem = m_ref[...]
    m_prev, m_next, scale = [], [], []
    for ci in range(cps):
        last = ci * CHUNK + CHUNK - 1
        g_c = b_last[last:last + 1]
        m_new = jnp.maximum(g_c + m_mem, g_c + run[last:last + 1])
        m_prev.append(m_mem)
        m_next.append(m_new)
        scale.append(jnp.exp(g_c + m_mem - m_new))
        m_mem = m_new
    m_ref[...] = m_mem
    per_chunk = lambda rows_: jnp.concatenate([jnp.broadcast_to(r, (CHUNK, BRANCH)) for r in rows_], axis=0)
    m_inter = b_col + per_chunk(m_prev)
    m_t = jnp.maximum(m_inter, b_col + run)
    w_inter = jnp.exp(m_inter - m_t)
    e_intra = jnp.exp(d_intra - m_t)
    k_w = k * jnp.exp(a_key - per_chunk(m_next))
    k_w_bf = k_w.astype(BF16)

    probs = [(ci, h) for ci in range(cps) for h in range(HEADS)]

    def cut(t, p):
        return t[p[0] * CHUNK:(p[0] + 1) * CHUNK, p[1] * HEAD_DIM:(p[1] + 1) * HEAD_DIM]

    def paste(pieces):
        return jnp.concatenate([jnp.concatenate([pieces[ci, h] for h in range(HEADS)], axis=1)
                                for ci in range(cps)], axis=0)

    eye = (_iota((CHUNK, CHUNK), 0) == _iota((CHUNK, CHUNK), 1)).astype(BF16)
    s_qk = {p: _dot_nt_bf(cut(q, p), cut(k, p)) * cut(e_intra, p) for p in probs}
    k_w_t = {p: lax.dot_general(eye, cut(k_w_bf, p), NT_DIMS, preferred_element_type=F32) for p in probs}
    intra = {p: _dot_bf(s_qk[p], cut(v, p)) for p in probs}
    inject = {p: _dot_bf(k_w_t[p], cut(v, p)) for p in probs}
    c_mem = [c_ref[h] for h in range(HEADS)]
    n_mem = n_ref[...]
    c_prev, n_prev = {}, []
    for ci in range(cps):
        n_prev.append(n_mem)
        n_mem = n_mem * scale[ci] + jnp.sum(k_w[ci * CHUNK:(ci + 1) * CHUNK], axis=0, keepdims=True)
        for h in range(HEADS):
            c_prev[ci, h] = c_mem[h]
            c_mem[h] = c_mem[h] * scale[ci][:, h * HEAD_DIM:(h + 1) * HEAD_DIM] + inject[ci, h]
    for h in range(HEADS):
        c_ref[h] = c_mem[h]
    n_ref[...] = n_mem
    inter = {p: _dot_bf(cut(q, p), c_prev[p]) for p in probs}
    num = w_inter * paste(inter) + paste(intra)
    qn = w_inter * _head_sums(q * per_chunk(n_prev), ones_bd, 3) + _head_sums(paste(s_qk), ones_bd, 3)
    h_tilde = num / jnp.maximum(jnp.abs(qn), jnp.exp(-m_t))
    mean_sq = _head_sums(h_tilde * h_tilde, ones_bd, 2) * (1.0 / HEAD_DIM)
    out_ref[0] = jax.nn.sigmoid(o_ref_in[0]) * (h_tilde * lax.rsqrt(mean_sq + NORM_EPS) * nw_ref[...])


def _mlstm(qk, v, o_pre, small, conv_w, i_bias, f_bias, norm_w, cps):
    b, s, _ = qk.shape
    rows = cps * CHUNK
    return pl.pallas_call(
        functools.partial(_mlstm_kernel, cps=cps),
        out_shape=jax.ShapeDtypeStruct((b, s, BRANCH), F32),
        grid=(b, s // rows),
        in_specs=[
            pl.BlockSpec((1, rows, 2 * BRANCH), lambda i, c: (i, c, 0)),
            pl.BlockSpec((1, rows, BRANCH), lambda i, c: (i, c, 0)),
            pl.BlockSpec((1, rows, BRANCH), lambda i, c: (i, c, 0)),
            pl.BlockSpec((1, rows, SMALL_W), lambda i, c: (i, c, 0)),
            pl.BlockSpec((CONV_WIDTH, 2 * BRANCH), lambda i, c: (0, 0)),
            pl.BlockSpec((1, HEADS), lambda i, c: (0, 0)),
            pl.BlockSpec((1, HEADS), lambda i, c: (0, 0)),
            pl.BlockSpec((1, BRANCH), lambda i, c: (0, 0)),
        ],
        out_specs=pl.BlockSpec((1, rows, BRANCH), lambda i, c: (i, c, 0)),
        scratch_shapes=[
            pltpu.VMEM((CONV_HIST + rows, 2 * BRANCH), F32),
            pltpu.VMEM((HEADS, HEAD_DIM, HEAD_DIM), F32),
            pltpu.VMEM((1, BRANCH), F32),
            pltpu.VMEM((1, BRANCH), F32),
        ],
        compiler_params=_cparams("parallel", "arbitrary"),
        name="mlstm",
    )(qk, v, o_pre, small, conv_w, i_bias, f_bias, norm_w)


def _s5_prep_kernel(lre_ref, lim_ref, ldt_ref, btre_ref, btim_ref, cre_ref, cim_ref, wglu_ref,
                    toep_ref, wst_ref, wout_ref, al_ref, glu_ref):
    gn, gp = S5_GROUPS * S5_CH, S5_GROUPS * S5_STATE
    lr = lre_ref[...]
    li = lim_ref[...]
    dt = jnp.exp(ldt_ref[...])
    lam = lr * dt
    theta = li * dt

    def power(e):
        mg = jnp.exp(lam * e)
        return mg * jnp.cos(theta * e), mg * jnp.sin(theta * e)

    def cmul(xr, xi, yr, yi):
        return xr * yr - xi * yi, xr * yi + xi * yr

    a_re, a_im = power(1.0)
    den = lr * lr + li * li
    z_re = ((a_re - 1.0) * lr + a_im * li) / den
    z_im = (a_im * lr - (a_re - 1.0) * li) / den
    same_group = (_iota((gn, gp), 0) // S5_CH) == (_iota((gn, gp), 1) // S5_STATE)

    def spread(t):
        return jnp.where(same_group, jnp.concatenate([t] * S5_GROUPS, axis=1), 0.0)

    bb_re, bb_im = cmul(z_re, z_im, spread(btre_ref[...]), spread(btim_ref[...]))
    c_re, c_im = spread(cre_ref[...]), spread(cim_ref[...])
    for s in range(S5_SUB):
        wr, wi = cmul(*power(float(S5_SUB - 1 - s)), bb_re, bb_im)
        wst_ref[s * gn:(s + 1) * gn, :] = jnp.concatenate([wr, wi], axis=1).astype(BF16)
        wr, wi = cmul(*power(float(s + 1)), c_re, c_im)
        wout_ref[s * gn:(s + 1) * gn, :] = jnp.concatenate([wr, -wi], axis=1).astype(BF16)
    c_cat = jnp.concatenate([c_re, -c_im], axis=1)
    resp = []
    for tau in range(S5_SUB):
        wr, wi = cmul(*power(float(tau)), bb_re, bb_im)
        resp.append(_dot_nt_hi(jnp.concatenate([wr, wi], axis=1), c_cat))
    zero = jnp.zeros((gn, gn), F32)
    w_glu = wglu_ref[...]
    for s in range(S5_SUB):
        toep_ref[s * gn:(s + 1) * gn, :] = jnp.concatenate(
            [resp[t - s] if t >= s else zero for t in range(S5_SUB)], axis=1).astype(BF16)
        glu_ref[s * gn:(s + 1) * gn, :] = jnp.concatenate(
            [w_glu if t == s else zero for t in range(S5_SUB)], axis=1).astype(BF16)
    l_re, l_im = power(float(S5_SUB))
    al_ref[...] = jnp.concatenate(
        [jnp.concatenate([l_re, l_re], axis=1), jnp.concatenate([-l_im, l_im], axis=1),
         jnp.zeros((6, 2 * gp), F32)], axis=0)


def _s5_prep(lam_re, lam_im, log_dt, b_re, b_im, c_re, c_im, w_glu):
    gn, gp = S5_GROUPS * S5_CH, S5_GROUPS * S5_STATE
    flat = S5_SUB * gn
    row = lambda a: a.reshape(1, gp)
    full = lambda shape: pl.BlockSpec(shape, lambda i: (0,) * len(shape))
    return pl.pallas_call(
        _s5_prep_kernel,
        out_shape=[
            jax.ShapeDtypeStruct((flat, flat), BF16),
            jax.ShapeDtypeStruct((flat, 2 * gp), BF16),
            jax.ShapeDtypeStruct((flat, 2 * gp), BF16),
            jax.ShapeDtypeStruct((8, 2 * gp), F32),
            jax.ShapeDtypeStruct((flat, flat), BF16),
        ],
        grid=(1,),
        in_specs=[full((1, gp))] * 3 + [full((gn, S5_STATE))] * 4 + [full((gn, gn))],
        out_specs=[full((flat, flat)), full((flat, 2 * gp)), full((flat, 2 * gp)), full((8, 2 * gp)),
                   full((flat, flat))],
        compiler_params=_cparams("arbitrary"),
        name="s5_prep",
    )(row(lam_re), row(lam_im), row(jnp.broadcast_to(log_dt[:, None], lam_re.shape)),
      jnp.swapaxes(b_re, 1, 2).reshape(gn, S5_STATE), jnp.swapaxes(b_im, 1, 2).reshape(gn, S5_STATE),
      c_re.reshape(gn, S5_STATE), c_im.reshape(gn, S5_STATE), w_glu)


def _s5_main_kernel(x_ref, toep_ref, wst_ref, wout_ref, al_ref, glu_ref, d_ref, b_ref, o_ref,
                    tok_ref, inj_ref, prev_ref, state_ref):
    @pl.when(pl.program_id(0) == 0)
    def _():
        state_ref[...] = jnp.zeros(state_ref.shape, F32)

    nb, toks, ch = x_ref.shape
    tc = toks // S5_SUB
    gp = state_ref.shape[1] // 2
    n_lane = ch // LANE
    for j in range(n_lane):
        tok_ref[j] = x_ref[...].reshape(nb * toks, ch)[:, j * LANE:(j + 1) * LANE]
    x = jnp.concatenate(
        [jnp.concatenate([tok_ref[j, pl.ds(S5_SUB * c + s, nb, stride=toks), :]
                          for s in range(S5_SUB) for j in range(n_lane)], axis=1) for c in range(tc)], axis=0)
    xb = x.astype(BF16)
    inj_ref[...] = _dot(xb, wst_ref[...])
    a_c = al_ref[0:1, :]
    a_s = al_ref[1:2, :]
    state = state_ref[...]
    for c in range(tc):
        prev_ref[c * nb:(c + 1) * nb, :] = state
        state = state * a_c + pltpu.roll(state, gp, axis=1) * a_s + inj_ref[c * nb:(c + 1) * nb, :]
    state_ref[...] = state
    y = _dot(xb, toep_ref[...]) + _dot_nt_bf(prev_ref[...], wout_ref[...])
    y = jax.nn.gelu(y + d_ref[...] * x)
    y = y * jax.nn.sigmoid(_dot(y.astype(BF16), glu_ref[...]) + b_ref[...])
    for c in range(tc):
        for s in range(S5_SUB):
            for j in range(n_lane):
                lanes = slice((s * n_lane + j) * LANE, (s * n_lane + j + 1) * LANE)
                tok_ref[j, pl.ds(S5_SUB * c + s, nb, stride=toks), :] = y[c * nb:(c + 1) * nb, lanes]
    o_ref[...] = jnp.concatenate([tok_ref[j] for j in range(n_lane)], axis=1).reshape(nb, toks, ch)


def _s5(u, batch, seq, prm):
    gp = S5_GROUPS * S5_STATE
    flat = S5_SUB * u.shape[1]
    rows = seq // S5_SUB
    tc = next(c for c in (32, 16, 8) if rows % c == 0)
    toep, wst, wout, al, glu = _s5_prep(prm["lam_re"], prm["lam_im"], prm["log_dt"], prm["b_re"], prm["b_im"],
                                        prm["c_re"], prm["c_im"], prm["w_glu"])
    ch = u.shape[1]
    tile = pl.BlockSpec((batch, tc * S5_SUB, ch), lambda i: (0, i, 0))
    y = pl.pallas_call(
        _s5_main_kernel,
        out_shape=jax.ShapeDtypeStruct((batch, seq, ch), F32),
        grid=(rows // tc,),
        in_specs=[tile, _resident((flat, flat)), _resident((flat, 2 * gp)), _resident((flat, 2 * gp)),
                  _resident((8, 2 * gp)), _resident((flat, flat)), _resident((1, flat)), _resident((1, flat))],
        out_specs=tile,
        scratch_shapes=[pltpu.VMEM((ch // LANE, batch * tc * S5_SUB, LANE), F32),
                        pltpu.VMEM((batch * tc, 2 * gp), F32),
                        pltpu.VMEM((batch * tc, 2 * gp), F32),
                        pltpu.VMEM((batch, 2 * gp), F32)],
        compiler_params=_cparams("arbitrary"),
        name="s5_main",
    )(u.reshape(batch, seq, ch), toep, wst, wout, al, glu,
      jnp.tile(prm["d"], (1, S5_SUB)), jnp.tile(prm["b_glu"], (1, S5_SUB)))
    return y.reshape(batch * seq, ch)


def _moba_prep_kernel(q_ref, k_ref, v_ref, pos_ref, inv_ref, qo_ref, ko_ref, vt_ref, sel_ref, km_ref):
    n = pl.program_id(1)
    nbp = km_ref.shape[1]

    @pl.when(n == 0)
    def _():
        km_ref[...] = jnp.zeros(km_ref.shape, F32)

    half = HEAD_DIM // 2
    ang = inv_ref[...] * pos_ref[0, 0]
    cos_t, sin_t = jnp.cos(ang), jnp.sin(ang)
    pairs_per_slab = LANE // HEAD_DIM
    cos = jnp.concatenate([cos_t, cos_t] * pairs_per_slab, axis=0).T
    sin = jnp.concatenate([-sin_t, sin_t] * pairs_per_slab, axis=0).T
    lane = _iota((MOBA_BLOCK, LANE), 1)
    low_half = (lane % HEAD_DIM) < half

    def rope(x):
        swapped = jnp.where(low_half, pltpu.roll(x, LANE - half, axis=1), pltpu.roll(x, half, axis=1))
        return x * cos + swapped * sin

    blk_row = _iota((nbp, MOBA_BLOCK), 0)
    for slab in range(BRANCH // LANE):
        q = rope(q_ref[0, :, slab * LANE:(slab + 1) * LANE])
        k = rope(k_ref[0, :, slab * LANE:(slab + 1) * LANE])
        qo_ref[0, 0, slab] = (q * (HEAD_DIM ** -0.5)).astype(BF16)
        k_mean = jnp.mean(k, axis=0, keepdims=True)
        for sub in range(pairs_per_slab):
            h = slab * pairs_per_slab + sub
            own = (lane // HEAD_DIM) == sub
            gate = _dot_nt_hi(km_ref[h], q)
            rank = jnp.zeros((nbp, MOBA_BLOCK), jnp.int32)
            for m in range(nbp):
                gm = gate[m:m + 1, :]
                beats = (gm > gate) | ((gm == gate) & (m < blk_row))
                rank = rank + jnp.where(beats & (m < n), 1, 0)
            sel_ref[0, 0, h] = ((blk_row < n) & (rank < MOBA_TOPK)).astype(F32)
            ko_ref[0, 0, h] = jnp.where(own, k, 0.0).astype(BF16)
            km_ref[h, pl.ds(n, 1), :] = jnp.where(own[0:1], k_mean, 0.0)
    v_t = v_ref[0].T
    tail = (_iota((VT_ROWS - HEAD_DIM, MOBA_BLOCK), 0) == 0).astype(F32)
    for h in range(HEADS):
        vt_ref[0, 0, h] = jnp.concatenate([v_t[h * HEAD_DIM:(h + 1) * HEAD_DIM], tail], axis=0).astype(BF16)


def _moba_prep(qkv, pos, inv):
    b, s, _ = qkv.shape
    nb = s // MOBA_BLOCK
    nbp = -(-nb // 8) * 8
    slabs = BRANCH // LANE
    blk = lambda j: pl.BlockSpec((1, MOBA_BLOCK, BRANCH), lambda i, n, j=j: (i, n, j))
    return pl.pallas_call(
        _moba_prep_kernel,
        out_shape=[
            jax.ShapeDtypeStruct((b, nb, slabs, MOBA_BLOCK, LANE), BF16),
            jax.ShapeDtypeStruct((b, nb, HEADS, MOBA_BLOCK, LANE), BF16),
            jax.ShapeDtypeStruct((b, nb, HEADS, VT_ROWS, MOBA_BLOCK), BF16),
            jax.ShapeDtypeStruct((b, nb, HEADS, nbp, MOBA_BLOCK), F32),
        ],
        grid=(b, nb),
        in_specs=[blk(0), blk(1), blk(2),
                  pl.BlockSpec((1, 1, 1, MOBA_BLOCK), lambda i, n: (i, n, 0, 0)),
                  pl.BlockSpec((HEAD_DIM // 2, 1), lambda i, n: (0, 0))],
        out_specs=[pl.BlockSpec((1, 1, slabs, MOBA_BLOCK, LANE), lambda i, n: (i, n, 0, 0, 0)),
                   pl.BlockSpec((1, 1, HEADS, MOBA_BLOCK, LANE), lambda i, n: (i, n, 0, 0, 0)),
                   pl.BlockSpec((1, 1, HEADS, VT_ROWS, MOBA_BLOCK), lambda i, n: (i, n, 0, 0, 0)),
                   pl.BlockSpec((1, 1, HEADS, nbp, MOBA_BLOCK), lambda i, n: (i, n, 0, 0, 0))],
        scratch_shapes=[pltpu.VMEM((HEADS, nbp, LANE), F32)],
        compiler_params=_cparams("parallel", "arbitrary"),
        name="moba_prep",
    )(qkv, qkv, qkv, pos, inv)


def _moba_attn_kernel(q_ref, k_ref, vt_ref, sel_ref, o_ref):
    i = pl.program_id(1)
    blk = MOBA_BLOCK
    nbp = sel_ref.shape[3]
    key_i = _iota((blk, blk), 0)
    qry_i = _iota((blk, blk), 1)
    sel_row = _iota((nbp, blk), 0)

    def update(blocks, carries, masks):
        scores = [[lax.dot_general(k_ref[0, n, h], q_ref[0, 0, h * HEAD_DIM // LANE], NT_DIMS,
                                   preferred_element_type=F32) for h in range(HEADS)] for n in blocks]
        stats = []
        for h in range(HEADS):
            m_run = carries[h][0]
            s = [jnp.where(masks[b][h], scores[b][h], -jnp.inf) for b in range(len(blocks))]
            m_new = m_run
            for s_b in s:
                m_new = jnp.maximum(m_new, jnp.max(s_b, axis=0, keepdims=True))
            m_safe = jnp.where(m_new == -jnp.inf, 0.0, m_new)
            p = jnp.concatenate([jnp.exp(s_b - m_safe).astype(BF16) for s_b in s], axis=0)
            stats.append((m_new, jnp.exp(m_run - m_safe), p))
        return tuple((stats[h][0], stats[h][1] * carries[h][1]
                      + _dot(jnp.concatenate([vt_ref[0, n, h] for n in blocks], axis=1), stats[h][2]))
                     for h in range(HEADS))

    def picked(n):
        return [jnp.max(jnp.where(sel_row == n, sel_ref[0, 0, h], 0.0), axis=0, keepdims=True) > 0.0
                for h in range(HEADS)]

    def body(t, carries):
        return update([2 * t, 2 * t + 1], carries, [picked(2 * t), picked(2 * t + 1)])

    init = (jnp.full((1, blk), -jnp.inf, F32), jnp.zeros((VT_ROWS, blk), F32))
    carries = lax.fori_loop(0, i // 2, body, (init,) * HEADS)
    causal = [key_i <= qry_i] * HEADS
    carries = lax.cond((i % 2) == 1,
                       lambda c: update([i - 1, i], c, [picked(i - 1), causal]),
                       lambda c: update([i], c, [causal]), carries)
    o_ref[0] = jnp.concatenate([acc[0:HEAD_DIM] / acc[HEAD_DIM:HEAD_DIM + 1] for _, acc in carries], axis=0).T


def _moba_attn(q, k, vt, sel):
    b, nb = q.shape[:2]
    nbp = sel.shape[3]
    return pl.pallas_call(
        _moba_attn_kernel,
        out_shape=jax.ShapeDtypeStruct((b, nb * MOBA_BLOCK, BRANCH), F32),
        grid=(b, nb),
        in_specs=[
            pl.BlockSpec((1, 1, BRANCH // LANE, MOBA_BLOCK, LANE), lambda i, n: (i, n, 0, 0, 0)),
            pl.BlockSpec((1, nb, HEADS, MOBA_BLOCK, LANE), lambda i, n: (i, 0, 0, 0, 0)),
            pl.BlockSpec((1, nb, HEADS, VT_ROWS, MOBA_BLOCK), lambda i, n: (i, 0, 0, 0, 0)),
            pl.BlockSpec((1, 1, HEADS, nbp, MOBA_BLOCK), lambda i, n: (i, n, 0, 0, 0)),
        ],
        out_specs=pl.BlockSpec((1, MOBA_BLOCK, BRANCH), lambda i, n: (i, n, 0)),
        compiler_params=_cparams("parallel", "arbitrary"),
        name="moba_attn",
    )(q, k, vt, sel)


def _moba(qkv, positions):
    b, s, _ = qkv.shape
    inv = ROPE_THETA ** (-jnp.arange(0, HEAD_DIM, 2, dtype=F32) / HEAD_DIM)
    pos = positions.astype(F32).reshape(b, s // MOBA_BLOCK, 1, MOBA_BLOCK)
    return _moba_attn(*_moba_prep(qkv, pos, inv[:, None]))


def _permute_w_in(w_in):
    cuts = [0]
    for width in (3 * BRANCH, BRANCH, HEADS, HEADS, 2 * BRANCH, BRANCH, BRANCH, HEADS, HEADS, BRANCH, 3 * BRANCH):
        cuts.append(cuts[-1] + width)
    seg = [w_in[:, cuts[n]:cuts[n + 1]] for n in range(len(cuts) - 1)]
    gdn_qkv, gdn_z, gdn_b, gdn_a, ml_qk, ml_v, ml_o, ml_i, ml_f, s5_u, moba_qkv = seg
    pad = jnp.zeros((w_in.shape[0], SMALL_W - 4 * HEADS), w_in.dtype)
    return jnp.concatenate([gdn_qkv, gdn_z, ml_qk, ml_v, ml_o, s5_u, moba_qkv, gdn_b, gdn_a, ml_i, ml_f, pad], axis=1)


def _token_tile(t, largest=512):
    for tm in (1024, 512, 256, 128, 64, 32, 16, 8):
        if tm > largest:
            continue
        if t % tm == 0:
            return tm
    raise ValueError(f"token count {t} is not a multiple of 8")


def _chunks_per_step(seq, most=8):
    n_chunks = seq // CHUNK
    for cps in (16, 8, 4, 2, 1):
        if cps <= most and n_chunks % cps == 0:
            return cps


def _ff_slice(f):
    for sw in (2 * LANE, LANE):
        if f % sw == 0:
            return sw
    raise ValueError(f"ffn width {f} is not a multiple of {LANE}")


def kernel(x, p, positions, ffn1_norm, ffn1_w_gu, ffn1_w_down, mix_norm, w_in, gdn_conv, gdn_a_log, gdn_dt_bias, gdn_norm, mlstm_conv, mlstm_i_bias, mlstm_f_bias, mlstm_norm, s5_lambda_re, s5_lambda_im, s5_b_re, s5_b_im, s5_c_re, s5_c_im, s5_d, s5_log_dt, s5_w_glu, s5_b_glu, w_gate, w_branch, w_out, ffn2_norm, ffn2_w_gu, ffn2_w_down, ple_norm, ple_w_proj, ple_w_gate, final_norm):
    batch, seq, d = x.shape
    depth = p.shape[0]
    t = batch * seq
    tm = _token_tile(t)
    tm_ffn = _token_tile(t, largest=1024)
    tf = _ff_slice(ffn1_w_down.shape[1])
    cps = _chunks_per_step(seq)
    bf = lambda a: a.astype(BF16)
    row = lambda a: a.reshape(1, -1)
    seq3 = lambda a: a.reshape(batch, seq, a.shape[-1])

    h = x.reshape(t, d)
    for i in range(depth):
        h = _ffn(h, row(ffn1_norm[i]), bf(ffn1_w_gu[i]), bf(ffn1_w_down[i]), tm_ffn, tf)

        gdn_qkv, gdn_z, ml_qk, ml_v, ml_o, s5_u, moba_qkv, small = _inproj(
            h, row(mix_norm[i]), bf(_permute_w_in(w_in[i])), tm)
        y_gdn = _gdn(seq3(gdn_qkv), seq3(gdn_z), seq3(small), gdn_conv[i], row(gdn_a_log[i]),
                     row(gdn_dt_bias[i]), row(gdn_norm[i]), cps)
        y_mlstm = _mlstm(seq3(ml_qk), seq3(ml_v), seq3(ml_o), seq3(small), mlstm_conv[i],
                         row(mlstm_i_bias[i]), row(mlstm_f_bias[i]), row(mlstm_norm[i]),
                         _chunks_per_step(seq, most=16))
        s5_prm = dict(lam_re=s5_lambda_re[i], lam_im=s5_lambda_im[i], log_dt=s5_log_dt[i],
                      b_re=s5_b_re[i], b_im=s5_b_im[i], c_re=s5_c_re[i], c_im=s5_c_im[i],
                      d=row(s5_d[i]), w_glu=s5_w_glu[i], b_glu=row(s5_b_glu[i]))
        y_s5 = _s5(s5_u, batch, seq, s5_prm)
        y_moba = _moba(seq3(moba_qkv), positions)
        ys = (y_gdn.reshape(t, BRANCH), y_mlstm.reshape(t, BRANCH), y_s5, y_moba.reshape(t, BRANCH))
        h = _merge(h, row(mix_norm[i]), ys, bf(w_gate[i]), bf(w_branch[i]), bf(w_out[i]), tm_ffn)

        ple = (row(ple_norm[i]), p[i].reshape(t, -1), bf(ple_w_proj[i]), bf(ple_w_gate[i]), row(final_norm))
        h = _ffn(h, row(ffn2_norm[i]), bf(ffn2_w_gu[i]), bf(ffn2_w_down[i]), tm_ffn, tf, ple, i == depth - 1)
    return h.reshape(batch, seq, d)
```
